```python
import math
import jax, jax.numpy as jnp
from jax import lax
import numpy as np

D_MODEL = 1024
BATCH = 2
SEQ = 8192
DEPTH = 2

CHUNK = 64
N_MIXERS = 2

SG_FFN = 6 * D_MODEL
SG_WIDTH = SG_FFN // 2
SG_BLOCK = 128
SG_GROUPS = 8
SG_GC = SG_WIDTH // SG_GROUPS

GLA_HEADS = 4
GLA_KD = D_MODEL // 2
GLA_VD = D_MODEL
GLA_DK = GLA_KD // GLA_HEADS
GLA_DV = GLA_VD // GLA_HEADS
GLA_RANK = 16
GLA_TAU = 16.0
GLA_IN = 2 * GLA_KD + 2 * GLA_VD + GLA_RANK

FFN_HIDDEN = ((8 * D_MODEL + 3 * 256 - 1) // (3 * 256)) * 256

EPS = 1e-6

kernel_name = "hybrid_gmlp_gla_chunk_causal_trunk"


def rmsnorm(x, g):
    xf = x.astype(jnp.float32)
    y = xf * lax.rsqrt(jnp.mean(xf * xf, axis=-1, keepdims=True) + EPS)
    return (y * g.astype(jnp.float32)).astype(x.dtype)


def layernorm(x, g, b):
    xf = x.astype(jnp.float32)
    mu = jnp.mean(xf, axis=-1, keepdims=True)
    xc = xf - mu
    y = xc * lax.rsqrt(jnp.mean(xc * xc, axis=-1, keepdims=True) + EPS)
    return (y * g.astype(jnp.float32) + b.astype(jnp.float32)).astype(x.dtype)


def sgu_mixer(h, w_in, ln_g, ln_b, w_s, b_s, w_out):
    B_, T, _ = h.shape
    z = jax.nn.gelu(h @ w_in, approximate=False)
    u, v = jnp.split(z, 2, axis=-1)
    v = layernorm(v, ln_g, ln_b)
    n = T // SG_BLOCK
    v = v.reshape(B_, n, SG_BLOCK, SG_GROUPS, SG_GC)
    chunk_id = jnp.arange(SG_BLOCK) // CHUNK
    mask = chunk_id[:, None] >= chunk_id[None, :]
    ws = jnp.where(mask[None], w_s, jnp.zeros((), w_s.dtype))
    s = jnp.einsum('gts,bnsgc->bntgc', ws, v) + b_s.T[None, None, :, :, None]
    y = u * s.reshape(B_, T, SG_WIDTH)
    return y @ w_out


def gla_mixer(h, w_in, w_g2, b_g, gn_g, w_out):
    B_, T, _ = h.shape
    proj = h @ w_in
    q, k, v, r, gz = jnp.split(
        proj, [GLA_KD, 2 * GLA_KD, 2 * GLA_KD + GLA_VD, 2 * GLA_KD + 2 * GLA_VD], axis=-1)
    log_a = jax.nn.log_sigmoid((gz @ w_g2 + b_g).astype(jnp.float32)) / GLA_TAU
    n = T // CHUNK

    def to_chunks(z, d):
        return z.astype(jnp.float32).reshape(B_, n, CHUNK, GLA_HEADS, d).transpose(1, 0, 3, 2, 4)

    qc = to_chunks(q, GLA_DK) * (GLA_DK ** -0.5)
    kc = to_chunks(k, GLA_DK)
    vc = to_chunks(v, GLA_DV)
    Gc = jnp.cumsum(to_chunks(log_a, GLA_DK), axis=3)
    causal = jnp.tril(jnp.ones((CHUNK, CHUNK), dtype=bool))

    def step(S, inp):
        qi, ki, vi, Gi = inp
        diff = Gi[:, :, :, None, :] - Gi[:, :, None, :, :]
        decay = jnp.exp(jnp.where(causal[None, None, :, :, None], diff, -jnp.inf))
        A = jnp.sum(qi[:, :, :, None, :] * ki[:, :, None, :, :] * decay, axis=-1)
        o = jnp.einsum('bhij,bhjv->bhiv', A, vi) + jnp.einsum('bhik,bhkv->bhiv', qi * jnp.exp(Gi), S)
        G_last = Gi[:, :, -1:, :]
        S = jnp.exp(G_last[:, :, 0, :])[..., None] * S + jnp.einsum(
            'bhjk,bhjv->bhkv', ki * jnp.exp(G_last - Gi), vi)
        return S, o

    S0 = jnp.zeros((B_, GLA_HEADS, GLA_DK, GLA_DV), jnp.float32)
    _, o = lax.scan(step, S0, (qc, kc, vc, Gc))
    o = o.transpose(1, 0, 3, 2, 4).reshape(B_, T, GLA_HEADS, GLA_DV)
    o = o * lax.rsqrt(jnp.mean(o * o, axis=-1, keepdims=True) + EPS) * gn_g.astype(jnp.float32)
    o = o.reshape(B_, T, GLA_VD).astype(h.dtype)
    return (jax.nn.silu(r) * o) @ w_out


def swiglu(h, w_gate, w_up, w_down):
    return (jax.nn.silu(h @ w_gate) * (h @ w_up)) @ w_down


def setup_inputs(seed: int = 0) -> dict:
    key = jax.random.key(seed)
    ks = jax.random.split(key, 24)
    na = (DEPTH + 1) // 2
    nb = DEPTH // 2
    f32 = jnp.float32

    def w(k, shape, fan_in):
        return jax.random.normal(k, shape, f32) * (fan_in ** -0.5)

    def gain(k, shape):
        return 1.0 + 0.05 * jax.random.normal(k, shape, f32)

    return {
        "x": jax.random.normal(ks[0], (BATCH, SEQ, D_MODEL), f32),
        "mix_norm": gain(ks[1], (DEPTH, D_MODEL)),
        "a_w_in": w(ks[2], (na, D_MODEL, SG_FFN), D_MODEL),
        "a_ln_g": gain(ks[3], (na, SG_WIDTH)),
        "a_ln_b": 0.02 * jax.random.normal(ks[4], (na, SG_WIDTH), f32),
        "a_w_s": w(ks[5], (na, SG_GROUPS, SG_BLOCK, SG_BLOCK), SG_BLOCK),
        "a_b_s": gain(ks[6], (na, SG_GROUPS, SG_BLOCK)),
        "a_w_out": w(ks[7], (na, SG_WIDTH, D_MODEL), SG_WIDTH),
        "b_w_in": w(ks[8], (nb, D_MODEL, GLA_IN), D_MODEL),
        "b_w_g2": w(ks[9], (nb, GLA_RANK, GLA_KD), GLA_RANK),
        "b_b_g": 0.01 * jax.random.normal(ks[10], (nb, GLA_KD), f32),
        "b_gn_g": gain(ks[11], (nb, GLA_HEADS, GLA_DV)),
        "b_w_out": w(ks[12], (nb, GLA_VD, D_MODEL), GLA_VD),
        "ffn_norm": gain(ks[13], (DEPTH, D_MODEL)),
        "ffn_w_gate": w(ks[14], (DEPTH, D_MODEL, FFN_HIDDEN), D_MODEL),
        "ffn_w_up": w(ks[15], (DEPTH, D_MODEL, FFN_HIDDEN), D_MODEL),
        "ffn_w_down": w(ks[16], (DEPTH, FFN_HIDDEN, D_MODEL), FFN_HIDDEN),
        "final_norm": gain(ks[17], (D_MODEL,)),
    }


def reference(x, mix_norm, a_w_in, a_ln_g, a_ln_b, a_w_s, a_b_s, a_w_out,
              b_w_in, b_w_g2, b_b_g, b_gn_g, b_w_out,
              ffn_norm, ffn_w_gate, ffn_w_up, ffn_w_down, final_norm):
    for i in range(DEPTH):
        h = rmsnorm(x, mix_norm[i])
        j = i // N_MIXERS
        if i % N_MIXERS == 0:
            x = x + sgu_mixer(h, a_w_in[j], a_ln_g[j], a_ln_b[j], a_w_s[j], a_b_s[j], a_w_out[j])
        else:
            x = x + gla_mixer(h, b_w_in[j], b_w_g2[j], b_b_g[j], b_gn_g[j], b_w_out[j])
        h = rmsnorm(x, ffn_norm[i])
        x = x + swiglu(h, ffn_w_gate[i], ffn_w_up[i], ffn_w_down[i])
    return rmsnorm(x, final_norm)
```

```python
import functools

import jax
import jax.numpy as jnp
from jax import lax
from jax.experimental import pallas as pl
from jax.experimental.pallas import tpu as pltpu

F32 = jnp.float32
BF16 = jnp.bfloat16

EPS = 1e-6
CHUNK = 64
SG_BLOCK = 128
SG_GROUPS = 8
GLA_HEADS = 4
GLA_RANK = 16
GLA_TAU = 16.0

VMEM_LIMIT_BYTES = 56 * 1024 * 1024


def _rms(x, g):
    return x * lax.rsqrt(jnp.mean(x * x, axis=-1, keepdims=True) + EPS) * g


def _gelu(x):
    return 0.5 * x * (1.0 + lax.erf(x * (2.0 ** -0.5)))


def _dot(a, b):
    return jnp.dot(a, b, preferred_element_type=F32)


def _dot_nt(a, b):
    return lax.dot_general(a, b, (((1,), (1,)), ((), ())), preferred_element_type=F32)


def _dot_tn(a, b):
    return lax.dot_general(a, b, (((0,), (0,)), ((), ())), preferred_element_type=F32)


def _resident(shape):
    return pl.BlockSpec(shape, lambda *_: (0,) * len(shape))


def _ffn_kernel(x_ref, g_ref, wg_ref, wu_ref, wd_ref, fin_ref, o_ref, *, final_norm):
    x = x_ref[...]
    h = _rms(x, g_ref[...]).astype(BF16)
    gate = _dot(h, wg_ref[...])
    up = _dot(h, wu_ref[...])
    a = (jax.nn.silu(gate) * up).astype(BF16)
    y = x + _dot(a, wd_ref[...])
    if final_norm:
        y = _rms(y, fin_ref[...])
    o_ref[...] = y


def _ffn(x2, g, wg, wu, wd, fin, *, final_norm, tm):
    m, d = x2.shape
    hid = wg.shape[1]
    return pl.pallas_call(
        functools.partial(_ffn_kernel, final_norm=final_norm),
        grid=(m // tm,),
        in_specs=[
            pl.BlockSpec((tm, d), lambda i: (i, 0)),
            _resident((1, d)),
            _resident((d, hid)),
            _resident((d, hid)),
            _resident((hid, d)),
            _resident((1, d)),
        ],
        out_specs=pl.BlockSpec((tm, d), lambda i: (i, 0)),
        out_shape=jax.ShapeDtypeStruct((m, d), F32),
        compiler_params=pltpu.CompilerParams(
            dimension_semantics=("arbitrary",), vmem_limit_bytes=VMEM_LIMIT_BYTES),
        name="swiglu",
    )(x2, g, wg, wu, wd, fin)


def _sgu_kernel(x_ref, g_ref, win_ref, lng_ref, lnb_ref, ws_ref, bst_ref, wout_ref,
                o_ref, u_s, vn_s, y_s, *, tm):
    width = u_s.shape[1]
    gc = width // SG_GROUPS
    x = x_ref[...]
    h = _rms(x, g_ref[...]).astype(BF16)
    u_s[...] = _gelu(_dot(h, win_ref[:, :width]))
    v = _gelu(_dot(h, win_ref[:, width:]))
    mu = jnp.mean(v, axis=-1, keepdims=True)
    vc = v - mu
    rstd = lax.rsqrt(jnp.mean(vc * vc, axis=-1, keepdims=True) + EPS)
    vn_s[...] = (vc * rstd * lng_ref[...] + lnb_ref[...]).astype(BF16)

    ti = lax.broadcasted_iota(jnp.int32, (SG_BLOCK, SG_BLOCK), 0) // CHUNK
    si = lax.broadcasted_iota(jnp.int32, (SG_BLOCK, SG_BLOCK), 1) // CHUNK
    causal = ti >= si
    for g in range(SG_GROUPS):
        ws = jnp.where(causal, ws_ref[g], 0.0).astype(BF16)
        bias = bst_ref[:, g:g + 1]
        cols = slice(g * gc, (g + 1) * gc)
        for w in range(tm // SG_BLOCK):
            rows = slice(w * SG_BLOCK, (w + 1) * SG_BLOCK)
            s = _dot(ws, vn_s[rows, cols]) + bias
            y_s[rows, cols] = (u_s[rows, cols] * s).astype(BF16)
    o_ref[...] = x + _dot(y_s[...], wout_ref[...])


def _sgu(x2, g, win, lng, lnb, ws, bst, wout, *, tm):
    m, d = x2.shape
    width = wout.shape[0]
    return pl.pallas_call(
        functools.partial(_sgu_kernel, tm=tm),
        grid=(m // tm,),
        in_specs=[
            pl.BlockSpec((tm, d), lambda i: (i, 0)),
            _resident((1, d)),
            _resident((d, 2 * width)),
            _resident((1, width)),
            _resident((1, width)),
            _resident(ws.shape),
            _resident(bst.shape),
            _resident((width, d)),
        ],
        out_specs=pl.BlockSpec((tm, d), lambda i: (i, 0)),
        out_shape=jax.ShapeDtypeStruct((m, d), F32),
        scratch_shapes=[
            pltpu.VMEM((tm, width), F32),
            pltpu.VMEM((tm, width), BF16),
            pltpu.VMEM((tm, width), BF16),
        ],
        compiler_params=pltpu.CompilerParams(
            dimension_semantics=("arbitrary",), vmem_limit_bytes=VMEM_LIMIT_BYTES),
        name="sgu",
    )(x2, g, win, lng, lnb, ws, bst, wout)


def _ref_rows(G, half):
    n, w = G.shape
    blk = 2 * half
    if blk >= 8:
        return jnp.concatenate(
            [jnp.broadcast_to(G[b * blk + half - 1:b * blk + half, :], (blk, w))
             for b in range(n // blk)], axis=0)
    assert blk == 4
    sub = lax.broadcasted_iota(jnp.int32, (8, w), 0)
    tiles = []
    for t in range(n // 8):
        lo = jnp.broadcast_to(G[8 * t + 1:8 * t + 2, :], (8, w))
        hi = jnp.broadcast_to(G[8 * t + 5:8 * t + 6, :], (8, w))
        tiles.append(jnp.where(sub < 4, lo, hi))
    return jnp.concatenate(tiles, axis=0)


def _gla_kernel(x_ref, g_ref, win_ref, wgz_ref, wg2_ref, bg_ref, gn_ref, wout_ref,
                o_ref, proj_s, la_s, og_s, st_s, *, tm, kd, vd):
    dk = kd // GLA_HEADS
    dv = vd // GLA_HEADS
    C = CHUNK

    @pl.when(pl.program_id(1) == 0)
    def _():
        st_s[...] = jnp.zeros_like(st_s)

    x = x_ref[0]
    h = _rms(x, g_ref[...]).astype(BF16)
    proj_s[...] = _dot(h, win_ref[...])
    gz = _dot(h, wgz_ref[...])
    z = _dot(gz.astype(BF16), wg2_ref[...]) + bg_ref[...]
    la_s[...] = (jnp.minimum(z, 0.0) - jnp.log1p(jnp.exp(-jnp.abs(z)))) / GLA_TAU

    ri = lax.broadcasted_iota(jnp.int32, (C, C), 0)
    ci = lax.broadcasted_iota(jnp.int32, (C, C), 1)
    tri = (ri >= ci).astype(BF16)
    halves = (32, 16, 8, 4, 2, 1)
    same_blk = [(ri // (2 * hf)) == (ci // (2 * hf)) for hf in halves]
    diag = ri == ci
    rowc = lax.broadcasted_iota(jnp.int32, (C, kd), 0)
    upper = [(rowc % (2 * hf)) >= hf for hf in halves]
    scale = dk ** -0.5

    for c in range(tm // C):
        rows = slice(c * C, (c + 1) * C)
        q = proj_s[rows, 0:kd] * scale
        k = proj_s[rows, kd:2 * kd]
        la = la_s[rows, :]
        la_hi = la.astype(BF16)
        r1 = la - la_hi.astype(F32)
        la_mid = r1.astype(BF16)
        la_lo = (r1 - la_mid.astype(F32)).astype(BF16)
        G = _dot(tri, la_hi) + _dot(tri, la_mid) + _dot(tri, la_lo)

        qs, ks = [], []
        for lvl, hf in enumerate(halves):
            if hf == 1:
                e = jnp.exp(la)
            else:
                e = jnp.exp(-jnp.abs(G - _ref_rows(G, hf)))
            qs.append(jnp.where(upper[lvl], q * e, 0.0).astype(BF16))
            ks.append(jnp.where(upper[lvl], 0.0, k * e).astype(BF16))
        qb = q.astype(BF16)
        kb = k.astype(BF16)
        g_last = G[C - 1:C, :]
        qd = (q * jnp.exp(G)).astype(BF16)
        kdec = (k * jnp.exp(g_last - G)).astype(BF16)
        dec = jnp.exp(g_last)

        for hd in range(GLA_HEADS):
            ks_ = slice(hd * dk, (hd + 1) * dk)
            vs_ = slice(2 * kd + hd * dv, 2 * kd + (hd + 1) * dv)
            rs_ = slice(2 * kd + vd + hd * dv, 2 * kd + vd + (hd + 1) * dv)
            a = jnp.where(diag, _dot_nt(qb[:, ks_], kb[:, ks_]), 0.0)
            for lvl in range(len(halves)):
                p = _dot_nt(qs[lvl][:, ks_], ks[lvl][:, ks_])
                a = a + jnp.where(same_blk[lvl], p, 0.0)
            v = proj_s[rows, vs_].astype(BF16)
            st = st_s[hd]
            o = _dot(a.astype(BF16), v) + _dot_nt(qd[:, ks_], st.astype(BF16))
            st_s[hd] = st * dec[:, ks_] + _dot_tn(v, kdec[:, ks_])
            o = o * lax.rsqrt(jnp.mean(o * o, axis=-1, keepdims=True) + EPS) * gn_ref[hd:hd + 1, :]
            r = proj_s[rows, rs_]
            og_s[rows, hd * dv:(hd + 1) * dv] = (jax.nn.silu(r) * o).astype(BF16)

    o_ref[0] = x + _dot(og_s[...], wout_ref[...])


def _gla(x, g, win, wgz, wg2, bg, gn, wout, *, tm):
    b, t, d = x.shape
    kd = wg2.shape[1]
    vd = wout.shape[0]
    dk = kd // GLA_HEADS
    dv = vd // GLA_HEADS
    nproj = win.shape[1]
    return pl.pallas_call(
        functools.partial(_gla_kernel, tm=tm, kd=kd, vd=vd),
        grid=(b, t // tm),
        in_specs=[
            pl.BlockSpec((1, tm, d), lambda i, j: (i, j, 0)),
            _resident((1, d)),
            _resident((d, nproj)),
            _resident(wgz.shape),
            _resident(wg2.shape),
            _resident((1, kd)),
            _resident(gn.shape),
            _resident((vd, d)),
        ],
        out_specs=pl.BlockSpec((1, tm, d), lambda i, j: (i, j, 0)),
        out_shape=jax.ShapeDtypeStruct((b, t, d), F32),
        scratch_shapes=[
            pltpu.VMEM((tm, nproj), F32),
            pltpu.VMEM((tm, kd), F32),
            pltpu.VMEM((tm, vd), BF16),
            pltpu.VMEM((GLA_HEADS, dv, dk), F32),
        ],
        compiler_params=pltpu.CompilerParams(
            dimension_semantics=("arbitrary", "arbitrary"), vmem_limit_bytes=VMEM_LIMIT_BYTES),
        name="gla",
    )(x, g, win, wgz, wg2, bg, gn, wout)


def kernel(x, mix_norm, a_w_in, a_ln_g, a_ln_b, a_w_s, a_b_s, a_w_out, b_w_in, b_w_g2, b_b_g, b_gn_g, b_w_out, ffn_norm, ffn_w_gate, ffn_w_up, ffn_w_down, final_norm):
    b, t, d = x.shape
    depth = mix_norm.shape[0]
    assert depth == 2 and a_w_in.shape[0] == 1 and b_w_in.shape[0] == 1
    m = b * t
    bf = lambda w: w.astype(BF16)
    row = lambda v: v.reshape(1, -1)
    nqkvr = b_w_in.shape[2] - GLA_RANK

    x2 = x.reshape(m, d)
    x2 = _sgu(x2, row(mix_norm[0]), bf(a_w_in[0]), row(a_ln_g[0]), row(a_ln_b[0]),
              a_w_s[0], a_b_s[0].T, bf(a_w_out[0]), tm=256)
    x2 = _ffn(x2, row(ffn_norm[0]), bf(ffn_w_gate[0]), bf(ffn_w_up[0]), bf(ffn_w_down[0]),
              row(final_norm), final_norm=False, tm=512)
    x3 = _gla(x2.reshape(b, t, d), row(mix_norm[1]), bf(b_w_in[0, :, :nqkvr]),
              bf(b_w_in[0, :, nqkvr:]), bf(b_w_g2[0]), row(b_b_g[0]), b_gn_g[0],
              bf(b_w_out[0]), tm=256)
    x2 = _ffn(x3.reshape(m, d), row(ffn_norm[1]), bf(ffn_w_gate[1]), bf(ffn_w_up[1]),
              bf(ffn_w_down[1]), row(final_norm), final_norm=True, tm=512)
    return x2.reshape(b, t, d)
```

```python
import functools

import jax
import jax.numpy as jnp
from jax import lax
from jax.experimental import pallas as pl
from jax.experimental.pallas import tpu as pltpu

F32 = jnp.float32
BF16 = jnp.bfloat16

EPS = 1e-6
CHUNK = 64
SG_BLOCK = 128
SG_GROUPS = 8
SG_GROUPS_PER_CHUNK = 2
GLA_HEADS = 4
GLA_RANK = 16
GLA_TAU = 16.0

VMEM_LIMIT_BYTES = 56 * 1024 * 1024


def _rms(x, g):
    return x * lax.rsqrt(jnp.mean(x * x, axis=-1, keepdims=True) + EPS) * g


def _gelu(x):
    return 0.5 * x * (1.0 + lax.erf(x * (2.0 ** -0.5)))


def _dot(a, b):
    return jnp.dot(a, b, preferred_element_type=F32)


def _dot_nt(a, b):
    return lax.dot_general(a, b, (((1,), (1,)), ((), ())), preferred_element_type=F32)


def _dot_tn(a, b):
    return lax.dot_general(a, b, (((0,), (0,)), ((), ())), preferred_element_type=F32)


def _resident(shape):
    return pl.BlockSpec(shape, lambda *_: (0,) * len(shape))


def _ffn_kernel(x_ref, g_ref, wg_ref, wu_ref, wd_ref, fin_ref, o_ref, *, final_norm):
    x = x_ref[...]
    h = _rms(x, g_ref[...]).astype(BF16)
    gate = _dot(h, wg_ref[...])
    up = _dot(h, wu_ref[...])
    a = (jax.nn.silu(gate) * up).astype(BF16)
    y = x + _dot(a, wd_ref[...])
    if final_norm:
        y = _rms(y, fin_ref[...])
    o_ref[...] = y


def _ffn(x2, g, wg, wu, wd, fin, *, final_norm, tm):
    m, d = x2.shape
    hid = wg.shape[1]
    return pl.pallas_call(
        functools.partial(_ffn_kernel, final_norm=final_norm),
        grid=(m // tm,),
        in_specs=[
            pl.BlockSpec((tm, d), lambda i: (i, 0)),
            _resident((1, d)),
            _resident((d, hid)),
            _resident((d, hid)),
            _resident((hid, d)),
            _resident((1, d)),
        ],
        out_specs=pl.BlockSpec((tm, d), lambda i: (i, 0)),
        out_shape=jax.ShapeDtypeStruct((m, d), F32),
        compiler_params=pltpu.CompilerParams(
            dimension_semantics=("arbitrary",), vmem_limit_bytes=VMEM_LIMIT_BYTES),
        name="swiglu",
    )(x2, g, wg, wu, wd, fin)


def _sgu_kernel(x_ref, g_ref, win_ref, lng_ref, lnb_ref, ws_ref, bst_ref, wout_ref,
                o_ref, v_s, y_s, *, tm):
    width = v_s.shape[1]
    gc = width // SG_GROUPS
    lanes = 128
    x = x_ref[...]
    h = _rms(x, g_ref[...]).astype(BF16)

    gpc = SG_GROUPS_PER_CHUNK
    cw = gpc * gc
    p1 = jnp.zeros((tm, lanes), F32)
    p2 = jnp.zeros((tm, lanes), F32)
    for c in range(SG_GROUPS // gpc):
        v = _gelu(_dot(h, win_ref[:, width + c * cw:width + (c + 1) * cw]))
        v_s[:, c * cw:(c + 1) * cw] = v
        for l in range(cw // lanes):
            vl = v[:, l * lanes:(l + 1) * lanes]
            p1 = p1 + vl
            p2 = p2 + vl * vl
    mu = jnp.sum(p1, axis=-1, keepdims=True) * (1.0 / width)
    var = jnp.sum(p2, axis=-1, keepdims=True) * (1.0 / width) - mu * mu
    rstd = lax.rsqrt(var + EPS)
    shift = -mu * rstd

    ti = lax.broadcasted_iota(jnp.int32, (SG_BLOCK, SG_BLOCK), 0) // CHUNK
    si = lax.broadcasted_iota(jnp.int32, (SG_BLOCK, SG_BLOCK), 1) // CHUNK
    causal = ti >= si
    for c in range(SG_GROUPS // gpc):
        ccols = slice(c * cw, (c + 1) * cw)
        u = _gelu(_dot(h, win_ref[:, ccols]))
        vn = ((v_s[:, ccols] * rstd + shift) * lng_ref[:, ccols] + lnb_ref[:, ccols]).astype(BF16)
        for gg in range(gpc):
            g = c * gpc + gg
            gcols = slice(gg * gc, (gg + 1) * gc)
            ws = jnp.where(causal, ws_ref[g], 0.0).astype(BF16)
            bias = bst_ref[:, g:g + 1]
            for w in range(tm // SG_BLOCK):
                rows = slice(w * SG_BLOCK, (w + 1) * SG_BLOCK)
                s = _dot(ws, vn[rows, gcols]) + bias
                y_s[rows, g * gc:(g + 1) * gc] = (u[rows, gcols] * s).astype(BF16)
    o_ref[...] = x + _dot(y_s[...], wout_ref[...])


def _sgu(x2, g, win, lng, lnb, ws, bst, wout, *, tm):
    m, d = x2.shape
    width = wout.shape[0]
    return pl.pallas_call(
        functools.partial(_sgu_kernel, tm=tm),
        grid=(m // tm,),
        in_specs=[
            pl.BlockSpec((tm, d), lambda i: (i, 0)),
            _resident((1, d)),
            _resident((d, 2 * width)),
            _resident((1, width)),
            _resident((1, width)),
            _resident(ws.shape),
            _resident(bst.shape),
            _resident((width, d)),
        ],
        out_specs=pl.BlockSpec((tm, d), lambda i: (i, 0)),
        out_shape=jax.ShapeDtypeStruct((m, d), F32),
        scratch_shapes=[
            pltpu.VMEM((tm, width), F32),
            pltpu.VMEM((tm, width), BF16),
        ],
        compiler_params=pltpu.CompilerParams(
            dimension_semantics=("arbitrary",), vmem_limit_bytes=VMEM_LIMIT_BYTES),
        name="sgu",
    )(x2, g, win, lng, lnb, ws, bst, wout)


def _ref_rows(G, half):
    n, w = G.shape
    blk = 2 * half
    if blk >= 8:
        return jnp.concatenate(
            [jnp.broadcast_to(G[b * blk + half - 1:b * blk + half, :], (blk, w))
             for b in range(n // blk)], axis=0)
    assert blk == 4
    sub = lax.broadcasted_iota(jnp.int32, (8, w), 0)
    tiles = []
    for t in range(n // 8):
        lo = jnp.broadcast_to(G[8 * t + 1:8 * t + 2, :], (8, w))
        hi = jnp.broadcast_to(G[8 * t + 5:8 * t + 6, :], (8, w))
        tiles.append(jnp.where(sub < 4, lo, hi))
    return jnp.concatenate(tiles, axis=0)


def _gla_kernel(x_ref, g_ref, win_ref, wgz_ref, wg2_ref, bg_ref, gn_ref, wout_ref,
                o_ref, proj_s, la_s, og_s, st_s, *, tm, kd, vd):
    dk = kd // GLA_HEADS
    dv = vd // GLA_HEADS
    C = CHUNK

    @pl.when(pl.program_id(1) == 0)
    def _():
        st_s[...] = jnp.zeros_like(st_s)

    x = x_ref[0]
    h = _rms(x, g_ref[...]).astype(BF16)
    proj_s[...] = _dot(h, win_ref[...])
    gz = _dot(h, wgz_ref[...])
    z = _dot(gz.astype(BF16), wg2_ref[...]) + bg_ref[...]
    la_s[...] = (jnp.minimum(z, 0.0) - jnp.log1p(jnp.exp(-jnp.abs(z)))) / GLA_TAU

    ri = lax.broadcasted_iota(jnp.int32, (C, C), 0)
    ci = lax.broadcasted_iota(jnp.int32, (C, C), 1)
    tri = (ri >= ci).astype(BF16)
    halves = (32, 16, 8, 4, 2, 1)
    same_blk = [(ri // (2 * hf)) == (ci // (2 * hf)) for hf in halves]
    diag = ri == ci
    rowc = lax.broadcasted_iota(jnp.int32, (C, kd), 0)
    upper = [(rowc % (2 * hf)) >= hf for hf in halves]
    scale = dk ** -0.5

    for c in range(tm // C):
        rows = slice(c * C, (c + 1) * C)
        q = proj_s[rows, 0:kd] * scale
        k = proj_s[rows, kd:2 * kd]
        la = la_s[rows, :]
        la_hi = la.astype(BF16)
        r1 = la - la_hi.astype(F32)
        la_mid = r1.astype(BF16)
        la_lo = (r1 - la_mid.astype(F32)).astype(BF16)
        G = _dot(tri, la_hi) + _dot(tri, la_mid) + _dot(tri, la_lo)

        qs, ks = [], []
        for lvl, hf in enumerate(halves):
            if hf == 1:
                e = jnp.exp(la)
            else:
                e = jnp.exp(-jnp.abs(G - _ref_rows(G, hf)))
            qs.append(jnp.where(upper[lvl], q * e, 0.0).astype(BF16))
            ks.append(jnp.where(upper[lvl], 0.0, k * e).astype(BF16))
        qb = q.astype(BF16)
        kb = k.astype(BF16)
        g_last = G[C - 1:C, :]
        qd = (q * jnp.exp(G)).astype(BF16)
        kdec = (k * jnp.exp(g_last - G)).astype(BF16)
        dec = jnp.exp(g_last)

        for hd in range(GLA_HEADS):
            ks_ = slice(hd * dk, (hd + 1) * dk)
            vs_ = slice(2 * kd + hd * dv, 2 * kd + (hd + 1) * dv)
            rs_ = slice(2 * kd + vd + hd * dv, 2 * kd + vd + (hd + 1) * dv)
            a = jnp.where(diag, _dot_nt(qb[:, ks_], kb[:, ks_]), 0.0)
            for lvl in range(len(halves)):
                p = _dot_nt(qs[lvl][:, ks_], ks[lvl][:, ks_])
                a = a + jnp.where(same_blk[lvl], p, 0.0)
            v = proj_s[rows, vs_].astype(BF16)
            st = st_s[hd]
            o = _dot(a.astype(BF16), v) + _dot_nt(qd[:, ks_], st.astype(BF16))
            st_s[hd] = st * dec[:, ks_] + _dot_tn(v, kdec[:, ks_])
            o = o * lax.rsqrt(jnp.mean(o * o, axis=-1, keepdims=True) + EPS) * gn_ref[hd:hd + 1, :]
            r = proj_s[rows, rs_]
            og_s[rows, hd * dv:(hd + 1) * dv] = (jax.nn.silu(r) * o).astype(BF16)

    o_ref[0] = x + _dot(og_s[...], wout_ref[...])


def _gla(x, g, win, wgz, wg2, bg, gn, wout, *, tm):
    b, t, d = x.shape
    kd = wg2.shape[1]
    vd = wout.shape[0]
    dk = kd // GLA_HEADS
    dv = vd // GLA_HEADS
    nproj = win.shape[1]
    return pl.pallas_call(
        functools.partial(_gla_kernel, tm=tm, kd=kd, vd=vd),
        grid=(b, t // tm),
        in_specs=[
            pl.BlockSpec((1, tm, d), lambda i, j: (i, j, 0)),
            _resident((1, d)),
            _resident((d, nproj)),
            _resident(wgz.shape),
            _resident(wg2.shape),
            _resident((1, kd)),
            _resident(gn.shape),
            _resident((vd, d)),
        ],
        out_specs=pl.BlockSpec((1, tm, d), lambda i, j: (i, j, 0)),
        out_shape=jax.ShapeDtypeStruct((b, t, d), F32),
        scratch_shapes=[
            pltpu.VMEM((tm, nproj), F32),
            pltpu.VMEM((tm, kd), F32),
            pltpu.VMEM((tm, vd), BF16),
            pltpu.VMEM((GLA_HEADS, dv, dk), F32),
        ],
        compiler_params=pltpu.CompilerParams(
            dimension_semantics=("arbitrary", "arbitrary"), vmem_limit_bytes=VMEM_LIMIT_BYTES),
        name="gla",
    )(x, g, win, wgz, wg2, bg, gn, wout)


def kernel(x, mix_norm, a_w_in, a_ln_g, a_ln_b, a_w_s, a_b_s, a_w_out, b_w_in, b_w_g2, b_b_g, b_gn_g, b_w_out, ffn_norm, ffn_w_gate, ffn_w_up, ffn_w_down, final_norm):
    b, t, d = x.shape
    depth = mix_norm.shape[0]
    assert depth == 2 and a_w_in.shape[0] == 1 and b_w_in.shape[0] == 1
    m = b * t
    bf = lambda w: w.astype(BF16)
    row = lambda v: v.reshape(1, -1)
    nqkvr = b_w_in.shape[2] - GLA_RANK

    x2 = x.reshape(m, d)
    x2 = _sgu(x2, row(mix_norm[0]), bf(a_w_in[0]), row(a_ln_g[0]), row(a_ln_b[0]),
              a_w_s[0], a_b_s[0].T, bf(a_w_out[0]), tm=512)
    x2 = _ffn(x2, row(ffn_norm[0]), bf(ffn_w_gate[0]), bf(ffn_w_up[0]), bf(ffn_w_down[0]),
              row(final_norm), final_norm=False, tm=512)
    x3 = _gla(x2.reshape(b, t, d), row(mix_norm[1]), bf(b_w_in[0, :, :nqkvr]),
              bf(b_w_in[0, :, nqkvr:]), bf(b_w_g2[0]), row(b_b_g[0]), b_gn_g[0],
              bf(b_w_out[0]), tm=512)
    x2 = _ffn(x3.reshape(m, d), row(ffn_norm[1]), bf(ffn_w_gate[1]), bf(ffn_w_up[1]),
              bf(ffn_w_down[1]), row(final_norm), final_norm=True, tm=512)
    return x2.reshape(b, t, d)
```

```python
import functools

import jax
import jax.numpy as jnp
from jax import lax
from jax.experimental import pallas as pl
from jax.experimental.pallas import tpu as pltpu

F32 = jnp.float32
BF16 = jnp.bfloat16

EPS = 1e-6
CHUNK = 64
SG_BLOCK = 128
SG_GROUPS = 8
SG_GROUPS_PER_CHUNK = 2
GLA_HEADS = 4
GLA_TAU = 16.0
GLA_PROJ_PIECE = (256, 256)

BF16_SUBLANES = 16

VMEM_LIMIT_BYTES = 56 * 1024 * 1024


def _rms(x, g):
    return x * lax.rsqrt(jnp.mean(x * x, axis=-1, keepdims=True) + EPS) * g


def _gelu(x):
    return 0.5 * x * (1.0 + lax.erf(x * (2.0 ** -0.5)))


def _dot(a, b):
    return jnp.dot(a, b, preferred_element_type=F32)


def _dot_nt(a, b):
    return lax.dot_general(a, b, (((1,), (1,)), ((), ())), preferred_element_type=F32)


def _dot_tn(a, b):
    return lax.dot_general(a, b, (((0,), (0,)), ((), ())), preferred_element_type=F32)


def _resident(shape):
    return pl.BlockSpec(shape, lambda *_: (0,) * len(shape))


def _cast_through(weights, nsteps):
    in_specs, out_specs, out_shapes = [], [], []
    for w, layer in weights:
        _, r, c = w.shape
        per = 1 if r % (nsteps * BF16_SUBLANES) == 0 else 2
        rb = r * per // nsteps
        assert rb % BF16_SUBLANES == 0 and rb * nsteps == r * per, (r, c, nsteps)
        in_specs.append(pl.BlockSpec((None, rb, c), lambda i, layer=layer, per=per: (layer, i // per, 0)))
        out_specs.append(pl.BlockSpec((rb, c), lambda i, per=per: (i // per, 0)))
        out_shapes.append(jax.ShapeDtypeStruct((r, c), BF16))
    return in_specs, out_specs, out_shapes


def _run_casts(src_refs, dst_refs):
    for src, dst in zip(src_refs, dst_refs):
        dst[...] = src[...].astype(BF16)


def _ffn_kernel(x_ref, g_ref, wg_ref, wu_ref, wd_ref, fin_ref, *rest, final_norm, ncast):
    o_ref = rest[ncast]
    _run_casts(rest[:ncast], rest[ncast + 1:])
    x = x_ref[...]
    h = _rms(x, g_ref[...]).astype(BF16)
    gate = _dot(h, wg_ref[...])
    up = _dot(h, wu_ref[...])
    a = (jax.nn.silu(gate) * up).astype(BF16)
    y = x + _dot(a, wd_ref[...])
    if final_norm:
        y = _rms(y, fin_ref[...])
    o_ref[...] = y


def _ffn(x2, g, wg, wu, wd, fin, *, final_norm, tm, cast=()):
    m, d = x2.shape
    hid = wg.shape[1]
    c_in, c_out, c_shapes = _cast_through(cast, m // tm)
    return pl.pallas_call(
        functools.partial(_ffn_kernel, final_norm=final_norm, ncast=len(cast)),
        grid=(m // tm,),
        in_specs=[
            pl.BlockSpec((tm, d), lambda i: (i, 0)),
            _resident((1, d)),
            _resident((d, hid)),
            _resident((d, hid)),
            _resident((hid, d)),
            _resident((1, d)),
        ] + c_in,
        out_specs=[pl.BlockSpec((tm, d), lambda i: (i, 0))] + c_out,
        out_shape=[jax.ShapeDtypeStruct((m, d), F32)] + c_shapes,
        compiler_params=pltpu.CompilerParams(
            dimension_semantics=("arbitrary",), vmem_limit_bytes=VMEM_LIMIT_BYTES),
        name="swiglu",
    )(x2, g, wg, wu, wd, fin, *[w for w, _ in cast])


def _sgu_kernel(x_ref, g_ref, win_ref, lng_ref, lnb_ref, ws_ref, bst_ref, wout_ref,
                *rest, tm, ncast):
    o_ref = rest[ncast]
    v_s, y_s = rest[2 * ncast + 1:]
    _run_casts(rest[:ncast], rest[ncast + 1:2 * ncast + 1])
    width = v_s.shape[1]
    gc = width // SG_GROUPS
    lanes = 128
    x = x_ref[...]
    h = _rms(x, g_ref[...]).astype(BF16)

    gpc = SG_GROUPS_PER_CHUNK
    cw = gpc * gc
    p1 = jnp.zeros((tm, lanes), F32)
    p2 = jnp.zeros((tm, lanes), F32)
    for c in range(SG_GROUPS // gpc):
        v = _gelu(_dot(h, win_ref[:, width + c * cw:width + (c + 1) * cw]))
        v_s[:, c * cw:(c + 1) * cw] = v
        for l in range(cw // lanes):
            vl = v[:, l * lanes:(l + 1) * lanes]
            p1 = p1 + vl
            p2 = p2 + vl * vl
    mu = jnp.sum(p1, axis=-1, keepdims=True) * (1.0 / width)
    var = jnp.sum(p2, axis=-1, keepdims=True) * (1.0 / width) - mu * mu
    rstd = lax.rsqrt(var + EPS)
    shift = -mu * rstd

    ti = lax.broadcasted_iota(jnp.int32, (SG_BLOCK, SG_BLOCK), 0) // CHUNK
    si = lax.broadcasted_iota(jnp.int32, (SG_BLOCK, SG_BLOCK), 1) // CHUNK
    causal = ti >= si
    for c in range(SG_GROUPS // gpc):
        ccols = slice(c * cw, (c + 1) * cw)
        u = _gelu(_dot(h, win_ref[:, ccols]))
        vn = ((v_s[:, ccols] * rstd + shift) * lng_ref[:, ccols] + lnb_ref[:, ccols]).astype(BF16)
        for gg in range(gpc):
            g = c * gpc + gg
            gcols = slice(gg * gc, (gg + 1) * gc)
            ws = jnp.where(causal, ws_ref[g], 0.0).astype(BF16)
            bias = bst_ref[:, g:g + 1]
            for w in range(tm // SG_BLOCK):
                rows = slice(w * SG_BLOCK, (w + 1) * SG_BLOCK)
                s = _dot(ws, vn[rows, gcols]) + bias
                y_s[rows, g * gc:(g + 1) * gc] = (u[rows, gcols] * s).astype(BF16)
    o_ref[...] = x + _dot(y_s[...], wout_ref[...])


def _sgu(x2, g, win, lng, lnb, ws, bst, wout, *, tm, cast=()):
    m, d = x2.shape
    width = wout.shape[0]
    c_in, c_out, c_shapes = _cast_through(cast, m // tm)
    return pl.pallas_call(
        functools.partial(_sgu_kernel, tm=tm, ncast=len(cast)),
        grid=(m // tm,),
        in_specs=[
            pl.BlockSpec((tm, d), lambda i: (i, 0)),
            _resident((1, d)),
            _resident((d, 2 * width)),
            _resident((1, width)),
            _resident((1, width)),
            _resident(ws.shape),
            _resident(bst.shape),
            _resident((width, d)),
        ] + c_in,
        out_specs=[pl.BlockSpec((tm, d), lambda i: (i, 0))] + c_out,
        out_shape=[jax.ShapeDtypeStruct((m, d), F32)] + c_shapes,
        scratch_shapes=[
            pltpu.VMEM((tm, width), F32),
            pltpu.VMEM((tm, width), BF16),
        ],
        compiler_params=pltpu.CompilerParams(
            dimension_semantics=("arbitrary",), vmem_limit_bytes=VMEM_LIMIT_BYTES),
        name="sgu",
    )(x2, g, win, lng, lnb, ws, bst, wout, *[w for w, _ in cast])


def _ref_rows(G, half):
    n, w = G.shape
    blk = 2 * half
    if blk >= 8:
        return jnp.concatenate(
            [jnp.broadcast_to(G[b * blk + half - 1:b * blk + half, :], (blk, w))
             for b in range(n // blk)], axis=0)
    assert blk == 4
    sub = lax.broadcasted_iota(jnp.int32, (8, w), 0)
    tiles = []
    for t in range(n // 8):
        lo = jnp.broadcast_to(G[8 * t + 1:8 * t + 2, :], (8, w))
        hi = jnp.broadcast_to(G[8 * t + 5:8 * t + 6, :], (8, w))
        tiles.append(jnp.where(sub < 4, lo, hi))
    return jnp.concatenate(tiles, axis=0)


def _gla_project(x_ref, row0, tm, g_ref, win_ref, wg2_ref, bg_ref, proj_ref, la_ref):
    state = {}
    ncol = proj_ref.shape[1]
    pr, pc = GLA_PROJ_PIECE
    tasks = []

    def norm(r):
        def run():
            x = x_ref[row0 + r * pr:row0 + (r + 1) * pr, :]
            state[r] = _rms(x, g_ref[...]).astype(BF16)
        return run

    def cols(r, p):
        def run():
            proj_ref[r * pr:(r + 1) * pr, p * pc:(p + 1) * pc] = _dot(state[r], win_ref[:, p * pc:(p + 1) * pc])
        return run

    def gate(r):
        def run():
            gz = _dot(state[r], win_ref[:, ncol:])
            z = _dot(gz.astype(BF16), wg2_ref[...].astype(BF16)) + bg_ref[...]
            la_ref[r * pr:(r + 1) * pr, :] = (jnp.minimum(z, 0.0) - jnp.log1p(jnp.exp(-jnp.abs(z)))) / GLA_TAU
        return run

    for r in range(tm // pr):
        tasks += [norm(r), gate(r)] + [cols(r, p) for p in range(ncol // pc)]
    return tasks


def _gla_chunks(proj_ref, la_ref, og_s, st_s, gn_ref, consts, between, *, tm, kd, vd):
    dk = kd // GLA_HEADS
    dv = vd // GLA_HEADS
    C = CHUNK
    tri3, halves, lvl_mask, diag, upper = consts
    scale = dk ** -0.5
    between = list(between)
    nchunks = tm // C
    nslots = nchunks * GLA_HEADS

    for c in range(nchunks):
        rows = slice(c * C, (c + 1) * C)
        q = proj_ref[rows, 0:kd] * scale
        k = proj_ref[rows, kd:2 * kd]
        la = la_ref[rows, :]
        la_hi = la.astype(BF16)
        r1 = la - la_hi.astype(F32)
        la_mid = r1.astype(BF16)
        la_lo = (r1 - la_mid.astype(F32)).astype(BF16)
        G = _dot(tri3, jnp.concatenate([la_hi, la_mid, la_lo, jnp.zeros_like(la_hi)], axis=0))

        xs = []
        for lvl, hf in enumerate(halves):
            e = jnp.exp(la) if hf == 1 else jnp.exp(-jnp.abs(G - _ref_rows(G, hf)))
            xs.append((jnp.where(upper[lvl], q, k) * e).astype(BF16))
        qb = q.astype(BF16)
        kb = k.astype(BF16)
        g_last = G[C - 1:C, :]
        qd = (q * jnp.exp(G)).astype(BF16)
        kdec = (k * jnp.exp(g_last - G)).astype(BF16)
        dec = jnp.exp(g_last)

        for hd in range(GLA_HEADS):
            ks_ = slice(hd * dk, (hd + 1) * dk)
            vs_ = slice(2 * kd + hd * dv, 2 * kd + (hd + 1) * dv)
            rs_ = slice(2 * kd + vd + hd * dv, 2 * kd + vd + (hd + 1) * dv)
            ps = [_dot_nt(qb[:, ks_], kb[:, ks_])] + [_dot_nt(xl[:, ks_], xl[:, ks_]) for xl in xs]
            slot = c * GLA_HEADS + hd
            for task in between[slot * len(between) // nslots:(slot + 1) * len(between) // nslots]:
                task()
            a = jnp.where(diag, ps[0], 0.0)
            for lvl in range(len(halves)):
                a = jnp.where(lvl_mask[lvl], ps[lvl + 1], a)
            v = proj_ref[rows, vs_].astype(BF16)
            st = st_s[hd]
            o = _dot(a.astype(BF16), v) + _dot_nt(qd[:, ks_], st.astype(BF16))
            st_s[hd] = st * dec[:, ks_] + _dot_tn(v, kdec[:, ks_])
            o = o * lax.rsqrt(jnp.mean(o * o, axis=-1, keepdims=True) + EPS) * gn_ref[hd:hd + 1, :]
            r = proj_ref[rows, rs_]
            og_s[rows, hd * dv:(hd + 1) * dv] = (jax.nn.silu(r) * o).astype(BF16)


def _gla_kernel(xc_ref, xn_ref, g_ref, win_ref, wg2_ref, bg_ref, gn_ref, wout_ref,
                *rest, tm, kd, vd, tiles_per_seq, ncast):
    o_ref = rest[ncast]
    pa_s, la_a, pb_s, la_b, og_s, st_s = rest[2 * ncast + 1:]
    _run_casts(rest[:ncast], rest[ncast + 1:2 * ncast + 1])
    C = CHUNK
    step = pl.program_id(0)
    wts = (g_ref, win_ref, wg2_ref, bg_ref)

    @pl.when(step == 0)
    def _():
        for task in _gla_project(xc_ref, 0, tm, *wts, pa_s, la_a):
            task()

    @pl.when((2 * step) % tiles_per_seq == 0)
    def _():
        st_s[...] = jnp.zeros_like(st_s)

    ri = lax.broadcasted_iota(jnp.int32, (C, C), 0)
    ci = lax.broadcasted_iota(jnp.int32, (C, C), 1)
    tri = (ri >= ci).astype(BF16)
    tri3 = jnp.concatenate([tri, tri, tri, jnp.zeros_like(tri)], axis=1)
    halves = (32, 16, 8, 4, 2, 1)
    lvl_mask = [((ri // (2 * hf)) == (ci // (2 * hf))) & ((ri % (2 * hf)) >= hf) & ((ci % (2 * hf)) < hf)
                for hf in halves]
    diag = ri == ci
    rowc = lax.broadcasted_iota(jnp.int32, (C, kd), 0)
    upper = [(rowc % (2 * hf)) >= hf for hf in halves]
    consts = (tri3, halves, lvl_mask, diag, upper)

    nxt = _gla_project(xc_ref, tm, tm, *wts, pb_s, la_b)
    _gla_chunks(pa_s, la_a, og_s, st_s, gn_ref, consts, nxt, tm=tm, kd=kd, vd=vd)
    o_ref[0:tm, :] = xc_ref[0:tm, :] + _dot(og_s[...], wout_ref[...])

    nxt = _gla_project(xn_ref, 0, tm, *wts, pa_s, la_a)
    _gla_chunks(pb_s, la_b, og_s, st_s, gn_ref, consts, nxt, tm=tm, kd=kd, vd=vd)
    o_ref[tm:2 * tm, :] = xc_ref[tm:2 * tm, :] + _dot(og_s[...], wout_ref[...])


def _gla(x2, g, win, wg2, bg, gn, wout, *, tm, seq_len, cast=()):
    m, d = x2.shape
    kd = wg2.shape[1]
    vd = wout.shape[0]
    dk = kd // GLA_HEADS
    dv = vd // GLA_HEADS
    nproj = win.shape[1] - wg2.shape[0]
    ntiles = m // tm
    assert seq_len % (2 * tm) == 0 and nproj % GLA_PROJ_PIECE[1] == 0 and tm % GLA_PROJ_PIECE[0] == 0
    c_in, c_out, c_shapes = _cast_through(cast, ntiles // 2)
    return pl.pallas_call(
        functools.partial(_gla_kernel, tm=tm, kd=kd, vd=vd, tiles_per_seq=seq_len // tm, ncast=len(cast)),
        grid=(ntiles // 2,),
        in_specs=[
            pl.BlockSpec((2 * tm, d), lambda i: (i, 0)),
            pl.BlockSpec((tm, d), lambda i: (jnp.minimum(2 * i + 2, ntiles - 1), 0)),
            _resident((1, d)),
            _resident(win.shape),
            _resident(wg2.shape),
            _resident((1, kd)),
            _resident(gn.shape),
            _resident((vd, d)),
        ] + c_in,
        out_specs=[pl.BlockSpec((2 * tm, d), lambda i: (i, 0))] + c_out,
        out_shape=[jax.ShapeDtypeStruct((m, d), F32)] + c_shapes,
        scratch_shapes=[
            pltpu.VMEM((tm, nproj), F32),
            pltpu.VMEM((tm, kd), F32),
            pltpu.VMEM((tm, nproj), F32),
            pltpu.VMEM((tm, kd), F32),
            pltpu.VMEM((tm, vd), BF16),
            pltpu.VMEM((GLA_HEADS, dv, dk), F32),
        ],
        compiler_params=pltpu.CompilerParams(
            dimension_semantics=("arbitrary",), vmem_limit_bytes=VMEM_LIMIT_BYTES),
        name="gla",
    )(x2, x2, g, win, wg2, bg, gn, wout, *[w for w, _ in cast])


def kernel(x, mix_norm, a_w_in, a_ln_g, a_ln_b, a_w_s, a_b_s, a_w_out, b_w_in, b_w_g2, b_b_g, b_gn_g, b_w_out, ffn_norm, ffn_w_gate, ffn_w_up, ffn_w_down, final_norm):
    b, t, d = x.shape
    depth = mix_norm.shape[0]
    assert depth == 2 and a_w_in.shape[0] == 1 and b_w_in.shape[0] == 1
    m = b * t
    bf = lambda w: w.astype(BF16)
    row = lambda v: v.reshape(1, -1)

    x2 = x.reshape(m, d)
    x2, wg0, wu0, wd0 = _sgu(
        x2, row(mix_norm[0]), bf(a_w_in[0]), row(a_ln_g[0]), row(a_ln_b[0]),
        a_w_s[0], a_b_s[0].T, bf(a_w_out[0]), tm=512,
        cast=((ffn_w_gate, 0), (ffn_w_up, 0), (ffn_w_down, 0)))
    x2, b_win, b_wout = _ffn(
        x2, row(ffn_norm[0]), wg0, wu0, wd0, row(final_norm), final_norm=False, tm=512,
        cast=((b_w_in, 0), (b_w_out, 0)))
    x2, wg1, wu1, wd1 = _gla(
        x2, row(mix_norm[1]), b_win, b_w_g2[0], row(b_b_g[0]), b_gn_g[0], b_wout, tm=512, seq_len=t,
        cast=((ffn_w_gate, 1), (ffn_w_up, 1), (ffn_w_down, 1)))
    (x2,) = _ffn(x2, row(ffn_norm[1]), wg1, wu1, wd1, row(final_norm), final_norm=True, tm=512)
    return x2.reshape(b, t, d)
```

```python
import functools

import jax
import jax.numpy as jnp
from jax import lax
from jax.experimental import pallas as pl
from jax.experimental.pallas import tpu as pltpu

F32 = jnp.float32
BF16 = jnp.bfloat16

EPS = 1e-6
CHUNK = 64
SG_BLOCK = 128
SG_GROUPS = 8
SG_GROUPS_PER_CHUNK = 2
GLA_HEADS = 4
GLA_TAU = 16.0
GLA_PROJ_PIECE = (256, 256)

BF16_SUBLANES = 16

VMEM_LIMIT_BYTES = 56 * 1024 * 1024


def _rms(x, g):
    return x * lax.rsqrt(jnp.mean(x * x, axis=-1, keepdims=True) + EPS) * g


def _gelu(x):
    return 0.5 * x * (1.0 + lax.erf(x * (2.0 ** -0.5)))


def _dot(a, b):
    return jnp.dot(a, b, preferred_element_type=F32)


def _dot_nt(a, b):
    return lax.dot_general(a, b, (((1,), (1,)), ((), ())), preferred_element_type=F32)


def _dot_tn(a, b):
    return lax.dot_general(a, b, (((0,), (0,)), ((), ())), preferred_element_type=F32)


def _resident(shape):
    return pl.BlockSpec(shape, lambda *_: (0,) * len(shape))


def _cast_through(weights, nsteps):
    in_specs, out_specs, out_shapes = [], [], []
    for w, layer in weights:
        _, r, c = w.shape
        per = 1 if r % (nsteps * BF16_SUBLANES) == 0 else 2
        rb = r * per // nsteps
        assert rb % BF16_SUBLANES == 0 and rb * nsteps == r * per, (r, c, nsteps)
        in_specs.append(pl.BlockSpec((None, rb, c), lambda i, layer=layer, per=per: (layer, i // per, 0)))
        out_specs.append(pl.BlockSpec((rb, c), lambda i, per=per: (i // per, 0)))
        out_shapes.append(jax.ShapeDtypeStruct((r, c), BF16))
    return in_specs, out_specs, out_shapes


def _run_casts(src_refs, dst_refs):
    for src, dst in zip(src_refs, dst_refs):
        dst[...] = src[...].astype(BF16)


def _ffn_kernel(x_ref, g_ref, wg_ref, wu_ref, wd_ref, fin_ref, *rest, final_norm, ncast):
    o_ref = rest[ncast]
    _run_casts(rest[:ncast], rest[ncast + 1:])
    x = x_ref[...]
    h = _rms(x, g_ref[...]).astype(BF16)
    gate = _dot(h, wg_ref[...])
    up = _dot(h, wu_ref[...])
    a = (jax.nn.silu(gate) * up).astype(BF16)
    y = x + _dot(a, wd_ref[...])
    if final_norm:
        y = _rms(y, fin_ref[...])
    o_ref[...] = y


def _ffn(x2, g, wg, wu, wd, fin, *, final_norm, tm, cast=()):
    m, d = x2.shape
    hid = wg.shape[1]
    c_in, c_out, c_shapes = _cast_through(cast, m // tm)
    return pl.pallas_call(
        functools.partial(_ffn_kernel, final_norm=final_norm, ncast=len(cast)),
        grid=(m // tm,),
        in_specs=[
            pl.BlockSpec((tm, d), lambda i: (i, 0)),
            _resident((1, d)),
            _resident((d, hid)),
            _resident((d, hid)),
            _resident((hid, d)),
            _resident((1, d)),
        ] + c_in,
        out_specs=[pl.BlockSpec((tm, d), lambda i: (i, 0))] + c_out,
        out_shape=[jax.ShapeDtypeStruct((m, d), F32)] + c_shapes,
        compiler_params=pltpu.CompilerParams(
            dimension_semantics=("arbitrary",), vmem_limit_bytes=VMEM_LIMIT_BYTES),
        name="swiglu",
    )(x2, g, wg, wu, wd, fin, *[w for w, _ in cast])


def _sgu_kernel(x_ref, g_ref, win_ref, lng_ref, lnb_ref, ws_ref, bst_ref, wout_ref,
                *rest, tm, ncast):
    o_ref = rest[ncast]
    v_s, y_s = rest[2 * ncast + 1:]
    _run_casts(rest[:ncast], rest[ncast + 1:2 * ncast + 1])
    width = v_s.shape[1]
    gc = width // SG_GROUPS
    lanes = 128
    x = x_ref[...]
    h = _rms(x, g_ref[...]).astype(BF16)

    gpc = SG_GROUPS_PER_CHUNK
    cw = gpc * gc
    p1 = jnp.zeros((tm, lanes), F32)
    p2 = jnp.zeros((tm, lanes), F32)
    for c in range(SG_GROUPS // gpc):
        v = _gelu(_dot(h, win_ref[:, width + c * cw:width + (c + 1) * cw]))
        v_s[:, c * cw:(c + 1) * cw] = v
        for l in range(cw // lanes):
            vl = v[:, l * lanes:(l + 1) * lanes]
            p1 = p1 + vl
            p2 = p2 + vl * vl
    mu = jnp.sum(p1, axis=-1, keepdims=True) * (1.0 / width)
    var = jnp.sum(p2, axis=-1, keepdims=True) * (1.0 / width) - mu * mu
    rstd = lax.rsqrt(var + EPS)
    shift = -mu * rstd

    ti = lax.broadcasted_iota(jnp.int32, (SG_BLOCK, SG_BLOCK), 0) // CHUNK
    si = lax.broadcasted_iota(jnp.int32, (SG_BLOCK, SG_BLOCK), 1) // CHUNK
    causal = ti >= si
    for c in range(SG_GROUPS // gpc):
        ccols = slice(c * cw, (c + 1) * cw)
        u = _gelu(_dot(h, win_ref[:, ccols]))
        vn = ((v_s[:, ccols] * rstd + shift) * lng_ref[:, ccols] + lnb_ref[:, ccols]).astype(BF16)
        for gg in range(gpc):
            g = c * gpc + gg
            gcols = slice(gg * gc, (gg + 1) * gc)
            ws = jnp.where(causal, ws_ref[g], 0.0).astype(BF16)
            bias = bst_ref[:, g:g + 1]
            for w in range(tm // SG_BLOCK):
                rows = slice(w * SG_BLOCK, (w + 1) * SG_BLOCK)
                s = _dot(ws, vn[rows, gcols]) + bias
                y_s[rows, g * gc:(g + 1) * gc] = (u[rows, gcols] * s).astype(BF16)
    o_ref[...] = x + _dot(y_s[...], wout_ref[...])


def _sgu(x2, g, win, lng, lnb, ws, bst, wout, *, tm, cast=()):
    m, d = x2.shape
    width = wout.shape[0]
    c_in, c_out, c_shapes = _cast_through(cast, m // tm)
    return pl.pallas_call(
        functools.partial(_sgu_kernel, tm=tm, ncast=len(cast)),
        grid=(m // tm,),
        in_specs=[
            pl.BlockSpec((tm, d), lambda i: (i, 0)),
            _resident((1, d)),
            _resident((d, 2 * width)),
            _resident((1, width)),
            _resident((1, width)),
            _resident(ws.shape),
            _resident(bst.shape),
            _resident((width, d)),
        ] + c_in,
        out_specs=[pl.BlockSpec((tm, d), lambda i: (i, 0))] + c_out,
        out_shape=[jax.ShapeDtypeStruct((m, d), F32)] + c_shapes,
        scratch_shapes=[
            pltpu.VMEM((tm, width), F32),
            pltpu.VMEM((tm, width), BF16),
        ],
        compiler_params=pltpu.CompilerParams(
            dimension_semantics=("arbitrary",), vmem_limit_bytes=VMEM_LIMIT_BYTES),
        name="sgu",
    )(x2, g, win, lng, lnb, ws, bst, wout, *[w for w, _ in cast])


def _ref_rows(G, half):
    n, w = G.shape
    blk = 2 * half
    if blk >= 8:
        return jnp.concatenate(
            [jnp.broadcast_to(G[b * blk + half - 1:b * blk + half, :], (blk, w))
             for b in range(n // blk)], axis=0)
    assert blk == 4
    sub = lax.broadcasted_iota(jnp.int32, (8, w), 0)
    tiles = []
    for t in range(n // 8):
        lo = jnp.broadcast_to(G[8 * t + 1:8 * t + 2, :], (8, w))
        hi = jnp.broadcast_to(G[8 * t + 5:8 * t + 6, :], (8, w))
        tiles.append(jnp.where(sub < 4, lo, hi))
    return jnp.concatenate(tiles, axis=0)


def _gla_project(x_ref, row0, tm, g_ref, win_ref, wg2_ref, bg_ref, proj_ref, la_ref):
    state = {}
    ncol = proj_ref.shape[1]
    pr, pc = GLA_PROJ_PIECE
    tasks = []

    def norm(r):
        def run():
            x = x_ref[row0 + r * pr:row0 + (r + 1) * pr, :]
            state[r] = _rms(x, g_ref[...]).astype(BF16)
        return run

    def cols(r, p):
        def run():
            proj_ref[r * pr:(r + 1) * pr, p * pc:(p + 1) * pc] = _dot(state[r], win_ref[:, p * pc:(p + 1) * pc])
        return run

    def gate(r):
        def run():
            gz = _dot(state[r], win_ref[:, ncol:])
            z = _dot(gz.astype(BF16), wg2_ref[...].astype(BF16)) + bg_ref[...]
            la_ref[r * pr:(r + 1) * pr, :] = (jnp.minimum(z, 0.0) - jnp.log1p(jnp.exp(-jnp.abs(z)))) / GLA_TAU
        return run

    for r in range(tm // pr):
        tasks += [norm(r), gate(r)] + [cols(r, p) for p in range(ncol // pc)]
    return tasks


def _gla_chunks(proj_ref, la_ref, og_s, st_s, gn_ref, consts, between, *, tm, kd, vd):
    dk = kd // GLA_HEADS
    dv = vd // GLA_HEADS
    C = CHUNK
    tri3, halves, lvl_mask, diag, upper = consts
    scale = dk ** -0.5
    between = list(between)
    nchunks = tm // C
    nslots = nchunks * GLA_HEADS

    for c in range(nchunks):
        rows = slice(c * C, (c + 1) * C)
        la_all = la_ref[rows, :]
        la_hi = la_all.astype(BF16)
        r1 = la_all - la_hi.astype(F32)
        la_mid = r1.astype(BF16)
        la_lo = (r1 - la_mid.astype(F32)).astype(BF16)
        G_all = _dot(tri3, jnp.concatenate([la_hi, la_mid, la_lo, jnp.zeros_like(la_hi)], axis=0))

        for hd in range(GLA_HEADS):
            ks_ = slice(hd * dk, (hd + 1) * dk)
            vs_ = slice(2 * kd + hd * dv, 2 * kd + (hd + 1) * dv)
            rs_ = slice(2 * kd + vd + hd * dv, 2 * kd + vd + (hd + 1) * dv)
            q = proj_ref[rows, hd * dk:(hd + 1) * dk] * scale
            k = proj_ref[rows, kd + hd * dk:kd + (hd + 1) * dk]
            la = la_all[:, ks_]
            G = G_all[:, ks_]
            ps = [_dot_nt(q.astype(BF16), k.astype(BF16))]
            for lvl, hf in enumerate(halves):
                if hf == 1:
                    e = jnp.exp(jnp.where(upper[lvl], la, 0.0))
                else:
                    e = jnp.exp(-jnp.abs(G - _ref_rows(G, hf)))
                xl = (jnp.where(upper[lvl], q, k) * e).astype(BF16)
                ps.append(_dot_nt(xl, xl))
            g_last = G[C - 1:C, :]
            qd = (q * jnp.exp(G)).astype(BF16)
            kdec = (k * jnp.exp(g_last - G)).astype(BF16)
            dec = jnp.exp(g_last)
            slot = c * GLA_HEADS + hd
            for task in between[slot * len(between) // nslots:(slot + 1) * len(between) // nslots]:
                task()
            a = jnp.where(diag, ps[0], 0.0)
            for lvl in range(len(halves)):
                a = jnp.where(lvl_mask[lvl], ps[lvl + 1], a)
            v = proj_ref[rows, vs_].astype(BF16)
            st = st_s[hd]
            o = _dot(a.astype(BF16), v) + _dot(qd, st.astype(BF16))
            dcol = jnp.transpose(jnp.broadcast_to(dec, (dk, dk)))
            st_s[hd] = st * jnp.concatenate([dcol] * (dv // dk), axis=1) + _dot_tn(kdec, v)
            o = o * lax.rsqrt(jnp.mean(o * o, axis=-1, keepdims=True) + EPS) * gn_ref[hd:hd + 1, :]
            r = proj_ref[rows, rs_]
            og_s[rows, hd * dv:(hd + 1) * dv] = (jax.nn.silu(r) * o).astype(BF16)


def _gla_kernel(xc_ref, xn_ref, g_ref, win_ref, wg2_ref, bg_ref, gn_ref, wout_ref,
                *rest, tm, kd, vd, tiles_per_seq, ncast):
    o_ref = rest[ncast]
    pa_s, la_a, pb_s, la_b, og_s, st_s = rest[2 * ncast + 1:]
    _run_casts(rest[:ncast], rest[ncast + 1:2 * ncast + 1])
    C = CHUNK
    step = pl.program_id(0)
    wts = (g_ref, win_ref, wg2_ref, bg_ref)

    @pl.when(step == 0)
    def _():
        for task in _gla_project(xc_ref, 0, tm, *wts, pa_s, la_a):
            task()

    @pl.when((2 * step) % tiles_per_seq == 0)
    def _():
        st_s[...] = jnp.zeros_like(st_s)

    ri = lax.broadcasted_iota(jnp.int32, (C, C), 0)
    ci = lax.broadcasted_iota(jnp.int32, (C, C), 1)
    tri = (ri >= ci).astype(BF16)
    tri3 = jnp.concatenate([tri, tri, tri, jnp.zeros_like(tri)], axis=1)
    halves = (32, 16, 8, 4, 2, 1)
    lvl_mask = [((ri // (2 * hf)) == (ci // (2 * hf))) & ((ri % (2 * hf)) >= hf) & ((ci % (2 * hf)) < hf)
                for hf in halves]
    diag = ri == ci
    rowc = lax.broadcasted_iota(jnp.int32, (C, kd // GLA_HEADS), 0)
    upper = [(rowc % (2 * hf)) >= hf for hf in halves]
    consts = (tri3, halves, lvl_mask, diag, upper)

    nxt = _gla_project(xc_ref, tm, tm, *wts, pb_s, la_b)
    _gla_chunks(pa_s, la_a, og_s, st_s, gn_ref, consts, nxt, tm=tm, kd=kd, vd=vd)
    o_ref[0:tm, :] = xc_ref[0:tm, :] + _dot(og_s[...], wout_ref[...])

    nxt = _gla_project(xn_ref, 0, tm, *wts, pa_s, la_a)
    _gla_chunks(pb_s, la_b, og_s, st_s, gn_ref, consts, nxt, tm=tm, kd=kd, vd=vd)
    o_ref[tm:2 * tm, :] = xc_ref[tm:2 * tm, :] + _dot(og_s[...], wout_ref[...])


def _gla(x2, g, win, wg2, bg, gn, wout, *, tm, seq_len, cast=()):
    m, d = x2.shape
    kd = wg2.shape[1]
    vd = wout.shape[0]
    dk = kd // GLA_HEADS
    dv = vd // GLA_HEADS
    nproj = win.shape[1] - wg2.shape[0]
    ntiles = m // tm
    assert seq_len % (2 * tm) == 0 and nproj % GLA_PROJ_PIECE[1] == 0 and tm % GLA_PROJ_PIECE[0] == 0
    c_in, c_out, c_shapes = _cast_through(cast, ntiles // 2)
    return pl.pallas_call(
        functools.partial(_gla_kernel, tm=tm, kd=kd, vd=vd, tiles_per_seq=seq_len // tm, ncast=len(cast)),
        grid=(ntiles // 2,),
        in_specs=[
            pl.BlockSpec((2 * tm, d), lambda i: (i, 0)),
            pl.BlockSpec((tm, d), lambda i: (jnp.minimum(2 * i + 2, ntiles - 1), 0)),
            _resident((1, d)),
            _resident(win.shape),
            _resident(wg2.shape),
            _resident((1, kd)),
            _resident(gn.shape),
            _resident((vd, d)),
        ] + c_in,
        out_specs=[pl.BlockSpec((2 * tm, d), lambda i: (i, 0))] + c_out,
        out_shape=[jax.ShapeDtypeStruct((m, d), F32)] + c_shapes,
        scratch_shapes=[
            pltpu.VMEM((tm, nproj), F32),
            pltpu.VMEM((tm, kd), F32),
            pltpu.VMEM((tm, nproj), F32),
            pltpu.VMEM((tm, kd), F32),
            pltpu.VMEM((tm, vd), BF16),
            pltpu.VMEM((GLA_HEADS, dk, dv), F32),
        ],
        compiler_params=pltpu.CompilerParams(
            dimension_semantics=("arbitrary",), vmem_limit_bytes=VMEM_LIMIT_BYTES),
        name="gla",
    )(x2, x2, g, win, wg2, bg, gn, wout, *[w for w, _ in cast])


def kernel(x, mix_norm, a_w_in, a_ln_g, a_ln_b, a_w_s, a_b_s, a_w_out, b_w_in, b_w_g2, b_b_g, b_gn_g, b_w_out, ffn_norm, ffn_w_gate, ffn_w_up, ffn_w_down, final_norm):
    b, t, d = x.shape
    depth = mix_norm.shape[0]
    assert depth == 2 and a_w_in.shape[0] == 1 and b_w_in.shape[0] == 1
    m = b * t
    bf = lambda w: w.astype(BF16)
    row = lambda v: v.reshape(1, -1)

    x2 = x.reshape(m, d)
    x2, wg0, wu0, wd0 = _sgu(
        x2, row(mix_norm[0]), bf(a_w_in[0]), row(a_ln_g[0]), row(a_ln_b[0]),
        a_w_s[0], a_b_s[0].T, bf(a_w_out[0]), tm=512,
        cast=((ffn_w_gate, 0), (ffn_w_up, 0), (ffn_w_down, 0)))
    x2, b_win, b_wout = _ffn(
        x2, row(ffn_norm[0]), wg0, wu0, wd0, row(final_norm), final_norm=False, tm=512,
        cast=((b_w_in, 0), (b_w_out, 0)))
    x2, wg1, wu1, wd1 = _gla(
        x2, row(mix_norm[1]), b_win, b_w_g2[0], row(b_b_g[0]), b_gn_g[0], b_wout, tm=256, seq_len=t,
        cast=((ffn_w_gate, 1), (ffn_w_up, 1), (ffn_w_down, 1)))
    (x2,) = _ffn(x2, row(ffn_norm[1]), wg1, wu1, wd1, row(final_norm), final_norm=True, tm=512)
    return x2.reshape(b, t, d)
```

```python
import functools

import jax
import jax.numpy as jnp
from jax import lax
from jax.experimental import pallas as pl
from jax.experimental.pallas import tpu as pltpu

F32 = jnp.float32
BF16 = jnp.bfloat16

EPS = 1e-6
CHUNK = 64
SG_BLOCK = 128
SG_GROUPS = 8
SG_GROUPS_PER_CHUNK = 2
GLA_HEADS = 4
GLA_TAU = 16.0
GLA_PROJ_PIECE = (256, 256)

BF16_SUBLANES = 16
MXU_N = 256
FFN_HIDDEN_CHUNKS = 4

VMEM_LIMIT_BYTES = 56 * 1024 * 1024


def _rms(x, g):
    return x * lax.rsqrt(jnp.mean(x * x, axis=-1, keepdims=True) + EPS) * g


def _gelu(x):
    return 0.5 * x * (1.0 + lax.erf(x * (2.0 ** -0.5)))


def _dot(a, b):
    return jnp.dot(a, b, preferred_element_type=F32)


def _dot_nt(a, b):
    return lax.dot_general(a, b, (((1,), (1,)), ((), ())), preferred_element_type=F32)


def _dot_tn(a, b):
    return lax.dot_general(a, b, (((0,), (0,)), ((), ())), preferred_element_type=F32)


def _resident(shape):
    return pl.BlockSpec(shape, lambda *_: (0,) * len(shape))


def _cast_through(weights, nsteps):
    in_specs, out_specs, out_shapes = [], [], []
    for w, layer in weights:
        _, r, c = w.shape
        per = 1 if r % (nsteps * BF16_SUBLANES) == 0 else 2
        rb = r * per // nsteps
        assert rb % BF16_SUBLANES == 0 and rb * nsteps == r * per, (r, c, nsteps)
        in_specs.append(pl.BlockSpec((None, rb, c), lambda i, layer=layer, per=per: (layer, i // per, 0)))
        out_specs.append(pl.BlockSpec((rb, c), lambda i, per=per: (i // per, 0)))
        out_shapes.append(jax.ShapeDtypeStruct((r, c), BF16))
    return in_specs, out_specs, out_shapes


def _run_casts(src_refs, dst_refs):
    for src, dst in zip(src_refs, dst_refs):
        dst[...] = src[...].astype(BF16)


def _ffn_kernel(x_ref, g_ref, wg_ref, wu_ref, wd_ref, fin_ref, *rest, final_norm, ncast):
    o_ref = rest[ncast]
    _run_casts(rest[:ncast], rest[ncast + 1:])
    x = x_ref[...]
    h = _rms(x, g_ref[...]).astype(BF16)
    hid = wg_ref.shape[1]
    ntiles = hid // MXU_N
    bounds = [MXU_N * (ntiles * j // FFN_HIDDEN_CHUNKS) for j in range(FFN_HIDDEN_CHUNKS + 1)]
    y = x
    for c0, c1 in zip(bounds[:-1], bounds[1:]):
        gate = _dot(h, wg_ref[:, c0:c1])
        up = _dot(h, wu_ref[:, c0:c1])
        a = (jax.nn.silu(gate) * up).astype(BF16)
        y = y + _dot(a, wd_ref[c0:c1, :])
    if final_norm:
        y = _rms(y, fin_ref[...])
    o_ref[...] = y


def _ffn(x2, g, wg, wu, wd, fin, *, final_norm, tm, cast=()):
    m, d = x2.shape
    hid = wg.shape[1]
    c_in, c_out, c_shapes = _cast_through(cast, m // tm)
    return pl.pallas_call(
        functools.partial(_ffn_kernel, final_norm=final_norm, ncast=len(cast)),
        grid=(m // tm,),
        in_specs=[
            pl.BlockSpec((tm, d), lambda i: (i, 0)),
            _resident((1, d)),
            _resident((d, hid)),
            _resident((d, hid)),
            _resident((hid, d)),
            _resident((1, d)),
        ] + c_in,
        out_specs=[pl.BlockSpec((tm, d), lambda i: (i, 0))] + c_out,
        out_shape=[jax.ShapeDtypeStruct((m, d), F32)] + c_shapes,
        compiler_params=pltpu.CompilerParams(
            dimension_semantics=("arbitrary",), vmem_limit_bytes=VMEM_LIMIT_BYTES),
        name="swiglu",
    )(x2, g, wg, wu, wd, fin, *[w for w, _ in cast])


def _sgu_kernel(x_ref, g_ref, win_ref, lng_ref, lnb_ref, ws_ref, bst_ref, wout_ref,
                *rest, tm, ncast):
    o_ref = rest[ncast]
    v_s, y_s = rest[2 * ncast + 1:]
    _run_casts(rest[:ncast], rest[ncast + 1:2 * ncast + 1])
    width = v_s.shape[1]
    gc = width // SG_GROUPS
    lanes = 128
    x = x_ref[...]
    h = _rms(x, g_ref[...]).astype(BF16)

    gpc = SG_GROUPS_PER_CHUNK
    cw = gpc * gc
    p1 = jnp.zeros((tm, lanes), F32)
    p2 = jnp.zeros((tm, lanes), F32)
    for c in range(SG_GROUPS // gpc):
        v = _gelu(_dot(h, win_ref[:, width + c * cw:width + (c + 1) * cw]))
        v_s[:, c * cw:(c + 1) * cw] = v
        for l in range(cw // lanes):
            vl = v[:, l * lanes:(l + 1) * lanes]
            p1 = p1 + vl
            p2 = p2 + vl * vl
    mu = jnp.sum(p1, axis=-1, keepdims=True) * (1.0 / width)
    var = jnp.sum(p2, axis=-1, keepdims=True) * (1.0 / width) - mu * mu
    rstd = lax.rsqrt(var + EPS)
    shift = -mu * rstd

    ti = lax.broadcasted_iota(jnp.int32, (SG_BLOCK, SG_BLOCK), 0) // CHUNK
    si = lax.broadcasted_iota(jnp.int32, (SG_BLOCK, SG_BLOCK), 1) // CHUNK
    causal = ti >= si
    for c in range(SG_GROUPS // gpc):
        ccols = slice(c * cw, (c + 1) * cw)
        u = _gelu(_dot(h, win_ref[:, ccols]))
        vn = ((v_s[:, ccols] * rstd + shift) * lng_ref[:, ccols] + lnb_ref[:, ccols]).astype(BF16)
        for gg in range(gpc):
            g = c * gpc + gg
            gcols = slice(gg * gc, (gg + 1) * gc)
            ws = jnp.where(causal, ws_ref[g], 0.0).astype(BF16)
            bias = bst_ref[:, g:g + 1]
            for w in range(tm // SG_BLOCK):
                rows = slice(w * SG_BLOCK, (w + 1) * SG_BLOCK)
                s = _dot(ws, vn[rows, gcols]) + bias
                y_s[rows, g * gc:(g + 1) * gc] = (u[rows, gcols] * s).astype(BF16)
    o_ref[...] = x + _dot(y_s[...], wout_ref[...])


def _sgu(x2, g, win, lng, lnb, ws, bst, wout, *, tm, cast=()):
    m, d = x2.shape
    width = wout.shape[0]
    c_in, c_out, c_shapes = _cast_through(cast, m // tm)
    return pl.pallas_call(
        functools.partial(_sgu_kernel, tm=tm, ncast=len(cast)),
        grid=(m // tm,),
        in_specs=[
            pl.BlockSpec((tm, d), lambda i: (i, 0)),
            _resident((1, d)),
            _resident((d, 2 * width)),
            _resident((1, width)),
            _resident((1, width)),
            _resident(ws.shape),
            _resident(bst.shape),
            _resident((width, d)),
        ] + c_in,
        out_specs=[pl.BlockSpec((tm, d), lambda i: (i, 0))] + c_out,
        out_shape=[jax.ShapeDtypeStruct((m, d), F32)] + c_shapes,
        scratch_shapes=[
            pltpu.VMEM((tm, width), F32),
            pltpu.VMEM((tm, width), BF16),
        ],
        compiler_params=pltpu.CompilerParams(
            dimension_semantics=("arbitrary",), vmem_limit_bytes=VMEM_LIMIT_BYTES),
        name="sgu",
    )(x2, g, win, lng, lnb, ws, bst, wout, *[w for w, _ in cast])


def _ref_rows(G, half):
    n, w = G.shape
    blk = 2 * half
    if blk >= 8:
        return jnp.concatenate(
            [jnp.broadcast_to(G[b * blk + half - 1:b * blk + half, :], (blk, w))
             for b in range(n // blk)], axis=0)
    assert blk == 4
    sub = lax.broadcasted_iota(jnp.int32, (8, w), 0)
    tiles = []
    for t in range(n // 8):
        lo = jnp.broadcast_to(G[8 * t + 1:8 * t + 2, :], (8, w))
        hi = jnp.broadcast_to(G[8 * t + 5:8 * t + 6, :], (8, w))
        tiles.append(jnp.where(sub < 4, lo, hi))
    return jnp.concatenate(tiles, axis=0)


def _gla_project(x_ref, row0, tm, g_ref, win_ref, wg2_ref, bg_ref, proj_ref, la_ref):
    state = {}
    ncol = proj_ref.shape[1]
    pr, pc = GLA_PROJ_PIECE
    tasks = []

    def norm(r):
        def run():
            x = x_ref[row0 + r * pr:row0 + (r + 1) * pr, :]
            state[r] = _rms(x, g_ref[...]).astype(BF16)
        return run

    def cols(r, p):
        def run():
            proj_ref[r * pr:(r + 1) * pr, p * pc:(p + 1) * pc] = _dot(state[r], win_ref[:, p * pc:(p + 1) * pc])
        return run

    def gate(r):
        def run():
            gz = _dot(state[r], win_ref[:, ncol:])
            z = _dot(gz.astype(BF16), wg2_ref[...].astype(BF16)) + bg_ref[...]
            la_ref[r * pr:(r + 1) * pr, :] = (jnp.minimum(z, 0.0) - jnp.log1p(jnp.exp(-jnp.abs(z)))) / GLA_TAU
        return run

    for r in range(tm // pr):
        tasks += [norm(r), gate(r)] + [cols(r, p) for p in range(ncol // pc)]
    return tasks


def _gla_chunks(proj_ref, la_ref, og_s, st_s, gn_ref, consts, between, *, tm, kd, vd):
    dk = kd // GLA_HEADS
    dv = vd // GLA_HEADS
    C = CHUNK
    tri3, halves, lvl_mask, diag, upper = consts
    scale = dk ** -0.5
    between = list(between)
    nchunks = tm // C
    nslots = nchunks * GLA_HEADS

    for c in range(nchunks):
        rows = slice(c * C, (c + 1) * C)
        la_all = la_ref[rows, :]
        la_hi = la_all.astype(BF16)
        r1 = la_all - la_hi.astype(F32)
        la_mid = r1.astype(BF16)
        la_lo = (r1 - la_mid.astype(F32)).astype(BF16)
        G_all = _dot(tri3, jnp.concatenate([la_hi, la_mid, la_lo, jnp.zeros_like(la_hi)], axis=0))

        for hd in range(GLA_HEADS):
            ks_ = slice(hd * dk, (hd + 1) * dk)
            vs_ = slice(2 * kd + hd * dv, 2 * kd + (hd + 1) * dv)
            rs_ = slice(2 * kd + vd + hd * dv, 2 * kd + vd + (hd + 1) * dv)
            q = proj_ref[rows, hd * dk:(hd + 1) * dk] * scale
            k = proj_ref[rows, kd + hd * dk:kd + (hd + 1) * dk]
            la = la_all[:, ks_]
            G = G_all[:, ks_]
            ps = [_dot_nt(q.astype(BF16), k.astype(BF16))]
            for lvl, hf in enumerate(halves):
                if hf == 1:
                    e = jnp.exp(jnp.where(upper[lvl], la, 0.0))
                else:
                    e = jnp.exp(-jnp.abs(G - _ref_rows(G, hf)))
                xl = (jnp.where(upper[lvl], q, k) * e).astype(BF16)
                ps.append(_dot_nt(xl, xl))
            g_last = G[C - 1:C, :]
            qd = (q * jnp.exp(G)).astype(BF16)
            kdec = (k * jnp.exp(g_last - G)).astype(BF16)
            dec = jnp.exp(g_last)
            slot = c * GLA_HEADS + hd
            for task in between[slot * len(between) // nslots:(slot + 1) * len(between) // nslots]:
                task()
            a = jnp.where(diag, ps[0], 0.0)
            for lvl in range(len(halves)):
                a = jnp.where(lvl_mask[lvl], ps[lvl + 1], a)
            v = proj_ref[rows, vs_].astype(BF16)
            st = st_s[hd]
            o = _dot(a.astype(BF16), v) + _dot(qd, st.astype(BF16))
            dcol = jnp.transpose(jnp.broadcast_to(dec, (dk, dk)))
            st_s[hd] = st * jnp.concatenate([dcol] * (dv // dk), axis=1) + _dot_tn(kdec, v)
            o = o * lax.rsqrt(jnp.mean(o * o, axis=-1, keepdims=True) + EPS) * gn_ref[hd:hd + 1, :]
            r = proj_ref[rows, rs_]
            og_s[rows, hd * dv:(hd + 1) * dv] = (jax.nn.silu(r) * o).astype(BF16)


def _gla_kernel(xc_ref, xn_ref, g_ref, win_ref, wg2_ref, bg_ref, gn_ref, wout_ref,
                *rest, tm, kd, vd, tiles_per_seq, ncast):
    o_ref = rest[ncast]
    pa_s, la_a, pb_s, la_b, og_s, st_s = rest[2 * ncast + 1:]
    _run_casts(rest[:ncast], rest[ncast + 1:2 * ncast + 1])
    C = CHUNK
    step = pl.program_id(0)
    wts = (g_ref, win_ref, wg2_ref, bg_ref)

    @pl.when(step == 0)
    def _():
        for task in _gla_project(xc_ref, 0, tm, *wts, pa_s, la_a):
            task()

    @pl.when((2 * step) % tiles_per_seq == 0)
    def _():
        st_s[...] = jnp.zeros_like(st_s)

    ri = lax.broadcasted_iota(jnp.int32, (C, C), 0)
    ci = lax.broadcasted_iota(jnp.int32, (C, C), 1)
    tri = (ri >= ci).astype(BF16)
    tri3 = jnp.concatenate([tri, tri, tri, jnp.zeros_like(tri)], axis=1)
    halves = (32, 16, 8, 4, 2, 1)
    lvl_mask = [((ri // (2 * hf)) == (ci // (2 * hf))) & ((ri % (2 * hf)) >= hf) & ((ci % (2 * hf)) < hf)
                for hf in halves]
    diag = ri == ci
    rowc = lax.broadcasted_iota(jnp.int32, (C, kd // GLA_HEADS), 0)
    upper = [(rowc % (2 * hf)) >= hf for hf in halves]
    consts = (tri3, halves, lvl_mask, diag, upper)

    nxt = _gla_project(xc_ref, tm, tm, *wts, pb_s, la_b)
    _gla_chunks(pa_s, la_a, og_s, st_s, gn_ref, consts, nxt, tm=tm, kd=kd, vd=vd)
    o_ref[0:tm, :] = xc_ref[0:tm, :] + _dot(og_s[...], wout_ref[...])

    nxt = _gla_project(xn_ref, 0, tm, *wts, pa_s, la_a)
    _gla_chunks(pb_s, la_b, og_s, st_s, gn_ref, consts, nxt, tm=tm, kd=kd, vd=vd)
    o_ref[tm:2 * tm, :] = xc_ref[tm:2 * tm, :] + _dot(og_s[...], wout_ref[...])


def _gla(x2, g, win, wg2, bg, gn, wout, *, tm, seq_len, cast=()):
    m, d = x2.shape
    kd = wg2.shape[1]
    vd = wout.shape[0]
    dk = kd // GLA_HEADS
    dv = vd // GLA_HEADS
    nproj = win.shape[1] - wg2.shape[0]
    ntiles = m // tm
    assert seq_len % (2 * tm) == 0 and nproj % GLA_PROJ_PIECE[1] == 0 and tm % GLA_PROJ_PIECE[0] == 0
    c_in, c_out, c_shapes = _cast_through(cast, ntiles // 2)
    return pl.pallas_call(
        functools.partial(_gla_kernel, tm=tm, kd=kd, vd=vd, tiles_per_seq=seq_len // tm, ncast=len(cast)),
        grid=(ntiles // 2,),
        in_specs=[
            pl.BlockSpec((2 * tm, d), lambda i: (i, 0)),
            pl.BlockSpec((tm, d), lambda i: (jnp.minimum(2 * i + 2, ntiles - 1), 0)),
            _resident((1, d)),
            _resident(win.shape),
            _resident(wg2.shape),
            _resident((1, kd)),
            _resident(gn.shape),
            _resident((vd, d)),
        ] + c_in,
        out_specs=[pl.BlockSpec((2 * tm, d), lambda i: (i, 0))] + c_out,
        out_shape=[jax.ShapeDtypeStruct((m, d), F32)] + c_shapes,
        scratch_shapes=[
            pltpu.VMEM((tm, nproj), F32),
            pltpu.VMEM((tm, kd), F32),
            pltpu.VMEM((tm, nproj), F32),
            pltpu.VMEM((tm, kd), F32),
            pltpu.VMEM((tm, vd), BF16),
            pltpu.VMEM((GLA_HEADS, dk, dv), F32),
        ],
        compiler_params=pltpu.CompilerParams(
            dimension_semantics=("arbitrary",), vmem_limit_bytes=VMEM_LIMIT_BYTES),
        name="gla",
    )(x2, x2, g, win, wg2, bg, gn, wout, *[w for w, _ in cast])


def kernel(x, mix_norm, a_w_in, a_ln_g, a_ln_b, a_w_s, a_b_s, a_w_out, b_w_in, b_w_g2, b_b_g, b_gn_g, b_w_out, ffn_norm, ffn_w_gate, ffn_w_up, ffn_w_down, final_norm):
    b, t, d = x.shape
    depth = mix_norm.shape[0]
    assert depth == 2 and a_w_in.shape[0] == 1 and b_w_in.shape[0] == 1
    m = b * t
    bf = lambda w: w.astype(BF16)
    row = lambda v: v.reshape(1, -1)

    x2 = x.reshape(m, d)
    x2, wg0, wu0, wd0 = _sgu(
        x2, row(mix_norm[0]), bf(a_w_in[0]), row(a_ln_g[0]), row(a_ln_b[0]),
        a_w_s[0], a_b_s[0].T, bf(a_w_out[0]), tm=512,
        cast=((ffn_w_gate, 0), (ffn_w_up, 0), (ffn_w_down, 0)))
    x2, b_win, b_wout = _ffn(
        x2, row(ffn_norm[0]), wg0, wu0, wd0, row(final_norm), final_norm=False, tm=1024,
        cast=((b_w_in, 0), (b_w_out, 0)))
    x2, wg1, wu1, wd1 = _gla(
        x2, row(mix_norm[1]), b_win, b_w_g2[0], row(b_b_g[0]), b_gn_g[0], b_wout, tm=256, seq_len=t,
        cast=((ffn_w_gate, 1), (ffn_w_up, 1), (ffn_w_down, 1)))
    (x2,) = _ffn(x2, row(ffn_norm[1]), wg1, wu1, wd1, row(final_norm), final_norm=True, tm=1024)
    return x2.reshape(b, t, d)
```

```python
import functools

import jax
import jax.numpy as jnp
from jax import lax
from jax.experimental import pallas as pl
from jax.experimental.pallas import tpu as pltpu

F32 = jnp.float32
BF16 = jnp.bfloat16

EPS = 1e-6
CHUNK = 64
SG_BLOCK = 128
SG_GROUPS = 8
SG_GROUPS_PER_CHUNK = 2
GLA_HEADS = 4
GLA_TAU = 16.0
GLA_SKEW = 1
GLA_PROJ_PIECE = (128, 512)

BF16_SUBLANES = 16
MXU_N = 256
FFN_HIDDEN_CHUNKS = 4

VMEM_LIMIT_BYTES = 56 * 1024 * 1024


def _rms(x, g):
    return x * lax.rsqrt(jnp.mean(x * x, axis=-1, keepdims=True) + EPS) * g


def _gelu(x):
    return 0.5 * x * (1.0 + lax.erf(x * (2.0 ** -0.5)))


def _dot(a, b):
    return jnp.dot(a, b, preferred_element_type=F32)


def _dot_nt(a, b):
    return lax.dot_general(a, b, (((1,), (1,)), ((), ())), preferred_element_type=F32)


def _dot_tn(a, b):
    return lax.dot_general(a, b, (((0,), (0,)), ((), ())), preferred_element_type=F32)


def _resident(shape):
    return pl.BlockSpec(shape, lambda *_: (0,) * len(shape))


def _cast_through(weights, nsteps):
    in_specs, out_specs, out_shapes = [], [], []
    for w, layer in weights:
        _, r, c = w.shape
        per = 1 if r % (nsteps * BF16_SUBLANES) == 0 else 2
        rb = r * per // nsteps
        assert rb % BF16_SUBLANES == 0 and rb * nsteps == r * per, (r, c, nsteps)
        in_specs.append(pl.BlockSpec((None, rb, c), lambda i, layer=layer, per=per: (layer, i // per, 0)))
        out_specs.append(pl.BlockSpec((rb, c), lambda i, per=per: (i // per, 0)))
        out_shapes.append(jax.ShapeDtypeStruct((r, c), BF16))
    return in_specs, out_specs, out_shapes


def _run_casts(src_refs, dst_refs):
    for src, dst in zip(src_refs, dst_refs):
        dst[...] = src[...].astype(BF16)


def _ffn_kernel(x_ref, g_ref, wg_ref, wu_ref, wd_ref, fin_ref, *rest, final_norm, ncast):
    o_ref = rest[ncast]
    _run_casts(rest[:ncast], rest[ncast + 1:])
    x = x_ref[...]
    h = _rms(x, g_ref[...]).astype(BF16)
    hid = wg_ref.shape[1]
    ntiles = hid // MXU_N
    bounds = [MXU_N * (ntiles * j // FFN_HIDDEN_CHUNKS) for j in range(FFN_HIDDEN_CHUNKS + 1)]
    y = x
    for c0, c1 in zip(bounds[:-1], bounds[1:]):
        gate = _dot(h, wg_ref[:, c0:c1])
        up = _dot(h, wu_ref[:, c0:c1])
        a = (jax.nn.silu(gate) * up).astype(BF16)
        y = y + _dot(a, wd_ref[c0:c1, :])
    if final_norm:
        y = _rms(y, fin_ref[...])
    o_ref[...] = y


def _ffn(x2, g, wg, wu, wd, fin, *, final_norm, tm, cast=()):
    m, d = x2.shape
    hid = wg.shape[1]
    c_in, c_out, c_shapes = _cast_through(cast, m // tm)
    return pl.pallas_call(
        functools.partial(_ffn_kernel, final_norm=final_norm, ncast=len(cast)),
        grid=(m // tm,),
        in_specs=[
            pl.BlockSpec((tm, d), lambda i: (i, 0)),
            _resident((1, d)),
            _resident((d, hid)),
            _resident((d, hid)),
            _resident((hid, d)),
            _resident((1, d)),
        ] + c_in,
        out_specs=[pl.BlockSpec((tm, d), lambda i: (i, 0))] + c_out,
        out_shape=[jax.ShapeDtypeStruct((m, d), F32)] + c_shapes,
        compiler_params=pltpu.CompilerParams(
            dimension_semantics=("arbitrary",), vmem_limit_bytes=VMEM_LIMIT_BYTES),
        name="swiglu",
    )(x2, g, wg, wu, wd, fin, *[w for w, _ in cast])


def _sgu_kernel(x_ref, g_ref, win_ref, lng_ref, lnb_ref, ws_ref, bst_ref, wout_ref,
                *rest, tm, ncast):
    o_ref = rest[ncast]
    v_s, y_s = rest[2 * ncast + 1:]
    _run_casts(rest[:ncast], rest[ncast + 1:2 * ncast + 1])
    width = v_s.shape[1]
    gc = width // SG_GROUPS
    lanes = 128
    x = x_ref[...]
    h = _rms(x, g_ref[...]).astype(BF16)

    gpc = SG_GROUPS_PER_CHUNK
    cw = gpc * gc
    p1 = jnp.zeros((tm, lanes), F32)
    p2 = jnp.zeros((tm, lanes), F32)
    for c in range(SG_GROUPS // gpc):
        v = _gelu(_dot(h, win_ref[:, width + c * cw:width + (c + 1) * cw]))
        v_s[:, c * cw:(c + 1) * cw] = v
        for l in range(cw // lanes):
            vl = v[:, l * lanes:(l + 1) * lanes]
            p1 = p1 + vl
            p2 = p2 + vl * vl
    mu = jnp.sum(p1, axis=-1, keepdims=True) * (1.0 / width)
    var = jnp.sum(p2, axis=-1, keepdims=True) * (1.0 / width) - mu * mu
    rstd = lax.rsqrt(var + EPS)
    shift = -mu * rstd

    ti = lax.broadcasted_iota(jnp.int32, (SG_BLOCK, SG_BLOCK), 0) // CHUNK
    si = lax.broadcasted_iota(jnp.int32, (SG_BLOCK, SG_BLOCK), 1) // CHUNK
    causal = ti >= si
    for c in range(SG_GROUPS // gpc):
        ccols = slice(c * cw, (c + 1) * cw)
        u = _gelu(_dot(h, win_ref[:, ccols]))
        vn = ((v_s[:, ccols] * rstd + shift) * lng_ref[:, ccols] + lnb_ref[:, ccols]).astype(BF16)
        for gg in range(gpc):
            g = c * gpc + gg
            gcols = slice(gg * gc, (gg + 1) * gc)
            ws = jnp.where(causal, ws_ref[g], 0.0).astype(BF16)
            bias = bst_ref[:, g:g + 1]
            for w in range(tm // SG_BLOCK):
                rows = slice(w * SG_BLOCK, (w + 1) * SG_BLOCK)
                s = _dot(ws, vn[rows, gcols]) + bias
                y_s[rows, g * gc:(g + 1) * gc] = (u[rows, gcols] * s).astype(BF16)
    o_ref[...] = x + _dot(y_s[...], wout_ref[...])


def _sgu(x2, g, win, lng, lnb, ws, bst, wout, *, tm, cast=()):
    m, d = x2.shape
    width = wout.shape[0]
    c_in, c_out, c_shapes = _cast_through(cast, m // tm)
    return pl.pallas_call(
        functools.partial(_sgu_kernel, tm=tm, ncast=len(cast)),
        grid=(m // tm,),
        in_specs=[
            pl.BlockSpec((tm, d), lambda i: (i, 0)),
            _resident((1, d)),
            _resident((d, 2 * width)),
            _resident((1, width)),
            _resident((1, width)),
            _resident(ws.shape),
            _resident(bst.shape),
            _resident((width, d)),
        ] + c_in,
        out_specs=[pl.BlockSpec((tm, d), lambda i: (i, 0))] + c_out,
        out_shape=[jax.ShapeDtypeStruct((m, d), F32)] + c_shapes,
        scratch_shapes=[
            pltpu.VMEM((tm, width), F32),
            pltpu.VMEM((tm, width), BF16),
        ],
        compiler_params=pltpu.CompilerParams(
            dimension_semantics=("arbitrary",), vmem_limit_bytes=VMEM_LIMIT_BYTES),
        name="sgu",
    )(x2, g, win, lng, lnb, ws, bst, wout, *[w for w, _ in cast])


def _ref_rows(G, half):
    n, w = G.shape
    blk = 2 * half
    if blk >= 8:
        return jnp.concatenate(
            [jnp.broadcast_to(G[b * blk + half - 1:b * blk + half, :], (blk, w))
             for b in range(n // blk)], axis=0)
    assert blk == 4
    sub = lax.broadcasted_iota(jnp.int32, (8, w), 0)
    tiles = []
    for t in range(n // 8):
        lo = jnp.broadcast_to(G[8 * t + 1:8 * t + 2, :], (8, w))
        hi = jnp.broadcast_to(G[8 * t + 5:8 * t + 6, :], (8, w))
        tiles.append(jnp.where(sub < 4, lo, hi))
    return jnp.concatenate(tiles, axis=0)


def _gla_project(x_ref, row0, tm, g_ref, win_ref, wg2_ref, bg_ref, proj_ref, la_ref):
    state = {}
    ncol = proj_ref.shape[1]
    pr, pc = GLA_PROJ_PIECE
    tasks = []

    def norm(r):
        def run():
            x = x_ref[row0 + r * pr:row0 + (r + 1) * pr, :]
            state[r] = _rms(x, g_ref[...]).astype(BF16)
        return run

    def cols(r, p):
        def run():
            proj_ref[r * pr:(r + 1) * pr, p * pc:(p + 1) * pc] = _dot(state[r], win_ref[:, p * pc:(p + 1) * pc])
        return run

    def gate(r):
        def run():
            gz = _dot(state[r], win_ref[:, ncol:])
            z = _dot(gz.astype(BF16), wg2_ref[...].astype(BF16)) + bg_ref[...]
            la_ref[r * pr:(r + 1) * pr, :] = (jnp.minimum(z, 0.0) - jnp.log1p(jnp.exp(-jnp.abs(z)))) / GLA_TAU
        return run

    for r in range(tm // pr):
        tasks += [norm(r), gate(r)] + [cols(r, p) for p in range(ncol // pc)]
    return tasks


def _gla_chunks(proj_ref, la_ref, og_s, st_s, gn_ref, consts, between, *, tm, kd, vd):
    dk = kd // GLA_HEADS
    dv = vd // GLA_HEADS
    C = CHUNK
    tri3, halves, lvl_mask, diag, upper = consts
    scale = dk ** -0.5
    between = list(between)
    nchunks = tm // C
    nslots = nchunks * GLA_HEADS

    def chunk_decay(c):
        la_all = la_ref[c * C:(c + 1) * C, :]
        la_hi = la_all.astype(BF16)
        r1 = la_all - la_hi.astype(F32)
        la_mid = r1.astype(BF16)
        la_lo = (r1 - la_mid.astype(F32)).astype(BF16)
        return la_all, _dot(tri3, jnp.concatenate([la_hi, la_mid, la_lo, jnp.zeros_like(la_hi)], axis=0))

    def scores(c, hd, la_all, G_all):
        rows = slice(c * C, (c + 1) * C)
        ks_ = slice(hd * dk, (hd + 1) * dk)
        q = proj_ref[rows, hd * dk:(hd + 1) * dk] * scale
        k = proj_ref[rows, kd + hd * dk:kd + (hd + 1) * dk]
        la = la_all[:, ks_]
        G = G_all[:, ks_]
        ps = [jnp.sum(q * k, axis=-1, keepdims=True)]
        for lvl, hf in enumerate(halves):
            if hf == 1:
                e = jnp.exp(jnp.where(upper[lvl], la, 0.0))
            else:
                e = jnp.exp(-jnp.abs(G - _ref_rows(G, hf)))
            xl = (jnp.where(upper[lvl], q, k) * e).astype(BF16)
            ps.append(_dot_nt(xl, xl))
        g_last = G[C - 1:C, :]
        qd = (q * jnp.exp(G)).astype(BF16)
        kdec = (k * jnp.exp(g_last - G)).astype(BF16)
        dec = jnp.exp(g_last)
        return ps, qd, kdec, dec

    def outputs(c, hd, ps, qd, kdec, dec):
        rows = slice(c * C, (c + 1) * C)
        vs_ = slice(2 * kd + hd * dv, 2 * kd + (hd + 1) * dv)
        rs_ = slice(2 * kd + vd + hd * dv, 2 * kd + vd + (hd + 1) * dv)
        a = jnp.where(diag, ps[0], 0.0)
        for lvl in range(len(halves)):
            a = jnp.where(lvl_mask[lvl], ps[lvl + 1], a)
        v = proj_ref[rows, vs_].astype(BF16)
        st = st_s[hd]
        o = _dot(a.astype(BF16), v) + _dot(qd, st.astype(BF16))
        dcol = jnp.transpose(jnp.broadcast_to(dec, (dk, dk)))
        st_s[hd] = st * jnp.concatenate([dcol] * (dv // dk), axis=1) + _dot_tn(kdec, v)
        o = o * lax.rsqrt(jnp.mean(o * o, axis=-1, keepdims=True) + EPS) * gn_ref[hd:hd + 1, :]
        r = proj_ref[rows, rs_]
        og_s[rows, hd * dv:(hd + 1) * dv] = (jax.nn.silu(r) * o).astype(BF16)

    decay = {0: chunk_decay(0)}
    pending = []
    for slot in range(nslots):
        c, hd = divmod(slot, GLA_HEADS)
        if hd == 0 and c + 1 < nchunks:
            decay[c + 1] = chunk_decay(c + 1)
        pending.append((c, hd) + scores(c, hd, *decay[c]))
        for task in between[slot * len(between) // nslots:(slot + 1) * len(between) // nslots]:
            task()
        if len(pending) > GLA_SKEW:
            outputs(*pending.pop(0))
    for item in pending:
        outputs(*item)


def _gla_kernel(xc_ref, xn_ref, g_ref, win_ref, wg2_ref, bg_ref, gn_ref, wout_ref,
                *rest, tm, kd, vd, tiles_per_seq, ncast):
    o_ref = rest[ncast]
    pa_s, la_a, pb_s, la_b, og_s, st_s = rest[2 * ncast + 1:]
    _run_casts(rest[:ncast], rest[ncast + 1:2 * ncast + 1])
    C = CHUNK
    step = pl.program_id(0)
    wts = (g_ref, win_ref, wg2_ref, bg_ref)

    @pl.when(step == 0)
    def _():
        for task in _gla_project(xc_ref, 0, tm, *wts, pa_s, la_a):
            task()

    @pl.when((2 * step) % tiles_per_seq == 0)
    def _():
        st_s[...] = jnp.zeros_like(st_s)

    ri = lax.broadcasted_iota(jnp.int32, (C, C), 0)
    ci = lax.broadcasted_iota(jnp.int32, (C, C), 1)
    tri = (ri >= ci).astype(BF16)
    tri3 = jnp.concatenate([tri, tri, tri, jnp.zeros_like(tri)], axis=1)
    halves = (32, 16, 8, 4, 2, 1)
    lvl_mask = [((ri // (2 * hf)) == (ci // (2 * hf))) & ((ri % (2 * hf)) >= hf) & ((ci % (2 * hf)) < hf)
                for hf in halves]
    diag = ri == ci
    rowc = lax.broadcasted_iota(jnp.int32, (C, kd // GLA_HEADS), 0)
    upper = [(rowc % (2 * hf)) >= hf for hf in halves]
    consts = (tri3, halves, lvl_mask, diag, upper)

    nxt = _gla_project(xc_ref, tm, tm, *wts, pb_s, la_b)
    _gla_chunks(pa_s, la_a, og_s, st_s, gn_ref, consts, nxt, tm=tm, kd=kd, vd=vd)
    o_ref[0:tm, :] = xc_ref[0:tm, :] + _dot(og_s[...], wout_ref[...])

    nxt = _gla_project(xn_ref, 0, tm, *wts, pa_s, la_a)
    _gla_chunks(pb_s, la_b, og_s, st_s, gn_ref, consts, nxt, tm=tm, kd=kd, vd=vd)
    o_ref[tm:2 * tm, :] = xc_ref[tm:2 * tm, :] + _dot(og_s[...], wout_ref[...])


def _gla(x2, g, win, wg2, bg, gn, wout, *, tm, seq_len, cast=()):
    m, d = x2.shape
    kd = wg2.shape[1]
    vd = wout.shape[0]
    dk = kd // GLA_HEADS
    dv = vd // GLA_HEADS
    nproj = win.shape[1] - wg2.shape[0]
    ntiles = m // tm
    assert seq_len % (2 * tm) == 0 and nproj % GLA_PROJ_PIECE[1] == 0 and tm % GLA_PROJ_PIECE[0] == 0
    c_in, c_out, c_shapes = _cast_through(cast, ntiles // 2)
    return pl.pallas_call(
        functools.partial(_gla_kernel, tm=tm, kd=kd, vd=vd, tiles_per_seq=seq_len // tm, ncast=len(cast)),
        grid=(ntiles // 2,),
        in_specs=[
            pl.BlockSpec((2 * tm, d), lambda i: (i, 0)),
            pl.BlockSpec((tm, d), lambda i: (jnp.minimum(2 * i + 2, ntiles - 1), 0)),
            _resident((1, d)),
            _resident(win.shape),
            _resident(wg2.shape),
            _resident((1, kd)),
            _resident(gn.shape),
            _resident((vd, d)),
        ] + c_in,
        out_specs=[pl.BlockSpec((2 * tm, d), lambda i: (i, 0))] + c_out,
        out_shape=[jax.ShapeDtypeStruct((m, d), F32)] + c_shapes,
        scratch_shapes=[
            pltpu.VMEM((tm, nproj), F32),
            pltpu.VMEM((tm, kd), F32),
            pltpu.VMEM((tm, nproj), F32),
            pltpu.VMEM((tm, kd), F32),
            pltpu.VMEM((tm, vd), BF16),
            pltpu.VMEM((GLA_HEADS, dk, dv), F32),
        ],
        compiler_params=pltpu.CompilerParams(
            dimension_semantics=("arbitrary",), vmem_limit_bytes=VMEM_LIMIT_BYTES),
        name="gla",
    )(x2, x2, g, win, wg2, bg, gn, wout, *[w for w, _ in cast])


def kernel(x, mix_norm, a_w_in, a_ln_g, a_ln_b, a_w_s, a_b_s, a_w_out, b_w_in, b_w_g2, b_b_g, b_gn_g, b_w_out, ffn_norm, ffn_w_gate, ffn_w_up, ffn_w_down, final_norm):
    b, t, d = x.shape
    depth = mix_norm.shape[0]
    assert depth == 2 and a_w_in.shape[0] == 1 and b_w_in.shape[0] == 1
    m = b * t
    bf = lambda w: w.astype(BF16)
    row = lambda v: v.reshape(1, -1)

    x2 = x.reshape(m, d)
    x2, wg0, wu0, wd0 = _sgu(
        x2, row(mix_norm[0]), bf(a_w_in[0]), row(a_ln_g[0]), row(a_ln_b[0]),
        a_w_s[0], a_b_s[0].T, bf(a_w_out[0]), tm=512,
        cast=((ffn_w_gate, 0), (ffn_w_up, 0), (ffn_w_down, 0)))
    x2, b_win, b_wout = _ffn(
        x2, row(ffn_norm[0]), wg0, wu0, wd0, row(final_norm), final_norm=False, tm=1024,
        cast=((b_w_in, 0), (b_w_out, 0)))
    x2, wg1, wu1, wd1 = _gla(
        x2, row(mix_norm[1]), b_win, b_w_g2[0], row(b_b_g[0]), b_gn_g[0], b_wout, tm=256, seq_len=t,
        cast=((ffn_w_gate, 1), (ffn_w_up, 1), (ffn_w_down, 1)))
    (x2,) = _ffn(x2, row(ffn_norm[1]), wg1, wu1, wd1, row(final_norm), final_norm=True, tm=1024)
    return x2.reshape(b, t, d)
```

```python
import functools

import jax
import jax.numpy as jnp
from jax import lax
from jax.experimental import pallas as pl
from jax.experimental.pallas import tpu as pltpu

F32 = jnp.float32
BF16 = jnp.bfloat16

EPS = 1e-6
CHUNK = 64
SG_BLOCK = 128
SG_GROUPS = 8
SG_GROUPS_PER_CHUNK = 2
GLA_HEADS = 4
GLA_TAU = 16.0
GLA_TASK_START = 0
GLA_SKEW = 1
GLA_PROJ_PIECE = (128, 512)

BF16_SUBLANES = 16
MXU_N = 256
FFN_HIDDEN_CHUNKS = 4

VMEM_LIMIT_BYTES = 56 * 1024 * 1024


def _rms(x, g):
    return x * lax.rsqrt(jnp.mean(x * x, axis=-1, keepdims=True) + EPS) * g


def _gelu(x):
    return 0.5 * x * (1.0 + lax.erf(x * (2.0 ** -0.5)))


def _dot(a, b):
    return jnp.dot(a, b, preferred_element_type=F32)


def _dot_nt(a, b):
    return lax.dot_general(a, b, (((1,), (1,)), ((), ())), preferred_element_type=F32)


def _dot_tn(a, b):
    return lax.dot_general(a, b, (((0,), (0,)), ((), ())), preferred_element_type=F32)


def _resident(shape):
    return pl.BlockSpec(shape, lambda *_: (0,) * len(shape))


def _cast_through(weights, nsteps):
    in_specs, out_specs, out_shapes = [], [], []
    for w, layer in weights:
        _, r, c = w.shape
        per = 1 if r % (nsteps * BF16_SUBLANES) == 0 else 2
        rb = r * per // nsteps
        assert rb % BF16_SUBLANES == 0 and rb * nsteps == r * per, (r, c, nsteps)
        in_specs.append(pl.BlockSpec((None, rb, c), lambda i, layer=layer, per=per: (layer, i // per, 0)))
        out_specs.append(pl.BlockSpec((rb, c), lambda i, per=per: (i // per, 0)))
        out_shapes.append(jax.ShapeDtypeStruct((r, c), BF16))
    return in_specs, out_specs, out_shapes


def _run_casts(src_refs, dst_refs):
    for src, dst in zip(src_refs, dst_refs):
        dst[...] = src[...].astype(BF16)


def _ffn_kernel(x_ref, g_ref, wg_ref, wu_ref, wd_ref, fin_ref, *rest, final_norm, ncast):
    o_ref = rest[ncast]
    _run_casts(rest[:ncast], rest[ncast + 1:])
    x = x_ref[...]
    h = _rms(x, g_ref[...]).astype(BF16)
    hid = wg_ref.shape[1]
    ntiles = hid // MXU_N
    bounds = [MXU_N * (ntiles * j // FFN_HIDDEN_CHUNKS) for j in range(FFN_HIDDEN_CHUNKS + 1)]
    y = x
    for c0, c1 in zip(bounds[:-1], bounds[1:]):
        gate = _dot(h, wg_ref[:, c0:c1])
        up = _dot(h, wu_ref[:, c0:c1])
        a = (jax.nn.silu(gate) * up).astype(BF16)
        y = y + _dot(a, wd_ref[c0:c1, :])
    if final_norm:
        y = _rms(y, fin_ref[...])
    o_ref[...] = y


def _ffn(x2, g, wg, wu, wd, fin, *, final_norm, tm, cast=()):
    m, d = x2.shape
    hid = wg.shape[1]
    c_in, c_out, c_shapes = _cast_through(cast, m // tm)
    return pl.pallas_call(
        functools.partial(_ffn_kernel, final_norm=final_norm, ncast=len(cast)),
        grid=(m // tm,),
        in_specs=[
            pl.BlockSpec((tm, d), lambda i: (i, 0)),
            _resident((1, d)),
            _resident((d, hid)),
            _resident((d, hid)),
            _resident((hid, d)),
            _resident((1, d)),
        ] + c_in,
        out_specs=[pl.BlockSpec((tm, d), lambda i: (i, 0))] + c_out,
        out_shape=[jax.ShapeDtypeStruct((m, d), F32)] + c_shapes,
        compiler_params=pltpu.CompilerParams(
            dimension_semantics=("arbitrary",), vmem_limit_bytes=VMEM_LIMIT_BYTES),
        name="swiglu",
    )(x2, g, wg, wu, wd, fin, *[w for w, _ in cast])


def _sgu_kernel(x_ref, g_ref, win_ref, lng_ref, lnb_ref, ws_ref, bst_ref, wout_ref,
                *rest, tm, ncast):
    o_ref = rest[ncast]
    v_s, y_s = rest[2 * ncast + 1:]
    _run_casts(rest[:ncast], rest[ncast + 1:2 * ncast + 1])
    width = v_s.shape[1]
    gc = width // SG_GROUPS
    lanes = 128
    x = x_ref[...]
    h = _rms(x, g_ref[...]).astype(BF16)

    gpc = SG_GROUPS_PER_CHUNK
    cw = gpc * gc
    p1 = jnp.zeros((tm, lanes), F32)
    p2 = jnp.zeros((tm, lanes), F32)
    for c in range(SG_GROUPS // gpc):
        v = _gelu(_dot(h, win_ref[:, width + c * cw:width + (c + 1) * cw]))
        v_s[:, c * cw:(c + 1) * cw] = v
        for l in range(cw // lanes):
            vl = v[:, l * lanes:(l + 1) * lanes]
            p1 = p1 + vl
            p2 = p2 + vl * vl
    mu = jnp.sum(p1, axis=-1, keepdims=True) * (1.0 / width)
    var = jnp.sum(p2, axis=-1, keepdims=True) * (1.0 / width) - mu * mu
    rstd = lax.rsqrt(var + EPS)
    shift = -mu * rstd

    ti = lax.broadcasted_iota(jnp.int32, (SG_BLOCK, SG_BLOCK), 0) // CHUNK
    si = lax.broadcasted_iota(jnp.int32, (SG_BLOCK, SG_BLOCK), 1) // CHUNK
    causal = ti >= si
    for c in range(SG_GROUPS // gpc):
        ccols = slice(c * cw, (c + 1) * cw)
        u = _gelu(_dot(h, win_ref[:, ccols]))
        vn = ((v_s[:, ccols] * rstd + shift) * lng_ref[:, ccols] + lnb_ref[:, ccols]).astype(BF16)
        for gg in range(gpc):
            g = c * gpc + gg
            gcols = slice(gg * gc, (gg + 1) * gc)
            ws = jnp.where(causal, ws_ref[g], 0.0).astype(BF16)
            bias = bst_ref[:, g:g + 1]
            for w in range(tm // SG_BLOCK):
                rows = slice(w * SG_BLOCK, (w + 1) * SG_BLOCK)
                s = _dot(ws, vn[rows, gcols]) + bias
                y_s[rows, g * gc:(g + 1) * gc] = (u[rows, gcols] * s).astype(BF16)
    o_ref[...] = x + _dot(y_s[...], wout_ref[...])


def _sgu(x2, g, win, lng, lnb, ws, bst, wout, *, tm, cast=()):
    m, d = x2.shape
    width = wout.shape[0]
    c_in, c_out, c_shapes = _cast_through(cast, m // tm)
    return pl.pallas_call(
        functools.partial(_sgu_kernel, tm=tm, ncast=len(cast)),
        grid=(m // tm,),
        in_specs=[
            pl.BlockSpec((tm, d), lambda i: (i, 0)),
            _resident((1, d)),
            _resident((d, 2 * width)),
            _resident((1, width)),
            _resident((1, width)),
            _resident(ws.shape),
            _resident(bst.shape),
            _resident((width, d)),
        ] + c_in,
        out_specs=[pl.BlockSpec((tm, d), lambda i: (i, 0))] + c_out,
        out_shape=[jax.ShapeDtypeStruct((m, d), F32)] + c_shapes,
        scratch_shapes=[
            pltpu.VMEM((tm, width), F32),
            pltpu.VMEM((tm, width), BF16),
        ],
        compiler_params=pltpu.CompilerParams(
            dimension_semantics=("arbitrary",), vmem_limit_bytes=VMEM_LIMIT_BYTES),
        name="sgu",
    )(x2, g, win, lng, lnb, ws, bst, wout, *[w for w, _ in cast])


def _ref_rows(G, half):
    n, w = G.shape
    blk = 2 * half
    if blk >= 8:
        return jnp.concatenate(
            [jnp.broadcast_to(G[b * blk + half - 1:b * blk + half, :], (blk, w))
             for b in range(n // blk)], axis=0)
    assert blk == 4
    sub = lax.broadcasted_iota(jnp.int32, (8, w), 0)
    tiles = []
    for t in range(n // 8):
        lo = jnp.broadcast_to(G[8 * t + 1:8 * t + 2, :], (8, w))
        hi = jnp.broadcast_to(G[8 * t + 5:8 * t + 6, :], (8, w))
        tiles.append(jnp.where(sub < 4, lo, hi))
    return jnp.concatenate(tiles, axis=0)


def _gla_project(x_ref, row0, tm, g_ref, win_ref, wg2_ref, bg_ref, proj_ref, la_ref):
    state = {}
    ncol = proj_ref.shape[1]
    pr, pc = GLA_PROJ_PIECE
    tasks = []

    def norm(r):
        def run():
            x = x_ref[row0 + r * pr:row0 + (r + 1) * pr, :]
            state[r] = _rms(x, g_ref[...]).astype(BF16)
        return run

    def cols(r, p):
        def run():
            proj_ref[r * pr:(r + 1) * pr, p * pc:(p + 1) * pc] = _dot(state[r], win_ref[:, p * pc:(p + 1) * pc])
        return run

    def gate(r):
        def run():
            gz = _dot(state[r], win_ref[:, ncol:])
            z = _dot(gz.astype(BF16), wg2_ref[...].astype(BF16)) + bg_ref[...]
            la_ref[r * pr:(r + 1) * pr, :] = (jnp.minimum(z, 0.0) - jnp.log1p(jnp.exp(-jnp.abs(z)))) / GLA_TAU
        return run

    for r in range(tm // pr):
        tasks += [norm(r), gate(r)] + [cols(r, p) for p in range(ncol // pc)]
    return tasks


def _gla_chunks(proj_ref, la_ref, og_s, st_s, gn_ref, consts, between, *, tm, kd, vd):
    dk = kd // GLA_HEADS
    dv = vd // GLA_HEADS
    C = CHUNK
    tri3, halves, lvl_mask, diag, upper = consts
    scale = dk ** -0.5
    between = list(between)
    nchunks = tm // C
    nslots = nchunks * GLA_HEADS

    def chunk_decay(c):
        la_all = la_ref[c * C:(c + 1) * C, :]
        la_hi = la_all.astype(BF16)
        r1 = la_all - la_hi.astype(F32)
        la_mid = r1.astype(BF16)
        la_lo = (r1 - la_mid.astype(F32)).astype(BF16)
        return la_all, _dot(tri3, jnp.concatenate([la_hi, la_mid, la_lo, jnp.zeros_like(la_hi)], axis=0))

    def scores(c, hd, la_all, G_all):
        rows = slice(c * C, (c + 1) * C)
        ks_ = slice(hd * dk, (hd + 1) * dk)
        q = proj_ref[rows, hd * dk:(hd + 1) * dk] * scale
        k = proj_ref[rows, kd + hd * dk:kd + (hd + 1) * dk]
        la = la_all[:, ks_]
        G = G_all[:, ks_]
        ps = [jnp.sum(q * k, axis=-1, keepdims=True)]
        for lvl, hf in enumerate(halves):
            if hf == 1:
                e = jnp.exp(jnp.where(upper[lvl], la, 0.0))
            else:
                e = jnp.exp(-jnp.abs(G - _ref_rows(G, hf)))
            xl = (jnp.where(upper[lvl], q, k) * e).astype(BF16)
            ps.append(_dot_nt(xl, xl))
        g_last = G[C - 1:C, :]
        qd = (q * jnp.exp(G)).astype(BF16)
        kdec = (k * jnp.exp(g_last - G)).astype(BF16)
        dec = jnp.exp(g_last)
        return ps, qd, kdec, dec

    def outputs(c, hd, ps, qd, kdec, dec):
        rows = slice(c * C, (c + 1) * C)
        vs_ = slice(2 * kd + hd * dv, 2 * kd + (hd + 1) * dv)
        rs_ = slice(2 * kd + vd + hd * dv, 2 * kd + vd + (hd + 1) * dv)
        a = jnp.where(diag, ps[0], 0.0)
        for lvl in range(len(halves)):
            a = jnp.where(lvl_mask[lvl], ps[lvl + 1], a)
        v = proj_ref[rows, vs_].astype(BF16)
        st = st_s[hd]
        pad = jnp.zeros((C, MXU_N - dk - C), BF16)
        lhs = jnp.concatenate([qd, a.astype(BF16), pad], axis=1)
        rhs = jnp.concatenate([st.astype(BF16), v, jnp.zeros((MXU_N - dk - C, dv), BF16)], axis=0)
        o = _dot(lhs, rhs)
        dcol = jnp.transpose(jnp.broadcast_to(dec, (dk, dk)))
        st_s[hd] = st * jnp.concatenate([dcol] * (dv // dk), axis=1) + _dot_tn(kdec, v)
        o = o * lax.rsqrt(jnp.mean(o * o, axis=-1, keepdims=True) + EPS) * gn_ref[hd:hd + 1, :]
        r = proj_ref[rows, rs_]
        og_s[rows, hd * dv:(hd + 1) * dv] = (jax.nn.silu(r) * o).astype(BF16)

    decay = {0: chunk_decay(0)}
    pending = []
    for slot in range(nslots):
        c, hd = divmod(slot, GLA_HEADS)
        if hd == 0 and c + 1 < nchunks:
            decay[c + 1] = chunk_decay(c + 1)
        pending.append((c, hd) + scores(c, hd, *decay[c]))
        ts = max(slot - GLA_TASK_START, 0)
        te = max(slot + 1 - GLA_TASK_START, 0)
        nts = nslots - GLA_TASK_START
        for task in between[ts * len(between) // nts:te * len(between) // nts]:
            task()
        if len(pending) > GLA_SKEW:
            outputs(*pending.pop(0))
    for item in pending:
        outputs(*item)


def _gla_kernel(xc_ref, xn_ref, g_ref, win_ref, wg2_ref, bg_ref, gn_ref, wout_ref,
                *rest, tm, kd, vd, tiles_per_seq, ncast):
    o_ref = rest[ncast]
    pa_s, la_a, pb_s, la_b, og_s, st_s = rest[2 * ncast + 1:]
    _run_casts(rest[:ncast], rest[ncast + 1:2 * ncast + 1])
    C = CHUNK
    step = pl.program_id(0)
    wts = (g_ref, win_ref, wg2_ref, bg_ref)

    @pl.when(step == 0)
    def _():
        for task in _gla_project(xc_ref, 0, tm, *wts, pa_s, la_a):
            task()

    @pl.when((2 * step) % tiles_per_seq == 0)
    def _():
        st_s[...] = jnp.zeros_like(st_s)

    ri = lax.broadcasted_iota(jnp.int32, (C, C), 0)
    ci = lax.broadcasted_iota(jnp.int32, (C, C), 1)
    tri = (ri >= ci).astype(BF16)
    tri3 = jnp.concatenate([tri, tri, tri, jnp.zeros_like(tri)], axis=1)
    halves = (32, 16, 8, 4, 2, 1)
    lvl_mask = [((ri // (2 * hf)) == (ci // (2 * hf))) & ((ri % (2 * hf)) >= hf) & ((ci % (2 * hf)) < hf)
                for hf in halves]
    diag = ri == ci
    rowc = lax.broadcasted_iota(jnp.int32, (C, kd // GLA_HEADS), 0)
    upper = [(rowc % (2 * hf)) >= hf for hf in halves]
    consts = (tri3, halves, lvl_mask, diag, upper)

    nxt = _gla_project(xc_ref, tm, tm, *wts, pb_s, la_b)
    _gla_chunks(pa_s, la_a, og_s, st_s, gn_ref, consts, nxt, tm=tm, kd=kd, vd=vd)
    o_ref[0:tm, :] = xc_ref[0:tm, :] + _dot(og_s[...], wout_ref[...])

    nxt = _gla_project(xn_ref, 0, tm, *wts, pa_s, la_a)
    _gla_chunks(pb_s, la_b, og_s, st_s, gn_ref, consts, nxt, tm=tm, kd=kd, vd=vd)
    o_ref[tm:2 * tm, :] = xc_ref[tm:2 * tm, :] + _dot(og_s[...], wout_ref[...])


def _gla(x2, g, win, wg2, bg, gn, wout, *, tm, seq_len, cast=()):
    m, d = x2.shape
    kd = wg2.shape[1]
    vd = wout.shape[0]
    dk = kd // GLA_HEADS
    dv = vd // GLA_HEADS
    nproj = win.shape[1] - wg2.shape[0]
    ntiles = m // tm
    assert seq_len % (2 * tm) == 0 and nproj % GLA_PROJ_PIECE[1] == 0 and tm % GLA_PROJ_PIECE[0] == 0
    c_in, c_out, c_shapes = _cast_through(cast, ntiles // 2)
    return pl.pallas_call(
        functools.partial(_gla_kernel, tm=tm, kd=kd, vd=vd, tiles_per_seq=seq_len // tm, ncast=len(cast)),
        grid=(ntiles // 2,),
        in_specs=[
            pl.BlockSpec((2 * tm, d), lambda i: (i, 0)),
            pl.BlockSpec((tm, d), lambda i: (jnp.minimum(2 * i + 2, ntiles - 1), 0)),
            _resident((1, d)),
            _resident(win.shape),
            _resident(wg2.shape),
            _resident((1, kd)),
            _resident(gn.shape),
            _resident((vd, d)),
        ] + c_in,
        out_specs=[pl.BlockSpec((2 * tm, d), lambda i: (i, 0))] + c_out,
        out_shape=[jax.ShapeDtypeStruct((m, d), F32)] + c_shapes,
        scratch_shapes=[
            pltpu.VMEM((tm, nproj), F32),
            pltpu.VMEM((tm, kd), F32),
            pltpu.VMEM((tm, nproj), F32),
            pltpu.VMEM((tm, kd), F32),
            pltpu.VMEM((tm, vd), BF16),
            pltpu.VMEM((GLA_HEADS, dk, dv), F32),
        ],
        compiler_params=pltpu.CompilerParams(
            dimension_semantics=("arbitrary",), vmem_limit_bytes=VMEM_LIMIT_BYTES),
        name="gla",
    )(x2, x2, g, win, wg2, bg, gn, wout, *[w for w, _ in cast])


def kernel(x, mix_norm, a_w_in, a_ln_g, a_ln_b, a_w_s, a_b_s, a_w_out, b_w_in, b_w_g2, b_b_g, b_gn_g, b_w_out, ffn_norm, ffn_w_gate, ffn_w_up, ffn_w_down, final_norm):
    b, t, d = x.shape
    depth = mix_norm.shape[0]
    assert depth == 2 and a_w_in.shape[0] == 1 and b_w_in.shape[0] == 1
    m = b * t
    bf = lambda w: w.astype(BF16)
    row = lambda v: v.reshape(1, -1)

    x2 = x.reshape(m, d)
    x2, wg0, wu0, wd0 = _sgu(
        x2, row(mix_norm[0]), bf(a_w_in[0]), row(a_ln_g[0]), row(a_ln_b[0]),
        a_w_s[0], a_b_s[0].T, bf(a_w_out[0]), tm=512,
        cast=((ffn_w_gate, 0), (ffn_w_up, 0), (ffn_w_down, 0)))
    x2, b_win, b_wout = _ffn(
        x2, row(ffn_norm[0]), wg0, wu0, wd0, row(final_norm), final_norm=False, tm=1024,
        cast=((b_w_in, 0), (b_w_out, 0)))
    x2, wg1, wu1, wd1 = _gla(
        x2, row(mix_norm[1]), b_win, b_w_g2[0], row(b_b_g[0]), b_gn_g[0], b_wout, tm=256, seq_len=t,
        cast=((ffn_w_gate, 1), (ffn_w_up, 1), (ffn_w_down, 1)))
    (x2,) = _ffn(x2, row(ffn_norm[1]), wg1, wu1, wd1, row(final_norm), final_norm=True, tm=1024)
    return x2.reshape(b, t, d)
```

```python
import functools

import jax
import jax.numpy as jnp
from jax import lax
from jax.experimental import pallas as pl
from jax.experimental.pallas import tpu as pltpu

F32 = jnp.float32
BF16 = jnp.bfloat16

EPS = 1e-6
CHUNK = 64
SG_BLOCK = 128
SG_GROUPS = 8
SG_GROUPS_PER_CHUNK = 2
GLA_HEADS = 4
GLA_TAU = 16.0
GLA_FAST_RANGE = 60.0
GLA_TASK_START = 0
GLA_SKEW = 1
GLA_PROJ_PIECE = (128, 512)

BF16_SUBLANES = 16
MXU_N = 256
FFN_HIDDEN_CHUNKS = 4

VMEM_LIMIT_BYTES = 56 * 1024 * 1024


def _rms(x, g):
    return x * lax.rsqrt(jnp.mean(x * x, axis=-1, keepdims=True) + EPS) * g


def _gelu(x):
    return 0.5 * x * (1.0 + lax.erf(x * (2.0 ** -0.5)))


def _dot(a, b):
    return jnp.dot(a, b, preferred_element_type=F32)


def _dot_nt(a, b):
    return lax.dot_general(a, b, (((1,), (1,)), ((), ())), preferred_element_type=F32)


def _dot_tn(a, b):
    return lax.dot_general(a, b, (((0,), (0,)), ((), ())), preferred_element_type=F32)


def _resident(shape):
    return pl.BlockSpec(shape, lambda *_: (0,) * len(shape))


def _cast_through(weights, nsteps):
    in_specs, out_specs, out_shapes = [], [], []
    for w, layer in weights:
        _, r, c = w.shape
        per = 1 if r % (nsteps * BF16_SUBLANES) == 0 else 2
        rb = r * per // nsteps
        assert rb % BF16_SUBLANES == 0 and rb * nsteps == r * per, (r, c, nsteps)
        in_specs.append(pl.BlockSpec((None, rb, c), lambda i, layer=layer, per=per: (layer, i // per, 0)))
        out_specs.append(pl.BlockSpec((rb, c), lambda i, per=per: (i // per, 0)))
        out_shapes.append(jax.ShapeDtypeStruct((r, c), BF16))
    return in_specs, out_specs, out_shapes


def _run_casts(src_refs, dst_refs):
    for src, dst in zip(src_refs, dst_refs):
        dst[...] = src[...].astype(BF16)


def _ffn_kernel(x_ref, g_ref, wg_ref, wu_ref, wd_ref, fin_ref, *rest, final_norm, ncast):
    o_ref = rest[ncast]
    _run_casts(rest[:ncast], rest[ncast + 1:])
    x = x_ref[...]
    h = _rms(x, g_ref[...]).astype(BF16)
    hid = wg_ref.shape[1]
    ntiles = hid // MXU_N
    bounds = [MXU_N * (ntiles * j // FFN_HIDDEN_CHUNKS) for j in range(FFN_HIDDEN_CHUNKS + 1)]
    y = x
    for c0, c1 in zip(bounds[:-1], bounds[1:]):
        gate = _dot(h, wg_ref[:, c0:c1])
        up = _dot(h, wu_ref[:, c0:c1])
        a = (jax.nn.silu(gate) * up).astype(BF16)
        y = y + _dot(a, wd_ref[c0:c1, :])
    if final_norm:
        y = _rms(y, fin_ref[...])
    o_ref[...] = y


def _ffn(x2, g, wg, wu, wd, fin, *, final_norm, tm, cast=()):
    m, d = x2.shape
    hid = wg.shape[1]
    c_in, c_out, c_shapes = _cast_through(cast, m // tm)
    return pl.pallas_call(
        functools.partial(_ffn_kernel, final_norm=final_norm, ncast=len(cast)),
        grid=(m // tm,),
        in_specs=[
            pl.BlockSpec((tm, d), lambda i: (i, 0)),
            _resident((1, d)),
            _resident((d, hid)),
            _resident((d, hid)),
            _resident((hid, d)),
            _resident((1, d)),
        ] + c_in,
        out_specs=[pl.BlockSpec((tm, d), lambda i: (i, 0))] + c_out,
        out_shape=[jax.ShapeDtypeStruct((m, d), F32)] + c_shapes,
        compiler_params=pltpu.CompilerParams(
            dimension_semantics=("arbitrary",), vmem_limit_bytes=VMEM_LIMIT_BYTES),
        name="swiglu",
    )(x2, g, wg, wu, wd, fin, *[w for w, _ in cast])


def _sgu_kernel(x_ref, g_ref, win_ref, lng_ref, lnb_ref, ws_ref, bst_ref, wout_ref,
                *rest, tm, ncast):
    o_ref = rest[ncast]
    v_s, y_s = rest[2 * ncast + 1:]
    _run_casts(rest[:ncast], rest[ncast + 1:2 * ncast + 1])
    width = v_s.shape[1]
    gc = width // SG_GROUPS
    lanes = 128
    x = x_ref[...]
    h = _rms(x, g_ref[...]).astype(BF16)

    gpc = SG_GROUPS_PER_CHUNK
    cw = gpc * gc
    p1 = jnp.zeros((tm, lanes), F32)
    p2 = jnp.zeros((tm, lanes), F32)
    for c in range(SG_GROUPS // gpc):
        v = _gelu(_dot(h, win_ref[:, width + c * cw:width + (c + 1) * cw]))
        v_s[:, c * cw:(c + 1) * cw] = v
        for l in range(cw // lanes):
            vl = v[:, l * lanes:(l + 1) * lanes]
            p1 = p1 + vl
            p2 = p2 + vl * vl
    mu = jnp.sum(p1, axis=-1, keepdims=True) * (1.0 / width)
    var = jnp.sum(p2, axis=-1, keepdims=True) * (1.0 / width) - mu * mu
    rstd = lax.rsqrt(var + EPS)
    shift = -mu * rstd

    ti = lax.broadcasted_iota(jnp.int32, (SG_BLOCK, SG_BLOCK), 0) // CHUNK
    si = lax.broadcasted_iota(jnp.int32, (SG_BLOCK, SG_BLOCK), 1) // CHUNK
    causal = ti >= si
    for c in range(SG_GROUPS // gpc):
        ccols = slice(c * cw, (c + 1) * cw)
        u = _gelu(_dot(h, win_ref[:, ccols]))
        vn = ((v_s[:, ccols] * rstd + shift) * lng_ref[:, ccols] + lnb_ref[:, ccols]).astype(BF16)
        for gg in range(gpc):
            g = c * gpc + gg
            gcols = slice(gg * gc, (gg + 1) * gc)
            ws = jnp.where(causal, ws_ref[g], 0.0).astype(BF16)
            bias = bst_ref[:, g:g + 1]
            for w in range(tm // SG_BLOCK):
                rows = slice(w * SG_BLOCK, (w + 1) * SG_BLOCK)
                s = _dot(ws, vn[rows, gcols]) + bias
                y_s[rows, g * gc:(g + 1) * gc] = (u[rows, gcols] * s).astype(BF16)
    o_ref[...] = x + _dot(y_s[...], wout_ref[...])


def _sgu(x2, g, win, lng, lnb, ws, bst, wout, *, tm, cast=()):
    m, d = x2.shape
    width = wout.shape[0]
    c_in, c_out, c_shapes = _cast_through(cast, m // tm)
    return pl.pallas_call(
        functools.partial(_sgu_kernel, tm=tm, ncast=len(cast)),
        grid=(m // tm,),
        in_specs=[
            pl.BlockSpec((tm, d), lambda i: (i, 0)),
            _resident((1, d)),
            _resident((d, 2 * width)),
            _resident((1, width)),
            _resident((1, width)),
            _resident(ws.shape),
            _resident(bst.shape),
            _resident((width, d)),
        ] + c_in,
        out_specs=[pl.BlockSpec((tm, d), lambda i: (i, 0))] + c_out,
        out_shape=[jax.ShapeDtypeStruct((m, d), F32)] + c_shapes,
        scratch_shapes=[
            pltpu.VMEM((tm, width), F32),
            pltpu.VMEM((tm, width), BF16),
        ],
        compiler_params=pltpu.CompilerParams(
            dimension_semantics=("arbitrary",), vmem_limit_bytes=VMEM_LIMIT_BYTES),
        name="sgu",
    )(x2, g, win, lng, lnb, ws, bst, wout, *[w for w, _ in cast])


def _ref_rows(G, half):
    n, w = G.shape
    blk = 2 * half
    if blk >= 8:
        return jnp.concatenate(
            [jnp.broadcast_to(G[b * blk + half - 1:b * blk + half, :], (blk, w))
             for b in range(n // blk)], axis=0)
    assert blk == 4
    sub = lax.broadcasted_iota(jnp.int32, (8, w), 0)
    tiles = []
    for t in range(n // 8):
        lo = jnp.broadcast_to(G[8 * t + 1:8 * t + 2, :], (8, w))
        hi = jnp.broadcast_to(G[8 * t + 5:8 * t + 6, :], (8, w))
        tiles.append(jnp.where(sub < 4, lo, hi))
    return jnp.concatenate(tiles, axis=0)


def _gla_project(x_ref, row0, tm, g_ref, win_ref, wg2_ref, bg_ref, proj_ref, la_ref, flag_ref, flag_idx):
    state = {}
    ncol = proj_ref.shape[1]
    pr, pc = GLA_PROJ_PIECE
    tasks = []

    def norm(r):
        def run():
            x = x_ref[row0 + r * pr:row0 + (r + 1) * pr, :]
            state[r] = _rms(x, g_ref[...]).astype(BF16)
            if r == 0:
                flag_ref[flag_idx] = jnp.float32(0.0)
        return run

    def cols(r, p):
        def run():
            proj_ref[r * pr:(r + 1) * pr, p * pc:(p + 1) * pc] = _dot(state[r], win_ref[:, p * pc:(p + 1) * pc])
        return run

    def gate(r):
        def run():
            gz = _dot(state[r], win_ref[:, ncol:])
            z = _dot(gz.astype(BF16), wg2_ref[...].astype(BF16)) + bg_ref[...]
            la = (jnp.minimum(z, 0.0) - jnp.log1p(jnp.exp(-jnp.abs(z)))) / GLA_TAU
            la_ref[r * pr:(r + 1) * pr, :] = la
            tot = [jnp.sum(la[c * CHUNK:(c + 1) * CHUNK, :], axis=0, keepdims=True) for c in range(pr // CHUNK)]
            flag_ref[flag_idx] = jnp.minimum(flag_ref[flag_idx], jnp.min(functools.reduce(jnp.minimum, tot)))
        return run

    for r in range(tm // pr):
        tasks += [norm(r), gate(r)] + [cols(r, p) for p in range(ncol // pc)]
    return tasks


def _gla_chunks(proj_ref, la_ref, og_s, st_s, gn_ref, consts, between, *, tm, kd, vd, fast):
    dk = kd // GLA_HEADS
    dv = vd // GLA_HEADS
    C = CHUNK
    tri3, halves, lvl_mask, diag, upper, causal = consts
    scale = dk ** -0.5
    between = list(between)
    nchunks = tm // C
    nslots = nchunks * GLA_HEADS

    def chunk_decay(c):
        la_all = la_ref[c * C:(c + 1) * C, :]
        la_hi = la_all.astype(BF16)
        r1 = la_all - la_hi.astype(F32)
        la_mid = r1.astype(BF16)
        la_lo = (r1 - la_mid.astype(F32)).astype(BF16)
        return la_all, _dot(tri3, jnp.concatenate([la_hi, la_mid, la_lo, jnp.zeros_like(la_hi)], axis=0))

    def scores(c, hd, la_all, G_all):
        rows = slice(c * C, (c + 1) * C)
        ks_ = slice(hd * dk, (hd + 1) * dk)
        q = proj_ref[rows, hd * dk:(hd + 1) * dk] * scale
        k = proj_ref[rows, kd + hd * dk:kd + (hd + 1) * dk]
        la = la_all[:, ks_]
        G = G_all[:, ks_]
        g_last = G[C - 1:C, :]
        qd = (q * jnp.exp(G)).astype(BF16)
        kdec = (k * jnp.exp(g_last - G)).astype(BF16)
        dec = jnp.exp(g_last)
        if fast:
            return [_dot_nt(qd, (k * jnp.exp(-G)).astype(BF16))], qd, kdec, dec
        ps = [jnp.sum(q * k, axis=-1, keepdims=True)]
        for lvl, hf in enumerate(halves):
            if hf == 1:
                e = jnp.exp(jnp.where(upper[lvl], la, 0.0))
            else:
                e = jnp.exp(-jnp.abs(G - _ref_rows(G, hf)))
            xl = (jnp.where(upper[lvl], q, k) * e).astype(BF16)
            ps.append(_dot_nt(xl, xl))
        return ps, qd, kdec, dec

    def outputs(c, hd, ps, qd, kdec, dec):
        rows = slice(c * C, (c + 1) * C)
        vs_ = slice(2 * kd + hd * dv, 2 * kd + (hd + 1) * dv)
        rs_ = slice(2 * kd + vd + hd * dv, 2 * kd + vd + (hd + 1) * dv)
        if fast:
            a = jnp.where(causal, ps[0], 0.0)
        else:
            a = jnp.where(diag, ps[0], 0.0)
            for lvl in range(len(halves)):
                a = jnp.where(lvl_mask[lvl], ps[lvl + 1], a)
        v = proj_ref[rows, vs_].astype(BF16)
        st = st_s[hd]
        pad = jnp.zeros((C, MXU_N - dk - C), BF16)
        lhs = jnp.concatenate([qd, a.astype(BF16), pad], axis=1)
        rhs = jnp.concatenate([st.astype(BF16), v, jnp.zeros((MXU_N - dk - C, dv), BF16)], axis=0)
        o = _dot(lhs, rhs)
        dcol = jnp.transpose(jnp.broadcast_to(dec, (dk, dk)))
        st_s[hd] = st * jnp.concatenate([dcol] * (dv // dk), axis=1) + _dot_tn(kdec, v)
        o = o * lax.rsqrt(jnp.mean(o * o, axis=-1, keepdims=True) + EPS) * gn_ref[hd:hd + 1, :]
        r = proj_ref[rows, rs_]
        og_s[rows, hd * dv:(hd + 1) * dv] = (jax.nn.silu(r) * o).astype(BF16)

    decay = {0: chunk_decay(0)}
    pending = []
    for slot in range(nslots):
        c, hd = divmod(slot, GLA_HEADS)
        if hd == 0 and c + 1 < nchunks:
            decay[c + 1] = chunk_decay(c + 1)
        pending.append((c, hd) + scores(c, hd, *decay[c]))
        ts = max(slot - GLA_TASK_START, 0)
        te = max(slot + 1 - GLA_TASK_START, 0)
        nts = nslots - GLA_TASK_START
        for task in between[ts * len(between) // nts:te * len(between) // nts]:
            task()
        if len(pending) > GLA_SKEW:
            outputs(*pending.pop(0))
    for item in pending:
        outputs(*item)


def _gla_kernel(xc_ref, xn_ref, g_ref, win_ref, wg2_ref, bg_ref, gn_ref, wout_ref,
                *rest, tm, kd, vd, tiles_per_seq, ncast):
    o_ref = rest[ncast]
    pa_s, la_a, pb_s, la_b, og_s, st_s, flag_s = rest[2 * ncast + 1:]
    _run_casts(rest[:ncast], rest[ncast + 1:2 * ncast + 1])
    C = CHUNK
    step = pl.program_id(0)
    wts = (g_ref, win_ref, wg2_ref, bg_ref)

    @pl.when(step == 0)
    def _():
        for task in _gla_project(xc_ref, 0, tm, *wts, pa_s, la_a, flag_s, 0):
            task()

    @pl.when((2 * step) % tiles_per_seq == 0)
    def _():
        st_s[...] = jnp.zeros_like(st_s)

    ri = lax.broadcasted_iota(jnp.int32, (C, C), 0)
    ci = lax.broadcasted_iota(jnp.int32, (C, C), 1)
    tri = (ri >= ci).astype(BF16)
    tri3 = jnp.concatenate([tri, tri, tri, jnp.zeros_like(tri)], axis=1)
    halves = (32, 16, 8, 4, 2, 1)
    lvl_mask = [((ri // (2 * hf)) == (ci // (2 * hf))) & ((ri % (2 * hf)) >= hf) & ((ci % (2 * hf)) < hf)
                for hf in halves]
    diag = ri == ci
    rowc = lax.broadcasted_iota(jnp.int32, (C, kd // GLA_HEADS), 0)
    upper = [(rowc % (2 * hf)) >= hf for hf in halves]
    consts = (tri3, halves, lvl_mask, diag, upper, ri >= ci)

    def tile(proj_ref, la_ref, flag_idx, nxt_x, nxt_row0, nxt_proj, nxt_la, nxt_flag):
        def run(fast):
            def body():
                nxt = _gla_project(nxt_x, nxt_row0, tm, *wts, nxt_proj, nxt_la, flag_s, nxt_flag)
                _gla_chunks(proj_ref, la_ref, og_s, st_s, gn_ref, consts, nxt, tm=tm, kd=kd, vd=vd, fast=fast)
            return body
        mild = flag_s[flag_idx] > -GLA_FAST_RANGE
        pl.when(mild)(run(True))
        pl.when(jnp.logical_not(mild))(run(False))

    tile(pa_s, la_a, 0, xc_ref, tm, pb_s, la_b, 1)
    o_ref[0:tm, :] = xc_ref[0:tm, :] + _dot(og_s[...], wout_ref[...])
    tile(pb_s, la_b, 1, xn_ref, 0, pa_s, la_a, 0)
    o_ref[tm:2 * tm, :] = xc_ref[tm:2 * tm, :] + _dot(og_s[...], wout_ref[...])


def _gla(x2, g, win, wg2, bg, gn, wout, *, tm, seq_len, cast=()):
    m, d = x2.shape
    kd = wg2.shape[1]
    vd = wout.shape[0]
    dk = kd // GLA_HEADS
    dv = vd // GLA_HEADS
    nproj = win.shape[1] - wg2.shape[0]
    ntiles = m // tm
    assert seq_len % (2 * tm) == 0 and nproj % GLA_PROJ_PIECE[1] == 0 and tm % GLA_PROJ_PIECE[0] == 0
    c_in, c_out, c_shapes = _cast_through(cast, ntiles // 2)
    return pl.pallas_call(
        functools.partial(_gla_kernel, tm=tm, kd=kd, vd=vd, tiles_per_seq=seq_len // tm, ncast=len(cast)),
        grid=(ntiles // 2,),
        in_specs=[
            pl.BlockSpec((2 * tm, d), lambda i: (i, 0)),
            pl.BlockSpec((tm, d), lambda i: (jnp.minimum(2 * i + 2, ntiles - 1), 0)),
            _resident((1, d)),
            _resident(win.shape),
            _resident(wg2.shape),
            _resident((1, kd)),
            _resident(gn.shape),
            _resident((vd, d)),
        ] + c_in,
        out_specs=[pl.BlockSpec((2 * tm, d), lambda i: (i, 0))] + c_out,
        out_shape=[jax.ShapeDtypeStruct((m, d), F32)] + c_shapes,
        scratch_shapes=[
            pltpu.VMEM((tm, nproj), F32),
            pltpu.VMEM((tm, kd), F32),
            pltpu.VMEM((tm, nproj), F32),
            pltpu.VMEM((tm, kd), F32),
            pltpu.VMEM((tm, vd), BF16),
            pltpu.VMEM((GLA_HEADS, dk, dv), F32),
            pltpu.SMEM((2,), F32),
        ],
        compiler_params=pltpu.CompilerParams(
            dimension_semantics=("arbitrary",), vmem_limit_bytes=VMEM_LIMIT_BYTES),
        name="gla",
    )(x2, x2, g, win, wg2, bg, gn, wout, *[w for w, _ in cast])


def kernel(x, mix_norm, a_w_in, a_ln_g, a_ln_b, a_w_s, a_b_s, a_w_out, b_w_in, b_w_g2, b_b_g, b_gn_g, b_w_out, ffn_norm, ffn_w_gate, ffn_w_up, ffn_w_down, final_norm):
    b, t, d = x.shape
    depth = mix_norm.shape[0]
    assert depth == 2 and a_w_in.shape[0] == 1 and b_w_in.shape[0] == 1
    m = b * t
    bf = lambda w: w.astype(BF16)
    row = lambda v: v.reshape(1, -1)

    x2 = x.reshape(m, d)
    x2, wg0, wu0, wd0 = _sgu(
        x2, row(mix_norm[0]), bf(a_w_in[0]), row(a_ln_g[0]), row(a_ln_b[0]),
        a_w_s[0], a_b_s[0].T, bf(a_w_out[0]), tm=512,
        cast=((ffn_w_gate, 0), (ffn_w_up, 0), (ffn_w_down, 0)))
    x2, b_win, b_wout = _ffn(
        x2, row(ffn_norm[0]), wg0, wu0, wd0, row(final_norm), final_norm=False, tm=1024,
        cast=((b_w_in, 0), (b_w_out, 0)))
    x2, wg1, wu1, wd1 = _gla(
        x2, row(mix_norm[1]), b_win, b_w_g2[0], row(b_b_g[0]), b_gn_g[0], b_wout, tm=256, seq_len=t,
        cast=((ffn_w_gate, 1), (ffn_w_up, 1), (ffn_w_down, 1)))
    (x2,) = _ffn(x2, row(ffn_norm[1]), wg1, wu1, wd1, row(final_norm), final_norm=True, tm=1024)
    return x2.reshape(b, t, d)
```

```python
import functools

import jax
import jax.numpy as jnp
from jax import lax
from jax.experimental import pallas as pl
from jax.experimental.pallas import tpu as pltpu

F32 = jnp.float32
BF16 = jnp.bfloat16

EPS = 1e-6
CHUNK = 64
SG_BLOCK = 128
SG_GROUPS = 8
SG_GROUPS_PER_CHUNK = 2
GLA_HEADS = 4
GLA_TAU = 16.0
GLA_FAST_RANGE = 60.0
GLA_TASK_START = 0
GLA_SKEW = 1
GLA_PROJ_PIECE = (256, 512)

BF16_SUBLANES = 16
MXU_N = 256
FFN_HIDDEN_CHUNKS = 4

VMEM_LIMIT_BYTES = 56 * 1024 * 1024


def _rms(x, g):
    return x * lax.rsqrt(jnp.mean(x * x, axis=-1, keepdims=True) + EPS) * g


def _gelu(x):
    return 0.5 * x * (1.0 + lax.erf(x * (2.0 ** -0.5)))


def _dot(a, b):
    return jnp.dot(a, b, preferred_element_type=F32)


def _dot_nt(a, b):
    return lax.dot_general(a, b, (((1,), (1,)), ((), ())), preferred_element_type=F32)


def _dot_tn(a, b):
    return lax.dot_general(a, b, (((0,), (0,)), ((), ())), preferred_element_type=F32)


def _resident(shape):
    return pl.BlockSpec(shape, lambda *_: (0,) * len(shape))


def _cast_through(weights, nsteps):
    in_specs, out_specs, out_shapes = [], [], []
    for w, layer in weights:
        _, r, c = w.shape
        per = 1 if r % (nsteps * BF16_SUBLANES) == 0 else 2
        rb = r * per // nsteps
        assert rb % BF16_SUBLANES == 0 and rb * nsteps == r * per, (r, c, nsteps)
        in_specs.append(pl.BlockSpec((None, rb, c), lambda i, layer=layer, per=per: (layer, i // per, 0)))
        out_specs.append(pl.BlockSpec((rb, c), lambda i, per=per: (i // per, 0)))
        out_shapes.append(jax.ShapeDtypeStruct((r, c), BF16))
    return in_specs, out_specs, out_shapes


def _run_casts(src_refs, dst_refs):
    for src, dst in zip(src_refs, dst_refs):
        dst[...] = src[...].astype(BF16)


def _ffn_kernel(x_ref, g_ref, wg_ref, wu_ref, wd_ref, fin_ref, *rest, final_norm, ncast):
    o_ref = rest[ncast]
    _run_casts(rest[:ncast], rest[ncast + 1:])
    x = x_ref[...]
    h = _rms(x, g_ref[...]).astype(BF16)
    hid = wg_ref.shape[1]
    ntiles = hid // MXU_N
    bounds = [MXU_N * (ntiles * j // FFN_HIDDEN_CHUNKS) for j in range(FFN_HIDDEN_CHUNKS + 1)]
    y = x
    for c0, c1 in zip(bounds[:-1], bounds[1:]):
        gate = _dot(h, wg_ref[:, c0:c1])
        up = _dot(h, wu_ref[:, c0:c1])
        a = (jax.nn.silu(gate) * up).astype(BF16)
        y = y + _dot(a, wd_ref[c0:c1, :])
    if final_norm:
        y = _rms(y, fin_ref[...])
    o_ref[...] = y


def _ffn(x2, g, wg, wu, wd, fin, *, final_norm, tm, cast=()):
    m, d = x2.shape
    hid = wg.shape[1]
    c_in, c_out, c_shapes = _cast_through(cast, m // tm)
    return pl.pallas_call(
        functools.partial(_ffn_kernel, final_norm=final_norm, ncast=len(cast)),
        grid=(m // tm,),
        in_specs=[
            pl.BlockSpec((tm, d), lambda i: (i, 0)),
            _resident((1, d)),
            _resident((d, hid)),
            _resident((d, hid)),
            _resident((hid, d)),
            _resident((1, d)),
        ] + c_in,
        out_specs=[pl.BlockSpec((tm, d), lambda i: (i, 0))] + c_out,
        out_shape=[jax.ShapeDtypeStruct((m, d), F32)] + c_shapes,
        compiler_params=pltpu.CompilerParams(
            dimension_semantics=("arbitrary",), vmem_limit_bytes=VMEM_LIMIT_BYTES),
        name="swiglu",
    )(x2, g, wg, wu, wd, fin, *[w for w, _ in cast])


def _sgu_kernel(x_ref, g_ref, win_ref, lng_ref, lnb_ref, ws_ref, bst_ref, wout_ref,
                *rest, tm, ncast):
    o_ref = rest[ncast]
    v_s, y_s = rest[2 * ncast + 1:]
    _run_casts(rest[:ncast], rest[ncast + 1:2 * ncast + 1])
    width = v_s.shape[1]
    gc = width // SG_GROUPS
    lanes = 128
    x = x_ref[...]
    h = _rms(x, g_ref[...]).astype(BF16)

    gpc = SG_GROUPS_PER_CHUNK
    cw = gpc * gc
    p1 = jnp.zeros((tm, lanes), F32)
    p2 = jnp.zeros((tm, lanes), F32)
    for c in range(SG_GROUPS // gpc):
        v = _gelu(_dot(h, win_ref[:, width + c * cw:width + (c + 1) * cw]))
        v_s[:, c * cw:(c + 1) * cw] = v
        for l in range(cw // lanes):
            vl = v[:, l * lanes:(l + 1) * lanes]
            p1 = p1 + vl
            p2 = p2 + vl * vl
    mu = jnp.sum(p1, axis=-1, keepdims=True) * (1.0 / width)
    var = jnp.sum(p2, axis=-1, keepdims=True) * (1.0 / width) - mu * mu
    rstd = lax.rsqrt(var + EPS)
    shift = -mu * rstd

    ti = lax.broadcasted_iota(jnp.int32, (SG_BLOCK, SG_BLOCK), 0) // CHUNK
    si = lax.broadcasted_iota(jnp.int32, (SG_BLOCK, SG_BLOCK), 1) // CHUNK
    causal = ti >= si
    for c in range(SG_GROUPS // gpc):
        ccols = slice(c * cw, (c + 1) * cw)
        u = _gelu(_dot(h, win_ref[:, ccols]))
        vn = ((v_s[:, ccols] * rstd + shift) * lng_ref[:, ccols] + lnb_ref[:, ccols]).astype(BF16)
        for gg in range(gpc):
            g = c * gpc + gg
            gcols = slice(gg * gc, (gg + 1) * gc)
            ws = jnp.where(causal, ws_ref[g], 0.0).astype(BF16)
            bias = bst_ref[:, g:g + 1]
            for w in range(tm // SG_BLOCK):
                rows = slice(w * SG_BLOCK, (w + 1) * SG_BLOCK)
                s = _dot(ws, vn[rows, gcols]) + bias
                y_s[rows, g * gc:(g + 1) * gc] = (u[rows, gcols] * s).astype(BF16)
    o_ref[...] = x + _dot(y_s[...], wout_ref[...])


def _sgu(x2, g, win, lng, lnb, ws, bst, wout, *, tm, cast=()):
    m, d = x2.shape
    width = wout.shape[0]
    c_in, c_out, c_shapes = _cast_through(cast, m // tm)
    return pl.pallas_call(
        functools.partial(_sgu_kernel, tm=tm, ncast=len(cast)),
        grid=(m // tm,),
        in_specs=[
            pl.BlockSpec((tm, d), lambda i: (i, 0)),
            _resident((1, d)),
            _resident((d, 2 * width)),
            _resident((1, width)),
            _resident((1, width)),
            _resident(ws.shape),
            _resident(bst.shape),
            _resident((width, d)),
        ] + c_in,
        out_specs=[pl.BlockSpec((tm, d), lambda i: (i, 0))] + c_out,
        out_shape=[jax.ShapeDtypeStruct((m, d), F32)] + c_shapes,
        scratch_shapes=[
            pltpu.VMEM((tm, width), F32),
            pltpu.VMEM((tm, width), BF16),
        ],
        compiler_params=pltpu.CompilerParams(
            dimension_semantics=("arbitrary",), vmem_limit_bytes=VMEM_LIMIT_BYTES),
        name="sgu",
    )(x2, g, win, lng, lnb, ws, bst, wout, *[w for w, _ in cast])


def _ref_rows(G, half):
    n, w = G.shape
    blk = 2 * half
    if blk >= 8:
        return jnp.concatenate(
            [jnp.broadcast_to(G[b * blk + half - 1:b * blk + half, :], (blk, w))
             for b in range(n // blk)], axis=0)
    assert blk == 4
    sub = lax.broadcasted_iota(jnp.int32, (8, w), 0)
    tiles = []
    for t in range(n // 8):
        lo = jnp.broadcast_to(G[8 * t + 1:8 * t + 2, :], (8, w))
        hi = jnp.broadcast_to(G[8 * t + 5:8 * t + 6, :], (8, w))
        tiles.append(jnp.where(sub < 4, lo, hi))
    return jnp.concatenate(tiles, axis=0)


def _gla_project(x_ref, row0, tm, g_ref, win_ref, wg2_ref, bg_ref, proj_ref, la_ref, h_ref, flag_ref, flag_idx):
    state = {}
    ncol = proj_ref.shape[1]
    pr, pc = GLA_PROJ_PIECE
    last = tm // pr - 1 if tm > pr else -1
    tasks = []

    def norm(r):
        def run():
            x = x_ref[row0 + r * pr:row0 + (r + 1) * pr, :]
            state[r] = _rms(x, g_ref[...]).astype(BF16)
            if r == 0:
                flag_ref[flag_idx] = jnp.float32(0.0)
            if r == last:
                h_ref[...] = state[r]
        run.vpu = True
        return run

    def cols(r, p):
        def run():
            h = h_ref[...] if r == last else state[r]
            proj_ref[r * pr:(r + 1) * pr, p * pc:(p + 1) * pc] = _dot(h, win_ref[:, p * pc:(p + 1) * pc])
        run.vpu = False
        return run

    def gate(r):
        def run():
            gz = _dot(state[r], win_ref[:, ncol:])
            z = _dot(gz.astype(BF16), wg2_ref[...].astype(BF16)) + bg_ref[...]
            la = (jnp.minimum(z, 0.0) - jnp.log1p(jnp.exp(-jnp.abs(z)))) / GLA_TAU
            la_ref[r * pr:(r + 1) * pr, :] = la
            tot = [jnp.sum(la[c * CHUNK:(c + 1) * CHUNK, :], axis=0, keepdims=True) for c in range(pr // CHUNK)]
            flag_ref[flag_idx] = jnp.minimum(flag_ref[flag_idx], jnp.min(functools.reduce(jnp.minimum, tot)))
        run.vpu = True
        return run

    for r in range(tm // pr):
        tasks += [norm(r), gate(r)] + ([] if r == last else [cols(r, p) for p in range(ncol // pc)])
    return tasks, ([cols(last, p) for p in range(ncol // pc)] if last >= 0 else [])


def _gla_chunks(proj_ref, la_ref, og_s, st_s, gn_ref, consts, between, *, tm, kd, vd, fast):
    dk = kd // GLA_HEADS
    dv = vd // GLA_HEADS
    C = CHUNK
    tri3, halves, lvl_mask, diag, upper, causal = consts
    scale = dk ** -0.5
    between = list(between)
    nchunks = tm // C
    nslots = nchunks * GLA_HEADS

    def chunk_decay(c):
        la_all = la_ref[c * C:(c + 1) * C, :]
        la_hi = la_all.astype(BF16)
        r1 = la_all - la_hi.astype(F32)
        la_mid = r1.astype(BF16)
        la_lo = (r1 - la_mid.astype(F32)).astype(BF16)
        return la_all, _dot(tri3, jnp.concatenate([la_hi, la_mid, la_lo, jnp.zeros_like(la_hi)], axis=0))

    def scores(c, hd, la_all, G_all):
        rows = slice(c * C, (c + 1) * C)
        ks_ = slice(hd * dk, (hd + 1) * dk)
        q = proj_ref[rows, hd * dk:(hd + 1) * dk] * scale
        k = proj_ref[rows, kd + hd * dk:kd + (hd + 1) * dk]
        la = la_all[:, ks_]
        G = G_all[:, ks_]
        g_last = G[C - 1:C, :]
        qd = (q * jnp.exp(G)).astype(BF16)
        kdec = (k * jnp.exp(g_last - G)).astype(BF16)
        dec = jnp.exp(g_last)
        if fast:
            return [_dot_nt(qd, (k * jnp.exp(-G)).astype(BF16))], qd, kdec, dec
        ps = [jnp.sum(q * k, axis=-1, keepdims=True)]
        for lvl, hf in enumerate(halves):
            if hf == 1:
                e = jnp.exp(jnp.where(upper[lvl], la, 0.0))
            else:
                e = jnp.exp(-jnp.abs(G - _ref_rows(G, hf)))
            xl = (jnp.where(upper[lvl], q, k) * e).astype(BF16)
            ps.append(_dot_nt(xl, xl))
        return ps, qd, kdec, dec

    def outputs(c, hd, ps, qd, kdec, dec):
        rows = slice(c * C, (c + 1) * C)
        vs_ = slice(2 * kd + hd * dv, 2 * kd + (hd + 1) * dv)
        rs_ = slice(2 * kd + vd + hd * dv, 2 * kd + vd + (hd + 1) * dv)
        if fast:
            a = jnp.where(causal, ps[0], 0.0)
        else:
            a = jnp.where(diag, ps[0], 0.0)
            for lvl in range(len(halves)):
                a = jnp.where(lvl_mask[lvl], ps[lvl + 1], a)
        v = proj_ref[rows, vs_].astype(BF16)
        st = st_s[hd]
        pad = jnp.zeros((C, MXU_N - dk - C), BF16)
        lhs = jnp.concatenate([qd, a.astype(BF16), pad], axis=1)
        rhs = jnp.concatenate([st.astype(BF16), v, jnp.zeros((MXU_N - dk - C, dv), BF16)], axis=0)
        o = _dot(lhs, rhs)
        dcol = jnp.transpose(jnp.broadcast_to(dec, (dk, dk)))
        st_s[hd] = st * jnp.concatenate([dcol] * (dv // dk), axis=1) + _dot_tn(kdec, v)
        o = o * lax.rsqrt(jnp.mean(o * o, axis=-1, keepdims=True) + EPS) * gn_ref[hd:hd + 1, :]
        r = proj_ref[rows, rs_]
        og_s[rows, hd * dv:(hd + 1) * dv] = (jax.nn.silu(r) * o).astype(BF16)

    decay = {0: chunk_decay(0)}
    pending = []
    for slot in range(nslots):
        c, hd = divmod(slot, GLA_HEADS)
        if hd == 0 and c + 1 < nchunks:
            decay[c + 1] = chunk_decay(c + 1)
        pending.append((c, hd) + scores(c, hd, *decay[c]))
        ts = max(slot - GLA_TASK_START, 0)
        te = max(slot + 1 - GLA_TASK_START, 0)
        nts = nslots - GLA_TASK_START
        for task in between[ts * len(between) // nts:te * len(between) // nts]:
            task()
        if len(pending) > GLA_SKEW:
            outputs(*pending.pop(0))
    for item in pending:
        outputs(*item)


def _gla_kernel(xc_ref, xn_ref, g_ref, win_ref, wg2_ref, bg_ref, gn_ref, wout_ref,
                *rest, tm, kd, vd, tiles_per_seq, ncast):
    o_ref = rest[ncast]
    pa_s, la_a, ha_s, pb_s, la_b, hb_s, og_s, st_s, flag_s = rest[2 * ncast + 1:]
    _run_casts(rest[:ncast], rest[ncast + 1:2 * ncast + 1])
    C = CHUNK
    step = pl.program_id(0)
    wts = (g_ref, win_ref, wg2_ref, bg_ref)

    @pl.when(step == 0)
    def _():
        for task in _gla_project(xc_ref, 0, tm, *wts, pa_s, la_a, ha_s, flag_s, 0)[0]:
            task()

    @pl.when((2 * step) % tiles_per_seq == 0)
    def _():
        st_s[...] = jnp.zeros_like(st_s)

    ri = lax.broadcasted_iota(jnp.int32, (C, C), 0)
    ci = lax.broadcasted_iota(jnp.int32, (C, C), 1)
    tri = (ri >= ci).astype(BF16)
    tri3 = jnp.concatenate([tri, tri, tri, jnp.zeros_like(tri)], axis=1)
    halves = (32, 16, 8, 4, 2, 1)
    lvl_mask = [((ri // (2 * hf)) == (ci // (2 * hf))) & ((ri % (2 * hf)) >= hf) & ((ci % (2 * hf)) < hf)
                for hf in halves]
    diag = ri == ci
    rowc = lax.broadcasted_iota(jnp.int32, (C, kd // GLA_HEADS), 0)
    upper = [(rowc % (2 * hf)) >= hf for hf in halves]
    consts = (tri3, halves, lvl_mask, diag, upper, ri >= ci)

    def tile(proj_ref, la_ref, h_ref, flag_idx, nxt_x, nxt_row0, nxt_proj, nxt_la, nxt_h, nxt_flag):
        def run(fast):
            def body():
                own_late = _gla_project(None, 0, tm, *wts, proj_ref, la_ref, h_ref, flag_s, flag_idx)[1]
                nxt_early = _gla_project(nxt_x, nxt_row0, tm, *wts, nxt_proj, nxt_la, nxt_h, flag_s, nxt_flag)[0]
                vpu = [t for t in nxt_early if t.vpu]
                mxu = own_late + [t for t in nxt_early if not t.vpu]
                tasks, gap = [], len(mxu) // len(vpu)
                for i, t in enumerate(vpu):
                    tasks += mxu[i * gap:i * gap + 1] + [t] + mxu[i * gap + 1:(i + 1) * gap]
                tasks += mxu[len(vpu) * gap:]
                if not own_late:
                    tasks = nxt_early
                _gla_chunks(proj_ref, la_ref, og_s, st_s, gn_ref, consts, tasks, tm=tm, kd=kd, vd=vd, fast=fast)
            return body
        mild = flag_s[flag_idx] > -GLA_FAST_RANGE
        pl.when(mild)(run(True))
        pl.when(jnp.logical_not(mild))(run(False))

    tile(pa_s, la_a, ha_s, 0, xc_ref, tm, pb_s, la_b, hb_s, 1)
    o_ref[0:tm, :] = xc_ref[0:tm, :] + _dot(og_s[...], wout_ref[...])
    tile(pb_s, la_b, hb_s, 1, xn_ref, 0, pa_s, la_a, ha_s, 0)
    o_ref[tm:2 * tm, :] = xc_ref[tm:2 * tm, :] + _dot(og_s[...], wout_ref[...])


def _gla(x2, g, win, wg2, bg, gn, wout, *, tm, seq_len, cast=()):
    m, d = x2.shape
    kd = wg2.shape[1]
    vd = wout.shape[0]
    dk = kd // GLA_HEADS
    dv = vd // GLA_HEADS
    nproj = win.shape[1] - wg2.shape[0]
    ntiles = m // tm
    assert seq_len % (2 * tm) == 0 and nproj % GLA_PROJ_PIECE[1] == 0 and tm % GLA_PROJ_PIECE[0] == 0
    c_in, c_out, c_shapes = _cast_through(cast, ntiles // 2)
    return pl.pallas_call(
        functools.partial(_gla_kernel, tm=tm, kd=kd, vd=vd, tiles_per_seq=seq_len // tm, ncast=len(cast)),
        grid=(ntiles // 2,),
        in_specs=[
            pl.BlockSpec((2 * tm, d), lambda i: (i, 0)),
            pl.BlockSpec((tm, d), lambda i: (jnp.minimum(2 * i + 2, ntiles - 1), 0)),
            _resident((1, d)),
            _resident(win.shape),
            _resident(wg2.shape),
            _resident((1, kd)),
            _resident(gn.shape),
            _resident((vd, d)),
        ] + c_in,
        out_specs=[pl.BlockSpec((2 * tm, d), lambda i: (i, 0))] + c_out,
        out_shape=[jax.ShapeDtypeStruct((m, d), F32)] + c_shapes,
        scratch_shapes=[
            pltpu.VMEM((tm, nproj), F32),
            pltpu.VMEM((tm, kd), F32),
            pltpu.VMEM((GLA_PROJ_PIECE[0], d), BF16),
            pltpu.VMEM((tm, nproj), F32),
            pltpu.VMEM((tm, kd), F32),
            pltpu.VMEM((GLA_PROJ_PIECE[0], d), BF16),
            pltpu.VMEM((tm, vd), BF16),
            pltpu.VMEM((GLA_HEADS, dk, dv), F32),
            pltpu.SMEM((2,), F32),
        ],
        compiler_params=pltpu.CompilerParams(
            dimension_semantics=("arbitrary",), vmem_limit_bytes=VMEM_LIMIT_BYTES),
        name="gla",
    )(x2, x2, g, win, wg2, bg, gn, wout, *[w for w, _ in cast])


def kernel(x, mix_norm, a_w_in, a_ln_g, a_ln_b, a_w_s, a_b_s, a_w_out, b_w_in, b_w_g2, b_b_g, b_gn_g, b_w_out, ffn_norm, ffn_w_gate, ffn_w_up, ffn_w_down, final_norm):
    b, t, d = x.shape
    depth = mix_norm.shape[0]
    assert depth == 2 and a_w_in.shape[0] == 1 and b_w_in.shape[0] == 1
    m = b * t
    bf = lambda w: w.astype(BF16)
    row = lambda v: v.reshape(1, -1)

    x2 = x.reshape(m, d)
    x2, wg0, wu0, wd0 = _sgu(
        x2, row(mix_norm[0]), bf(a_w_in[0]), row(a_ln_g[0]), row(a_ln_b[0]),
        a_w_s[0], a_b_s[0].T, bf(a_w_out[0]), tm=512,
        cast=((ffn_w_gate, 0), (ffn_w_up, 0), (ffn_w_down, 0)))
    x2, b_win, b_wout = _ffn(
        x2, row(ffn_norm[0]), wg0, wu0, wd0, row(final_norm), final_norm=False, tm=1024,
        cast=((b_w_in, 0), (b_w_out, 0)))
    x2, wg1, wu1, wd1 = _gla(
        x2, row(mix_norm[1]), b_win, b_w_g2[0], row(b_b_g[0]), b_gn_g[0], b_wout, tm=256, seq_len=t,
        cast=((ffn_w_gate, 1), (ffn_w_up, 1), (ffn_w_down, 1)))
    (x2,) = _ffn(x2, row(ffn_norm[1]), wg1, wu1, wd1, row(final_norm), final_norm=True, tm=1024)
    return x2.reshape(b, t, d)
```

```python
import functools

import jax
import jax.numpy as jnp
from jax import lax
from jax.experimental import pallas as pl
from jax.experimental.pallas import tpu as pltpu

F32 = jnp.float32
BF16 = jnp.bfloat16

EPS = 1e-6
CHUNK = 64
SG_BLOCK = 128
SG_GROUPS = 8
SG_GROUPS_PER_CHUNK = 2
GLA_HEADS = 4
GLA_TAU = 16.0
GLA_FAST_RANGE = 60.0
GLA_TASK_START = 0
GLA_SKEW = 1
GLA_PROJ_PIECE = (256, 512)

BF16_SUBLANES = 16
MXU_N = 256
FFN_HIDDEN_CHUNKS = 4

VMEM_LIMIT_BYTES = 56 * 1024 * 1024


def _rms(x, g):
    return x * lax.rsqrt(jnp.mean(x * x, axis=-1, keepdims=True) + EPS) * g


def _gelu(x):
    return 0.5 * x * (1.0 + lax.erf(x * (2.0 ** -0.5)))


def _dot(a, b):
    return jnp.dot(a, b, preferred_element_type=F32)


def _dot_nt(a, b):
    return lax.dot_general(a, b, (((1,), (1,)), ((), ())), preferred_element_type=F32)


def _dot_tn(a, b):
    return lax.dot_general(a, b, (((0,), (0,)), ((), ())), preferred_element_type=F32)


def _resident(shape):
    return pl.BlockSpec(shape, lambda *_: (0,) * len(shape))


def _cast_through(weights, nsteps):
    in_specs, out_specs, out_shapes = [], [], []
    for w, layer in weights:
        _, r, c = w.shape
        per = 1 if r % (nsteps * BF16_SUBLANES) == 0 else 2
        rb = r * per // nsteps
        assert rb % BF16_SUBLANES == 0 and rb * nsteps == r * per, (r, c, nsteps)
        in_specs.append(pl.BlockSpec((None, rb, c), lambda i, layer=layer, per=per: (layer, i // per, 0)))
        out_specs.append(pl.BlockSpec((rb, c), lambda i, per=per: (i // per, 0)))
        out_shapes.append(jax.ShapeDtypeStruct((r, c), BF16))
    return in_specs, out_specs, out_shapes


def _run_casts(src_refs, dst_refs):
    for src, dst in zip(src_refs, dst_refs):
        dst[...] = src[...].astype(BF16)


def _ffn_kernel(x_ref, xn_ref, g_ref, wg_ref, wu_ref, wd_ref, fin_ref, *rest, final_norm, ncast):
    o_ref = rest[ncast]
    ha_s, hb_s, g0_s = rest[2 * ncast + 1:]
    _run_casts(rest[:ncast], rest[ncast + 1:2 * ncast + 1])
    step = pl.program_id(0)

    hid = wg_ref.shape[1]
    ntiles = hid // MXU_N
    bounds = [MXU_N * (ntiles * j // FFN_HIDDEN_CHUNKS) for j in range(FFN_HIDDEN_CHUNKS + 1)]

    def lookahead(src_ref, h_dst):
        h = _rms(src_ref[...], g_ref[...]).astype(BF16)
        h_dst[...] = h
        g0_s[...] = _dot(h, wg_ref[:, bounds[0]:bounds[1]])

    @pl.when(step == 0)
    def _():
        lookahead(x_ref, ha_s)

    def body(h_cur, h_nxt):
        h = h_cur[...]
        y = x_ref[...]
        for j, (c0, c1) in enumerate(zip(bounds[:-1], bounds[1:])):
            gate = g0_s[...] if j == 0 else _dot(h, wg_ref[:, c0:c1])
            up = _dot(h, wu_ref[:, c0:c1])
            if j == 1:
                lookahead(xn_ref, h_nxt)
            a = (jax.nn.silu(gate) * up).astype(BF16)
            y = y + _dot(a, wd_ref[c0:c1, :])
        if final_norm:
            y = _rms(y, fin_ref[...])
        o_ref[...] = y

    pl.when(step % 2 == 0)(functools.partial(body, ha_s, hb_s))
    pl.when(step % 2 == 1)(functools.partial(body, hb_s, ha_s))


def _ffn(x2, g, wg, wu, wd, fin, *, final_norm, tm, cast=()):
    m, d = x2.shape
    hid = wg.shape[1]
    nsteps = m // tm
    c_in, c_out, c_shapes = _cast_through(cast, nsteps)
    return pl.pallas_call(
        functools.partial(_ffn_kernel, final_norm=final_norm, ncast=len(cast)),
        grid=(nsteps,),
        in_specs=[
            pl.BlockSpec((tm, d), lambda i: (i, 0)),
            pl.BlockSpec((tm, d), lambda i: (jnp.minimum(i + 1, nsteps - 1), 0)),
            _resident((1, d)),
            _resident((d, hid)),
            _resident((d, hid)),
            _resident((hid, d)),
            _resident((1, d)),
        ] + c_in,
        out_specs=[pl.BlockSpec((tm, d), lambda i: (i, 0))] + c_out,
        out_shape=[jax.ShapeDtypeStruct((m, d), F32)] + c_shapes,
        scratch_shapes=[pltpu.VMEM((tm, d), BF16), pltpu.VMEM((tm, d), BF16),
                        pltpu.VMEM((tm, MXU_N * (hid // MXU_N // FFN_HIDDEN_CHUNKS)), F32)],
        compiler_params=pltpu.CompilerParams(
            dimension_semantics=("arbitrary",), vmem_limit_bytes=VMEM_LIMIT_BYTES),
        name="swiglu",
    )(x2, x2, g, wg, wu, wd, fin, *[w for w, _ in cast])


def _sgu_kernel(x_ref, g_ref, win_ref, lng_ref, lnb_ref, ws_ref, bst_ref, wout_ref,
                *rest, tm, ncast):
    o_ref = rest[ncast]
    v_s, y_s = rest[2 * ncast + 1:]
    _run_casts(rest[:ncast], rest[ncast + 1:2 * ncast + 1])
    width = v_s.shape[1]
    gc = width // SG_GROUPS
    lanes = 128
    x = x_ref[...]
    h = _rms(x, g_ref[...]).astype(BF16)

    gpc = SG_GROUPS_PER_CHUNK
    cw = gpc * gc
    p1 = jnp.zeros((tm, lanes), F32)
    p2 = jnp.zeros((tm, lanes), F32)
    for c in range(SG_GROUPS // gpc):
        v = _gelu(_dot(h, win_ref[:, width + c * cw:width + (c + 1) * cw]))
        v_s[:, c * cw:(c + 1) * cw] = v
        for l in range(cw // lanes):
            vl = v[:, l * lanes:(l + 1) * lanes]
            p1 = p1 + vl
            p2 = p2 + vl * vl
    mu = jnp.sum(p1, axis=-1, keepdims=True) * (1.0 / width)
    var = jnp.sum(p2, axis=-1, keepdims=True) * (1.0 / width) - mu * mu
    rstd = lax.rsqrt(var + EPS)
    shift = -mu * rstd

    ti = lax.broadcasted_iota(jnp.int32, (SG_BLOCK, SG_BLOCK), 0) // CHUNK
    si = lax.broadcasted_iota(jnp.int32, (SG_BLOCK, SG_BLOCK), 1) // CHUNK
    causal = ti >= si
    for c in range(SG_GROUPS // gpc):
        ccols = slice(c * cw, (c + 1) * cw)
        u = _gelu(_dot(h, win_ref[:, ccols]))
        vn = ((v_s[:, ccols] * rstd + shift) * lng_ref[:, ccols] + lnb_ref[:, ccols]).astype(BF16)
        for gg in range(gpc):
            g = c * gpc + gg
            gcols = slice(gg * gc, (gg + 1) * gc)
            ws = jnp.where(causal, ws_ref[g], 0.0).astype(BF16)
            bias = bst_ref[:, g:g + 1]
            for w in range(tm // SG_BLOCK):
                rows = slice(w * SG_BLOCK, (w + 1) * SG_BLOCK)
                s = _dot(ws, vn[rows, gcols]) + bias
                y_s[rows, g * gc:(g + 1) * gc] = (u[rows, gcols] * s).astype(BF16)
    o_ref[...] = x + _dot(y_s[...], wout_ref[...])


def _sgu(x2, g, win, lng, lnb, ws, bst, wout, *, tm, cast=()):
    m, d = x2.shape
    width = wout.shape[0]
    c_in, c_out, c_shapes = _cast_through(cast, m // tm)
    return pl.pallas_call(
        functools.partial(_sgu_kernel, tm=tm, ncast=len(cast)),
        grid=(m // tm,),
        in_specs=[
            pl.BlockSpec((tm, d), lambda i: (i, 0)),
            _resident((1, d)),
            _resident((d, 2 * width)),
            _resident((1, width)),
            _resident((1, width)),
            _resident(ws.shape),
            _resident(bst.shape),
            _resident((width, d)),
        ] + c_in,
        out_specs=[pl.BlockSpec((tm, d), lambda i: (i, 0))] + c_out,
        out_shape=[jax.ShapeDtypeStruct((m, d), F32)] + c_shapes,
        scratch_shapes=[
            pltpu.VMEM((tm, width), F32),
            pltpu.VMEM((tm, width), BF16),
        ],
        compiler_params=pltpu.CompilerParams(
            dimension_semantics=("arbitrary",), vmem_limit_bytes=VMEM_LIMIT_BYTES),
        name="sgu",
    )(x2, g, win, lng, lnb, ws, bst, wout, *[w for w, _ in cast])


def _ref_rows(G, half):
    n, w = G.shape
    blk = 2 * half
    if blk >= 8:
        return jnp.concatenate(
            [jnp.broadcast_to(G[b * blk + half - 1:b * blk + half, :], (blk, w))
             for b in range(n // blk)], axis=0)
    assert blk == 4
    sub = lax.broadcasted_iota(jnp.int32, (8, w), 0)
    tiles = []
    for t in range(n // 8):
        lo = jnp.broadcast_to(G[8 * t + 1:8 * t + 2, :], (8, w))
        hi = jnp.broadcast_to(G[8 * t + 5:8 * t + 6, :], (8, w))
        tiles.append(jnp.where(sub < 4, lo, hi))
    return jnp.concatenate(tiles, axis=0)


def _gla_project(x_ref, row0, tm, g_ref, win_ref, wg2_ref, bg_ref, proj_ref, la_ref, h_ref, flag_ref, flag_idx):
    state = {}
    ncol = proj_ref.shape[1]
    pr, pc = GLA_PROJ_PIECE
    last = tm // pr - 1 if tm > pr else -1
    tasks = []

    def norm(r):
        def run():
            x = x_ref[row0 + r * pr:row0 + (r + 1) * pr, :]
            state[r] = _rms(x, g_ref[...]).astype(BF16)
            if r == 0:
                flag_ref[flag_idx] = jnp.float32(0.0)
            if r == last:
                h_ref[...] = state[r]
        run.vpu = True
        return run

    def cols(r, p):
        def run():
            h = h_ref[...] if r == last else state[r]
            proj_ref[r * pr:(r + 1) * pr, p * pc:(p + 1) * pc] = _dot(h, win_ref[:, p * pc:(p + 1) * pc])
        run.vpu = False
        return run

    def gate(r):
        def run():
            gz = _dot(state[r], win_ref[:, ncol:])
            z = _dot(gz.astype(BF16), wg2_ref[...].astype(BF16)) + bg_ref[...]
            la = (jnp.minimum(z, 0.0) - jnp.log1p(jnp.exp(-jnp.abs(z)))) / GLA_TAU
            la_ref[r * pr:(r + 1) * pr, :] = la
            tot = [jnp.sum(la[c * CHUNK:(c + 1) * CHUNK, :], axis=0, keepdims=True) for c in range(pr // CHUNK)]
            flag_ref[flag_idx] = jnp.minimum(flag_ref[flag_idx], jnp.min(functools.reduce(jnp.minimum, tot)))
        run.vpu = True
        return run

    for r in range(tm // pr):
        tasks += [norm(r), gate(r)] + ([] if r == last else [cols(r, p) for p in range(ncol // pc)])
    return tasks, ([cols(last, p) for p in range(ncol // pc)] if last >= 0 else [])


def _gla_chunks(proj_ref, la_ref, og_s, st_s, gn_ref, consts, between, *, tm, kd, vd, fast):
    dk = kd // GLA_HEADS
    dv = vd // GLA_HEADS
    C = CHUNK
    tri3, halves, lvl_mask, diag, upper, causal = consts
    scale = dk ** -0.5
    between = list(between)
    nchunks = tm // C
    nslots = nchunks * GLA_HEADS

    def chunk_decay(c):
        la_all = la_ref[c * C:(c + 1) * C, :]
        la_hi = la_all.astype(BF16)
        r1 = la_all - la_hi.astype(F32)
        la_mid = r1.astype(BF16)
        la_lo = (r1 - la_mid.astype(F32)).astype(BF16)
        return la_all, _dot(tri3, jnp.concatenate([la_hi, la_mid, la_lo, jnp.zeros_like(la_hi)], axis=0))

    def scores(c, hd, la_all, G_all):
        rows = slice(c * C, (c + 1) * C)
        ks_ = slice(hd * dk, (hd + 1) * dk)
        q = proj_ref[rows, hd * dk:(hd + 1) * dk] * scale
        k = proj_ref[rows, kd + hd * dk:kd + (hd + 1) * dk]
        la = la_all[:, ks_]
        G = G_all[:, ks_]
        g_last = G[C - 1:C, :]
        qd = (q * jnp.exp(G)).astype(BF16)
        kdec = (k * jnp.exp(g_last - G)).astype(BF16)
        dec = jnp.exp(g_last)
        if fast:
            return [_dot_nt(qd, (k * jnp.exp(-G)).astype(BF16))], qd, kdec, dec
        ps = [jnp.sum(q * k, axis=-1, keepdims=True)]
        for lvl, hf in enumerate(halves):
            if hf == 1:
                e = jnp.exp(jnp.where(upper[lvl], la, 0.0))
            else:
                e = jnp.exp(-jnp.abs(G - _ref_rows(G, hf)))
            xl = (jnp.where(upper[lvl], q, k) * e).astype(BF16)
            ps.append(_dot_nt(xl, xl))
        return ps, qd, kdec, dec

    def outputs(c, hd, ps, qd, kdec, dec):
        rows = slice(c * C, (c + 1) * C)
        vs_ = slice(2 * kd + hd * dv, 2 * kd + (hd + 1) * dv)
        rs_ = slice(2 * kd + vd + hd * dv, 2 * kd + vd + (hd + 1) * dv)
        if fast:
            a = jnp.where(causal, ps[0], 0.0)
        else:
            a = jnp.where(diag, ps[0], 0.0)
            for lvl in range(len(halves)):
                a = jnp.where(lvl_mask[lvl], ps[lvl + 1], a)
        v = proj_ref[rows, vs_].astype(BF16)
        st = st_s[hd]
        pad = jnp.zeros((C, MXU_N - dk - C), BF16)
        lhs = jnp.concatenate([qd, a.astype(BF16), pad], axis=1)
        rhs = jnp.concatenate([st.astype(BF16), v, jnp.zeros((MXU_N - dk - C, dv), BF16)], axis=0)
        o = _dot(lhs, rhs)
        dcol = jnp.transpose(jnp.broadcast_to(dec, (dk, dk)))
        st_s[hd] = st * jnp.concatenate([dcol] * (dv // dk), axis=1) + _dot_tn(kdec, v)
        o = o * lax.rsqrt(jnp.mean(o * o, axis=-1, keepdims=True) + EPS) * gn_ref[hd:hd + 1, :]
        r = proj_ref[rows, rs_]
        og_s[rows, hd * dv:(hd + 1) * dv] = (jax.nn.silu(r) * o).astype(BF16)

    decay = {0: chunk_decay(0)}
    pending = []
    for slot in range(nslots):
        c, hd = divmod(slot, GLA_HEADS)
        if hd == 0 and c + 1 < nchunks:
            decay[c + 1] = chunk_decay(c + 1)
        pending.append((c, hd) + scores(c, hd, *decay[c]))
        ts = max(slot - GLA_TASK_START, 0)
        te = max(slot + 1 - GLA_TASK_START, 0)
        nts = nslots - GLA_TASK_START
        for task in between[ts * len(between) // nts:te * len(between) // nts]:
            task()
        if len(pending) > GLA_SKEW:
            outputs(*pending.pop(0))
    for item in pending:
        outputs(*item)


def _gla_kernel(xc_ref, xn_ref, g_ref, win_ref, wg2_ref, bg_ref, gn_ref, wout_ref,
                *rest, tm, kd, vd, tiles_per_seq, ncast):
    o_ref = rest[ncast]
    pa_s, la_a, ha_s, pb_s, la_b, hb_s, og_s, st_s, flag_s = rest[2 * ncast + 1:]
    _run_casts(rest[:ncast], rest[ncast + 1:2 * ncast + 1])
    C = CHUNK
    step = pl.program_id(0)
    wts = (g_ref, win_ref, wg2_ref, bg_ref)

    @pl.when(step == 0)
    def _():
        for task in _gla_project(xc_ref, 0, tm, *wts, pa_s, la_a, ha_s, flag_s, 0)[0]:
            task()

    @pl.when((2 * step) % tiles_per_seq == 0)
    def _():
        st_s[...] = jnp.zeros_like(st_s)

    ri = lax.broadcasted_iota(jnp.int32, (C, C), 0)
    ci = lax.broadcasted_iota(jnp.int32, (C, C), 1)
    tri = (ri >= ci).astype(BF16)
    tri3 = jnp.concatenate([tri, tri, tri, jnp.zeros_like(tri)], axis=1)
    halves = (32, 16, 8, 4, 2, 1)
    lvl_mask = [((ri // (2 * hf)) == (ci // (2 * hf))) & ((ri % (2 * hf)) >= hf) & ((ci % (2 * hf)) < hf)
                for hf in halves]
    diag = ri == ci
    rowc = lax.broadcasted_iota(jnp.int32, (C, kd // GLA_HEADS), 0)
    upper = [(rowc % (2 * hf)) >= hf for hf in halves]
    consts = (tri3, halves, lvl_mask, diag, upper, ri >= ci)

    def tile(proj_ref, la_ref, h_ref, flag_idx, nxt_x, nxt_row0, nxt_proj, nxt_la, nxt_h, nxt_flag):
        def run(fast):
            def body():
                own_late = _gla_project(None, 0, tm, *wts, proj_ref, la_ref, h_ref, flag_s, flag_idx)[1]
                nxt_early = _gla_project(nxt_x, nxt_row0, tm, *wts, nxt_proj, nxt_la, nxt_h, flag_s, nxt_flag)[0]
                vpu = [t for t in nxt_early if t.vpu]
                mxu = own_late + [t for t in nxt_early if not t.vpu]
                tasks, gap = [], len(mxu) // len(vpu)
                for i, t in enumerate(vpu):
                    tasks += mxu[i * gap:i * gap + 1] + [t] + mxu[i * gap + 1:(i + 1) * gap]
                tasks += mxu[len(vpu) * gap:]
                if not own_late:
                    tasks = nxt_early
                _gla_chunks(proj_ref, la_ref, og_s, st_s, gn_ref, consts, tasks, tm=tm, kd=kd, vd=vd, fast=fast)
            return body
        mild = flag_s[flag_idx] > -GLA_FAST_RANGE
        pl.when(mild)(run(True))
        pl.when(jnp.logical_not(mild))(run(False))

    tile(pa_s, la_a, ha_s, 0, xc_ref, tm, pb_s, la_b, hb_s, 1)
    o_ref[0:tm, :] = xc_ref[0:tm, :] + _dot(og_s[...], wout_ref[...])
    tile(pb_s, la_b, hb_s, 1, xn_ref, 0, pa_s, la_a, ha_s, 0)
    o_ref[tm:2 * tm, :] = xc_ref[tm:2 * tm, :] + _dot(og_s[...], wout_ref[...])


def _gla(x2, g, win, wg2, bg, gn, wout, *, tm, seq_len, cast=()):
    m, d = x2.shape
    kd = wg2.shape[1]
    vd = wout.shape[0]
    dk = kd // GLA_HEADS
    dv = vd // GLA_HEADS
    nproj = win.shape[1] - wg2.shape[0]
    ntiles = m // tm
    assert seq_len % (2 * tm) == 0 and nproj % GLA_PROJ_PIECE[1] == 0 and tm % GLA_PROJ_PIECE[0] == 0
    c_in, c_out, c_shapes = _cast_through(cast, ntiles // 2)
    return pl.pallas_call(
        functools.partial(_gla_kernel, tm=tm, kd=kd, vd=vd, tiles_per_seq=seq_len // tm, ncast=len(cast)),
        grid=(ntiles // 2,),
        in_specs=[
            pl.BlockSpec((2 * tm, d), lambda i: (i, 0)),
            pl.BlockSpec((tm, d), lambda i: (jnp.minimum(2 * i + 2, ntiles - 1), 0)),
            _resident((1, d)),
            _resident(win.shape),
            _resident(wg2.shape),
            _resident((1, kd)),
            _resident(gn.shape),
            _resident((vd, d)),
        ] + c_in,
        out_specs=[pl.BlockSpec((2 * tm, d), lambda i: (i, 0))] + c_out,
        out_shape=[jax.ShapeDtypeStruct((m, d), F32)] + c_shapes,
        scratch_shapes=[
            pltpu.VMEM((tm, nproj), F32),
            pltpu.VMEM((tm, kd), F32),
            pltpu.VMEM((GLA_PROJ_PIECE[0], d), BF16),
            pltpu.VMEM((tm, nproj), F32),
            pltpu.VMEM((tm, kd), F32),
            pltpu.VMEM((GLA_PROJ_PIECE[0], d), BF16),
            pltpu.VMEM((tm, vd), BF16),
            pltpu.VMEM((GLA_HEADS, dk, dv), F32),
            pltpu.SMEM((2,), F32),
        ],
        compiler_params=pltpu.CompilerParams(
            dimension_semantics=("arbitrary",), vmem_limit_bytes=VMEM_LIMIT_BYTES),
        name="gla",
    )(x2, x2, g, win, wg2, bg, gn, wout, *[w for w, _ in cast])


def kernel(x, mix_norm, a_w_in, a_ln_g, a_ln_b, a_w_s, a_b_s, a_w_out, b_w_in, b_w_g2, b_b_g, b_gn_g, b_w_out, ffn_norm, ffn_w_gate, ffn_w_up, ffn_w_down, final_norm):
    b, t, d = x.shape
    depth = mix_norm.shape[0]
    assert depth == 2 and a_w_in.shape[0] == 1 and b_w_in.shape[0] == 1
    m = b * t
    bf = lambda w: w.astype(BF16)
    row = lambda v: v.reshape(1, -1)

    x2 = x.reshape(m, d)
    x2, wg0, wu0, wd0 = _sgu(
        x2, row(mix_norm[0]), bf(a_w_in[0]), row(a_ln_g[0]), row(a_ln_b[0]),
        a_w_s[0], a_b_s[0].T, bf(a_w_out[0]), tm=512,
        cast=((ffn_w_gate, 0), (ffn_w_up, 0), (ffn_w_down, 0)))
    x2, b_wout = _ffn(
        x2, row(ffn_norm[0]), wg0, wu0, wd0, row(final_norm), final_norm=False, tm=512,
        cast=((b_w_out, 0),))
    x2, wg1, wu1, wd1 = _gla(
        x2, row(mix_norm[1]), bf(b_w_in[0]), b_w_g2[0], row(b_b_g[0]), b_gn_g[0], b_wout, tm=256, seq_len=t,
        cast=((ffn_w_gate, 1), (ffn_w_up, 1), (ffn_w_down, 1)))
    (x2,) = _ffn(x2, row(ffn_norm[1]), wg1, wu1, wd1, row(final_norm), final_norm=True, tm=512)
    return x2.reshape(b, t, d)
```

```python
import functools

import jax
import jax.numpy as jnp
from jax import lax
from jax.experimental import pallas as pl
from jax.experimental.pallas import tpu as pltpu

F32 = jnp.float32
BF16 = jnp.bfloat16

EPS = 1e-6
CHUNK = 64
SG_BLOCK = 128
SG_GROUPS = 8
SG_GROUPS_PER_CHUNK = 2
GLA_HEADS = 4
GLA_TAU = 16.0
GLA_FAST_RANGE = 60.0
GLA_SKEW = 1
GLA_PROJ_PIECE = (256, 512)

BF16_SUBLANES = 16
MXU_N = 256
FFN_HIDDEN_CHUNKS = 4

SGU_TM = 512
FFN_TM = 1024
GLA_TM = 256

VMEM_LIMIT_BYTES = 56 * 1024 * 1024


def _rms(x, g):
    return x * lax.rsqrt(jnp.mean(x * x, axis=-1, keepdims=True) + EPS) * g


def _gelu(x):
    return 0.5 * x * (1.0 + lax.erf(x * (2.0 ** -0.5)))


def _dot(a, b):
    return jnp.dot(a, b, preferred_element_type=F32)


def _dot_nt(a, b):
    return lax.dot_general(a, b, (((1,), (1,)), ((), ())), preferred_element_type=F32)


def _dot_tn(a, b):
    return lax.dot_general(a, b, (((0,), (0,)), ((), ())), preferred_element_type=F32)


def _resident(shape):
    return pl.BlockSpec(shape, lambda *_: (0,) * len(shape))


def _cast_through(weights, nsteps):
    in_specs, out_specs, out_shapes = [], [], []
    for w, layer in weights:
        _, r, c = w.shape
        per = 1 if r % (nsteps * BF16_SUBLANES) == 0 else 2
        rb = r * per // nsteps
        assert rb % BF16_SUBLANES == 0 and rb * nsteps == r * per, (r, c, nsteps)
        in_specs.append(pl.BlockSpec((None, rb, c), lambda i, layer=layer, per=per: (layer, i // per, 0)))
        out_specs.append(pl.BlockSpec((rb, c), lambda i, per=per: (i // per, 0)))
        out_shapes.append(jax.ShapeDtypeStruct((r, c), BF16))
    return in_specs, out_specs, out_shapes


def _run_casts(src_refs, dst_refs):
    for src, dst in zip(src_refs, dst_refs):
        dst[...] = src[...].astype(BF16)


def _ffn_kernel(x_ref, g_ref, wg_ref, wu_ref, wd_ref, fin_ref, *rest, final_norm, ncast):
    o_ref = rest[ncast]
    _run_casts(rest[:ncast], rest[ncast + 1:])
    x = x_ref[...]
    h = _rms(x, g_ref[...]).astype(BF16)
    hid = wg_ref.shape[1]
    ntiles = hid // MXU_N
    bounds = [MXU_N * (ntiles * j // FFN_HIDDEN_CHUNKS) for j in range(FFN_HIDDEN_CHUNKS + 1)]
    y = x
    for c0, c1 in zip(bounds[:-1], bounds[1:]):
        gate = _dot(h, wg_ref[:, c0:c1])
        up = _dot(h, wu_ref[:, c0:c1])
        a = (jax.nn.silu(gate) * up).astype(BF16)
        y = y + _dot(a, wd_ref[c0:c1, :])
    if final_norm:
        y = _rms(y, fin_ref[...])
    o_ref[...] = y


def _ffn(x2, g, wg, wu, wd, fin, *, final_norm, tm, cast=()):
    m, d = x2.shape
    hid = wg.shape[1]
    c_in, c_out, c_shapes = _cast_through(cast, m // tm)
    return pl.pallas_call(
        functools.partial(_ffn_kernel, final_norm=final_norm, ncast=len(cast)),
        grid=(m // tm,),
        in_specs=[
            pl.BlockSpec((tm, d), lambda i: (i, 0)),
            _resident((1, d)),
            _resident((d, hid)),
            _resident((d, hid)),
            _resident((hid, d)),
            _resident((1, d)),
        ] + c_in,
        out_specs=[pl.BlockSpec((tm, d), lambda i: (i, 0))] + c_out,
        out_shape=[jax.ShapeDtypeStruct((m, d), F32)] + c_shapes,
        compiler_params=pltpu.CompilerParams(
            dimension_semantics=("arbitrary",), vmem_limit_bytes=VMEM_LIMIT_BYTES),
        name="swiglu",
    )(x2, g, wg, wu, wd, fin, *[w for w, _ in cast])


def _sgu_kernel(x_ref, g_ref, win_ref, lng_ref, lnb_ref, ws_ref, bst_ref, wout_ref,
                *rest, tm, ncast):
    o_ref = rest[ncast]
    v_s, y_s = rest[2 * ncast + 1:]
    _run_casts(rest[:ncast], rest[ncast + 1:2 * ncast + 1])
    width = v_s.shape[1]
    gc = width // SG_GROUPS
    lanes = 128
    x = x_ref[...]
    h = _rms(x, g_ref[...]).astype(BF16)

    gpc = SG_GROUPS_PER_CHUNK
    cw = gpc * gc
    p1 = jnp.zeros((tm, lanes), F32)
    p2 = jnp.zeros((tm, lanes), F32)
    for c in range(SG_GROUPS // gpc):
        v = _gelu(_dot(h, win_ref[:, width + c * cw:width + (c + 1) * cw]))
        v_s[:, c * cw:(c + 1) * cw] = v
        for l in range(cw // lanes):
            vl = v[:, l * lanes:(l + 1) * lanes]
            p1 = p1 + vl
            p2 = p2 + vl * vl
    mu = jnp.sum(p1, axis=-1, keepdims=True) * (1.0 / width)
    var = jnp.sum(p2, axis=-1, keepdims=True) * (1.0 / width) - mu * mu
    rstd = lax.rsqrt(var + EPS)
    shift = -mu * rstd

    ti = lax.broadcasted_iota(jnp.int32, (SG_BLOCK, SG_BLOCK), 0) // CHUNK
    si = lax.broadcasted_iota(jnp.int32, (SG_BLOCK, SG_BLOCK), 1) // CHUNK
    causal = ti >= si
    for c in range(SG_GROUPS // gpc):
        ccols = slice(c * cw, (c + 1) * cw)
        u = _gelu(_dot(h, win_ref[:, ccols]))
        vn = ((v_s[:, ccols] * rstd + shift) * lng_ref[:, ccols] + lnb_ref[:, ccols]).astype(BF16)
        for gg in range(gpc):
            g = c * gpc + gg
            gcols = slice(gg * gc, (gg + 1) * gc)
            ws = jnp.where(causal, ws_ref[g], 0.0).astype(BF16)
            bias = bst_ref[:, g:g + 1]
            for w in range(tm // SG_BLOCK):
                rows = slice(w * SG_BLOCK, (w + 1) * SG_BLOCK)
                s = _dot(ws, vn[rows, gcols]) + bias
                y_s[rows, g * gc:(g + 1) * gc] = (u[rows, gcols] * s).astype(BF16)
    o_ref[...] = x + _dot(y_s[...], wout_ref[...])


def _sgu(x2, g, win, lng, lnb, ws, bst, wout, *, tm, cast=()):
    m, d = x2.shape
    width = wout.shape[0]
    c_in, c_out, c_shapes = _cast_through(cast, m // tm)
    return pl.pallas_call(
        functools.partial(_sgu_kernel, tm=tm, ncast=len(cast)),
        grid=(m // tm,),
        in_specs=[
            pl.BlockSpec((tm, d), lambda i: (i, 0)),
            _resident((1, d)),
            _resident((d, 2 * width)),
            _resident((1, width)),
            _resident((1, width)),
            _resident(ws.shape),
            _resident(bst.shape),
            _resident((width, d)),
        ] + c_in,
        out_specs=[pl.BlockSpec((tm, d), lambda i: (i, 0))] + c_out,
        out_shape=[jax.ShapeDtypeStruct((m, d), F32)] + c_shapes,
        scratch_shapes=[
            pltpu.VMEM((tm, width), F32),
            pltpu.VMEM((tm, width), BF16),
        ],
        compiler_params=pltpu.CompilerParams(
            dimension_semantics=("arbitrary",), vmem_limit_bytes=VMEM_LIMIT_BYTES),
        name="sgu",
    )(x2, g, win, lng, lnb, ws, bst, wout, *[w for w, _ in cast])


def _ref_rows(G, half):
    n, w = G.shape
    blk = 2 * half
    if blk >= 8:
        return jnp.concatenate(
            [jnp.broadcast_to(G[b * blk + half - 1:b * blk + half, :], (blk, w))
             for b in range(n // blk)], axis=0)
    assert blk == 4
    sub = lax.broadcasted_iota(jnp.int32, (8, w), 0)
    tiles = []
    for t in range(n // 8):
        lo = jnp.broadcast_to(G[8 * t + 1:8 * t + 2, :], (8, w))
        hi = jnp.broadcast_to(G[8 * t + 5:8 * t + 6, :], (8, w))
        tiles.append(jnp.where(sub < 4, lo, hi))
    return jnp.concatenate(tiles, axis=0)


def _gla_project(x_ref, row0, tm, g_ref, win_ref, wg2_ref, bg_ref, proj_ref, la_ref, flag_ref, flag_idx):
    state = {}
    ncol = proj_ref.shape[1]
    pr, pc = GLA_PROJ_PIECE
    tasks = []

    def norm(r):
        def run():
            x = x_ref[row0 + r * pr:row0 + (r + 1) * pr, :]
            state[r] = _rms(x, g_ref[...]).astype(BF16)
            if r == 0:
                flag_ref[flag_idx] = jnp.float32(0.0)
        return run

    def cols(r, p):
        def run():
            proj_ref[r * pr:(r + 1) * pr, p * pc:(p + 1) * pc] = _dot(state[r], win_ref[:, p * pc:(p + 1) * pc])
        return run

    def gate(r):
        def run():
            gz = _dot(state[r], win_ref[:, ncol:])
            z = _dot(gz.astype(BF16), wg2_ref[...].astype(BF16)) + bg_ref[...]
            la = (jnp.minimum(z, 0.0) - jnp.log1p(jnp.exp(-jnp.abs(z)))) / GLA_TAU
            la_ref[r * pr:(r + 1) * pr, :] = la
            tot = [jnp.sum(la[c * CHUNK:(c + 1) * CHUNK, :], axis=0, keepdims=True) for c in range(pr // CHUNK)]
            flag_ref[flag_idx] = jnp.minimum(flag_ref[flag_idx], jnp.min(functools.reduce(jnp.minimum, tot)))
        return run

    for r in range(tm // pr):
        tasks += [norm(r), gate(r)] + [cols(r, p) for p in range(ncol // pc)]
    return tasks


def _gla_chunks(proj_ref, la_ref, og_s, st_s, gn_ref, consts, between, *, tm, kd, vd, fast):
    dk = kd // GLA_HEADS
    dv = vd // GLA_HEADS
    C = CHUNK
    tri3, halves, lvl_mask, diag, upper, causal = consts
    scale = dk ** -0.5
    between = list(between)
    nchunks = tm // C
    nslots = nchunks * GLA_HEADS

    def chunk_decay(c):
        la_all = la_ref[c * C:(c + 1) * C, :]
        la_hi = la_all.astype(BF16)
        r1 = la_all - la_hi.astype(F32)
        la_mid = r1.astype(BF16)
        la_lo = (r1 - la_mid.astype(F32)).astype(BF16)
        return la_all, _dot(tri3, jnp.concatenate([la_hi, la_mid, la_lo, jnp.zeros_like(la_hi)], axis=0))

    def scores(c, hd, la_all, G_all):
        rows = slice(c * C, (c + 1) * C)
        ks_ = slice(hd * dk, (hd + 1) * dk)
        q = proj_ref[rows, hd * dk:(hd + 1) * dk] * scale
        k = proj_ref[rows, kd + hd * dk:kd + (hd + 1) * dk]
        la = la_all[:, ks_]
        G = G_all[:, ks_]
        g_last = G[C - 1:C, :]
        qd = (q * jnp.exp(G)).astype(BF16)
        kdec = (k * jnp.exp(g_last - G)).astype(BF16)
        dec = jnp.exp(g_last)
        if fast:
            return [_dot_nt(qd, (k * jnp.exp(-G)).astype(BF16))], qd, kdec, dec
        ps = [jnp.sum(q * k, axis=-1, keepdims=True)]
        for lvl, hf in enumerate(halves):
            if hf == 1:
                e = jnp.exp(jnp.where(upper[lvl], la, 0.0))
            else:
                e = jnp.exp(-jnp.abs(G - _ref_rows(G, hf)))
            xl = (jnp.where(upper[lvl], q, k) * e).astype(BF16)
            ps.append(_dot_nt(xl, xl))
        return ps, qd, kdec, dec

    def outputs(c, hd, ps, qd, kdec, dec):
        rows = slice(c * C, (c + 1) * C)
        vs_ = slice(2 * kd + hd * dv, 2 * kd + (hd + 1) * dv)
        rs_ = slice(2 * kd + vd + hd * dv, 2 * kd + vd + (hd + 1) * dv)
        if fast:
            a = jnp.where(causal, ps[0], 0.0)
        else:
            a = jnp.where(diag, ps[0], 0.0)
            for lvl in range(len(halves)):
                a = jnp.where(lvl_mask[lvl], ps[lvl + 1], a)
        v = proj_ref[rows, vs_].astype(BF16)
        st = st_s[hd]
        pad = jnp.zeros((C, MXU_N - dk - C), BF16)
        lhs = jnp.concatenate([qd, a.astype(BF16), pad], axis=1)
        rhs = jnp.concatenate([st.astype(BF16), v, jnp.zeros((MXU_N - dk - C, dv), BF16)], axis=0)
        o = _dot(lhs, rhs)
        dcol = jnp.transpose(jnp.broadcast_to(dec, (dk, dk)))
        st_s[hd] = st * jnp.concatenate([dcol] * (dv // dk), axis=1) + _dot_tn(kdec, v)
        o = o * lax.rsqrt(jnp.mean(o * o, axis=-1, keepdims=True) + EPS) * gn_ref[hd:hd + 1, :]
        r = proj_ref[rows, rs_]
        og_s[rows, hd * dv:(hd + 1) * dv] = (jax.nn.silu(r) * o).astype(BF16)

    decay = {0: chunk_decay(0)}
    pending = []
    for slot in range(nslots):
        c, hd = divmod(slot, GLA_HEADS)
        if hd == 0 and c + 1 < nchunks:
            decay[c + 1] = chunk_decay(c + 1)
        pending.append((c, hd) + scores(c, hd, *decay[c]))
        for task in between[slot * len(between) // nslots:(slot + 1) * len(between) // nslots]:
            task()
        if len(pending) > GLA_SKEW:
            outputs(*pending.pop(0))
    for item in pending:
        outputs(*item)


def _gla_kernel(xc_ref, xn_ref, g_ref, win_ref, wg2_ref, bg_ref, gn_ref, wout_ref,
                *rest, tm, kd, vd, tiles_per_seq, ncast):
    o_ref = rest[ncast]
    pa_s, la_a, pb_s, la_b, og_s, st_s, flag_s = rest[2 * ncast + 1:]
    _run_casts(rest[:ncast], rest[ncast + 1:2 * ncast + 1])
    C = CHUNK
    step = pl.program_id(0)
    wts = (g_ref, win_ref, wg2_ref, bg_ref)

    @pl.when(step == 0)
    def _():
        for task in _gla_project(xc_ref, 0, tm, *wts, pa_s, la_a, flag_s, 0):
            task()

    @pl.when((2 * step) % tiles_per_seq == 0)
    def _():
        st_s[...] = jnp.zeros_like(st_s)

    ri = lax.broadcasted_iota(jnp.int32, (C, C), 0)
    ci = lax.broadcasted_iota(jnp.int32, (C, C), 1)
    tri = (ri >= ci).astype(BF16)
    tri3 = jnp.concatenate([tri, tri, tri, jnp.zeros_like(tri)], axis=1)
    halves = (32, 16, 8, 4, 2, 1)
    lvl_mask = [((ri // (2 * hf)) == (ci // (2 * hf))) & ((ri % (2 * hf)) >= hf) & ((ci % (2 * hf)) < hf)
                for hf in halves]
    diag = ri == ci
    rowc = lax.broadcasted_iota(jnp.int32, (C, kd // GLA_HEADS), 0)
    upper = [(rowc % (2 * hf)) >= hf for hf in halves]
    consts = (tri3, halves, lvl_mask, diag, upper, ri >= ci)

    def tile(proj_ref, la_ref, flag_idx, nxt_x, nxt_row0, nxt_proj, nxt_la, nxt_flag):
        def run(fast):
            def body():
                nxt = _gla_project(nxt_x, nxt_row0, tm, *wts, nxt_proj, nxt_la, flag_s, nxt_flag)
                _gla_chunks(proj_ref, la_ref, og_s, st_s, gn_ref, consts, nxt, tm=tm, kd=kd, vd=vd, fast=fast)
            return body
        mild = flag_s[flag_idx] > -GLA_FAST_RANGE
        pl.when(mild)(run(True))
        pl.when(jnp.logical_not(mild))(run(False))

    tile(pa_s, la_a, 0, xc_ref, tm, pb_s, la_b, 1)
    o_ref[0:tm, :] = xc_ref[0:tm, :] + _dot(og_s[...], wout_ref[...])
    tile(pb_s, la_b, 1, xn_ref, 0, pa_s, la_a, 0)
    o_ref[tm:2 * tm, :] = xc_ref[tm:2 * tm, :] + _dot(og_s[...], wout_ref[...])


def _gla(x2, g, win, wg2, bg, gn, wout, *, tm, seq_len, cast=()):
    m, d = x2.shape
    kd = wg2.shape[1]
    vd = wout.shape[0]
    dk = kd // GLA_HEADS
    dv = vd // GLA_HEADS
    nproj = win.shape[1] - wg2.shape[0]
    ntiles = m // tm
    assert seq_len % (2 * tm) == 0 and nproj % GLA_PROJ_PIECE[1] == 0 and tm % GLA_PROJ_PIECE[0] == 0
    c_in, c_out, c_shapes = _cast_through(cast, ntiles // 2)
    return pl.pallas_call(
        functools.partial(_gla_kernel, tm=tm, kd=kd, vd=vd, tiles_per_seq=seq_len // tm, ncast=len(cast)),
        grid=(ntiles // 2,),
        in_specs=[
            pl.BlockSpec((2 * tm, d), lambda i: (i, 0)),
            pl.BlockSpec((tm, d), lambda i: (jnp.minimum(2 * i + 2, ntiles - 1), 0)),
            _resident((1, d)),
            _resident(win.shape),
            _resident(wg2.shape),
            _resident((1, kd)),
            _resident(gn.shape),
            _resident((vd, d)),
        ] + c_in,
        out_specs=[pl.BlockSpec((2 * tm, d), lambda i: (i, 0))] + c_out,
        out_shape=[jax.ShapeDtypeStruct((m, d), F32)] + c_shapes,
        scratch_shapes=[
            pltpu.VMEM((tm, nproj), F32),
            pltpu.VMEM((tm, kd), F32),
            pltpu.VMEM((tm, nproj), F32),
            pltpu.VMEM((tm, kd), F32),
            pltpu.VMEM((tm, vd), BF16),
            pltpu.VMEM((GLA_HEADS, dk, dv), F32),
            pltpu.SMEM((2,), F32),
        ],
        compiler_params=pltpu.CompilerParams(
            dimension_semantics=("arbitrary",), vmem_limit_bytes=VMEM_LIMIT_BYTES),
        name="gla",
    )(x2, x2, g, win, wg2, bg, gn, wout, *[w for w, _ in cast])


def kernel(x, mix_norm, a_w_in, a_ln_g, a_ln_b, a_w_s, a_b_s, a_w_out, b_w_in, b_w_g2, b_b_g, b_gn_g, b_w_out, ffn_norm, ffn_w_gate, ffn_w_up, ffn_w_down, final_norm):
    b, t, d = x.shape
    depth = mix_norm.shape[0]
    assert depth == 2 and a_w_in.shape[0] == 1 and b_w_in.shape[0] == 1
    m = b * t
    bf = lambda w: w.astype(BF16)
    row = lambda v: v.reshape(1, -1)

    x2 = x.reshape(m, d)
    x2, wg0, wu0, wd0 = _sgu(
        x2, row(mix_norm[0]), bf(a_w_in[0]), row(a_ln_g[0]), row(a_ln_b[0]),
        a_w_s[0], a_b_s[0].T, bf(a_w_out[0]), tm=SGU_TM,
        cast=((ffn_w_gate, 0), (ffn_w_up, 0), (ffn_w_down, 0)))
    x2, b_wout = _ffn(
        x2, row(ffn_norm[0]), wg0, wu0, wd0, row(final_norm), final_norm=False, tm=FFN_TM,
        cast=((b_w_out, 0),))
    x2, wg1, wu1, wd1 = _gla(
        x2, row(mix_norm[1]), bf(b_w_in[0]), b_w_g2[0], row(b_b_g[0]), b_gn_g[0], b_wout, tm=GLA_TM, seq_len=t,
        cast=((ffn_w_gate, 1), (ffn_w_up, 1), (ffn_w_down, 1)))
    (x2,) = _ffn(x2, row(ffn_norm[1]), wg1, wu1, wd1, row(final_norm), final_norm=True, tm=FFN_TM)
    return x2.reshape(b, t, d)
```

```python
import functools

import jax
import jax.numpy as jnp
from jax import lax
from jax.experimental import pallas as pl
from jax.experimental.pallas import tpu as pltpu

F32 = jnp.float32
BF16 = jnp.bfloat16

EPS = 1e-6
CHUNK = 64
SG_BLOCK = 128
SG_GROUPS = 8
SG_GROUPS_PER_CHUNK = 2
GLA_HEADS = 4
GLA_TAU = 16.0
GLA_FAST_RANGE = 60.0
GLA_SKEW = 1
GLA_PROJ_PIECE = (256, 512)

BF16_SUBLANES = 16
MXU_N = 256
FFN_HIDDEN_CHUNKS = 4

SGU_TM = 512
FFN_TM = 1024
GLA_TM = 256

VMEM_LIMIT_BYTES = 56 * 1024 * 1024


def _rms(x, g):
    return x * lax.rsqrt(jnp.mean(x * x, axis=-1, keepdims=True) + EPS) * g


def _gelu(x):
    return 0.5 * x * (1.0 + lax.erf(x * (2.0 ** -0.5)))


def _dot(a, b):
    return jnp.dot(a, b, preferred_element_type=F32)


def _dot_nt(a, b):
    return lax.dot_general(a, b, (((1,), (1,)), ((), ())), preferred_element_type=F32)


def _dot_tn(a, b):
    return lax.dot_general(a, b, (((0,), (0,)), ((), ())), preferred_element_type=F32)


def _resident(shape):
    return pl.BlockSpec(shape, lambda *_: (0,) * len(shape))


def _cast_through(weights, nsteps):
    in_specs, out_specs, out_shapes = [], [], []
    for w, layer in weights:
        _, r, c = w.shape
        per = 1 if r % (nsteps * BF16_SUBLANES) == 0 else 2
        rb = r * per // nsteps
        assert rb % BF16_SUBLANES == 0 and rb * nsteps == r * per, (r, c, nsteps)
        in_specs.append(pl.BlockSpec((None, rb, c), lambda i, layer=layer, per=per: (layer, i // per, 0)))
        out_specs.append(pl.BlockSpec((rb, c), lambda i, per=per: (i // per, 0)))
        out_shapes.append(jax.ShapeDtypeStruct((r, c), BF16))
    return in_specs, out_specs, out_shapes


def _run_casts(src_refs, dst_refs):
    for src, dst in zip(src_refs, dst_refs):
        dst[...] = src[...].astype(BF16)


def _ffn_kernel(x_ref, g_ref, wg_ref, wu_ref, wd_ref, fin_ref, *rest, final_norm, ncast):
    o_ref = rest[ncast]
    _run_casts(rest[:ncast], rest[ncast + 1:])
    x = x_ref[...]
    h = _rms(x, g_ref[...]).astype(BF16)
    hid = wg_ref.shape[1]
    ntiles = hid // MXU_N
    bounds = [MXU_N * (ntiles * j // FFN_HIDDEN_CHUNKS) for j in range(FFN_HIDDEN_CHUNKS + 1)]
    y = x
    for c0, c1 in zip(bounds[:-1], bounds[1:]):
        gate = _dot(h, wg_ref[:, c0:c1])
        up = _dot(h, wu_ref[:, c0:c1])
        a = (jax.nn.silu(gate) * up).astype(BF16)
        y = y + _dot(a, wd_ref[c0:c1, :])
    if final_norm:
        y = _rms(y, fin_ref[...])
    o_ref[...] = y


def _ffn(x2, g, wg, wu, wd, fin, *, final_norm, tm, cast=()):
    m, d = x2.shape
    hid = wg.shape[1]
    c_in, c_out, c_shapes = _cast_through(cast, m // tm)
    return pl.pallas_call(
        functools.partial(_ffn_kernel, final_norm=final_norm, ncast=len(cast)),
        grid=(m // tm,),
        in_specs=[
            pl.BlockSpec((tm, d), lambda i: (i, 0)),
            _resident((1, d)),
            _resident((d, hid)),
            _resident((d, hid)),
            _resident((hid, d)),
            _resident((1, d)),
        ] + c_in,
        out_specs=[pl.BlockSpec((tm, d), lambda i: (i, 0))] + c_out,
        out_shape=[jax.ShapeDtypeStruct((m, d), F32)] + c_shapes,
        compiler_params=pltpu.CompilerParams(
            dimension_semantics=("arbitrary",), vmem_limit_bytes=VMEM_LIMIT_BYTES),
        name="swiglu",
    )(x2, g, wg, wu, wd, fin, *[w for w, _ in cast])


def _sgu_kernel(x_ref, g_ref, win_ref, lng_ref, lnb_ref, ws_ref, bst_ref, wout_ref,
                *rest, tm, ncast):
    o_ref = rest[ncast]
    v_s, y_s = rest[2 * ncast + 1:]
    _run_casts(rest[:ncast], rest[ncast + 1:2 * ncast + 1])
    width = v_s.shape[1]
    gc = width // SG_GROUPS
    lanes = 128
    x = x_ref[...]
    h = _rms(x, g_ref[...]).astype(BF16)

    gpc = SG_GROUPS_PER_CHUNK
    cw = gpc * gc
    p1 = jnp.zeros((tm, lanes), F32)
    p2 = jnp.zeros((tm, lanes), F32)
    for c in range(SG_GROUPS // gpc):
        v = _gelu(_dot(h, win_ref[:, width + c * cw:width + (c + 1) * cw]))
        v_s[:, c * cw:(c + 1) * cw] = v
        for l in range(cw // lanes):
            vl = v[:, l * lanes:(l + 1) * lanes]
            p1 = p1 + vl
            p2 = p2 + vl * vl
    mu = jnp.sum(p1, axis=-1, keepdims=True) * (1.0 / width)
    var = jnp.sum(p2, axis=-1, keepdims=True) * (1.0 / width) - mu * mu
    rstd = lax.rsqrt(var + EPS)
    shift = -mu * rstd

    ti = lax.broadcasted_iota(jnp.int32, (SG_BLOCK, SG_BLOCK), 0) // CHUNK
    si = lax.broadcasted_iota(jnp.int32, (SG_BLOCK, SG_BLOCK), 1) // CHUNK
    causal = ti >= si
    for c in range(SG_GROUPS // gpc):
        ccols = slice(c * cw, (c + 1) * cw)
        u = _gelu(_dot(h, win_ref[:, ccols]))
        vn = ((v_s[:, ccols] * rstd + shift) * lng_ref[:, ccols] + lnb_ref[:, ccols]).astype(BF16)
        for gg in range(gpc):
            g = c * gpc + gg
            gcols = slice(gg * gc, (gg + 1) * gc)
            ws = jnp.where(causal, ws_ref[g], 0.0).astype(BF16)
            bias = bst_ref[:, g:g + 1]
            for w in range(tm // SG_BLOCK):
                rows = slice(w * SG_BLOCK, (w + 1) * SG_BLOCK)
                s = _dot(ws, vn[rows, gcols]) + bias
                y_s[rows, g * gc:(g + 1) * gc] = (u[rows, gcols] * s).astype(BF16)
    o_ref[...] = x + _dot(y_s[...], wout_ref[...])


def _sgu(x2, g, win, lng, lnb, ws, bst, wout, *, tm, cast=()):
    m, d = x2.shape
    width = wout.shape[0]
    c_in, c_out, c_shapes = _cast_through(cast, m // tm)
    return pl.pallas_call(
        functools.partial(_sgu_kernel, tm=tm, ncast=len(cast)),
        grid=(m // tm,),
        in_specs=[
            pl.BlockSpec((tm, d), lambda i: (i, 0)),
            _resident((1, d)),
            _resident((d, 2 * width)),
            _resident((1, width)),
            _resident((1, width)),
            _resident(ws.shape),
            _resident(bst.shape),
            _resident((width, d)),
        ] + c_in,
        out_specs=[pl.BlockSpec((tm, d), lambda i: (i, 0))] + c_out,
        out_shape=[jax.ShapeDtypeStruct((m, d), F32)] + c_shapes,
        scratch_shapes=[
            pltpu.VMEM((tm, width), F32),
            pltpu.VMEM((tm, width), BF16),
        ],
        compiler_params=pltpu.CompilerParams(
            dimension_semantics=("arbitrary",), vmem_limit_bytes=VMEM_LIMIT_BYTES),
        name="sgu",
    )(x2, g, win, lng, lnb, ws, bst, wout, *[w for w, _ in cast])


def _ref_rows(G, half):
    n, w = G.shape
    blk = 2 * half
    if blk >= 8:
        return jnp.concatenate(
            [jnp.broadcast_to(G[b * blk + half - 1:b * blk + half, :], (blk, w))
             for b in range(n // blk)], axis=0)
    assert blk == 4
    sub = lax.broadcasted_iota(jnp.int32, (8, w), 0)
    tiles = []
    for t in range(n // 8):
        lo = jnp.broadcast_to(G[8 * t + 1:8 * t + 2, :], (8, w))
        hi = jnp.broadcast_to(G[8 * t + 5:8 * t + 6, :], (8, w))
        tiles.append(jnp.where(sub < 4, lo, hi))
    return jnp.concatenate(tiles, axis=0)


def _gla_project(x_ref, row0, tm, g_ref, win_ref, wg2_ref, bg_ref, proj_ref, la_ref, flag_ref, flag_idx):
    state = {}
    ncol = proj_ref.shape[1]
    pr, pc = GLA_PROJ_PIECE
    tasks = []

    def norm(r):
        def run():
            x = x_ref[row0 + r * pr:row0 + (r + 1) * pr, :]
            state[r] = _rms(x, g_ref[...]).astype(BF16)
            if r == 0:
                flag_ref[flag_idx] = jnp.float32(0.0)
        return run

    def cols(r, p):
        def run():
            proj_ref[r * pr:(r + 1) * pr, p * pc:(p + 1) * pc] = _dot(state[r], win_ref[:, p * pc:(p + 1) * pc])
        return run

    def gate(r):
        def run():
            gz = _dot(state[r], win_ref[:, ncol:])
            z = _dot(gz.astype(BF16), wg2_ref[...].astype(BF16)) + bg_ref[...]
            la = (jnp.minimum(z, 0.0) - jnp.log1p(jnp.exp(-jnp.abs(z)))) / GLA_TAU
            la_ref[r * pr:(r + 1) * pr, :] = la
            tot = [jnp.sum(la[c * CHUNK:(c + 1) * CHUNK, :], axis=0, keepdims=True) for c in range(pr // CHUNK)]
            flag_ref[flag_idx] = jnp.minimum(flag_ref[flag_idx], jnp.min(functools.reduce(jnp.minimum, tot)))
        return run

    for r in range(tm // pr):
        tasks += [norm(r), gate(r)] + [cols(r, p) for p in range(ncol // pc)]
    return tasks


def _gla_chunks(proj_ref, la_ref, og_s, st_s, gn_ref, consts, between, *, tm, kd, vd, fast):
    dk = kd // GLA_HEADS
    dv = vd // GLA_HEADS
    C = CHUNK
    tri3, halves, lvl_mask, diag, upper, causal = consts
    scale = dk ** -0.5
    between = list(between)
    nchunks = tm // C
    nslots = nchunks * GLA_HEADS

    def chunk_decay(c):
        la_all = la_ref[c * C:(c + 1) * C, :]
        la_hi = la_all.astype(BF16)
        r1 = la_all - la_hi.astype(F32)
        la_mid = r1.astype(BF16)
        la_lo = (r1 - la_mid.astype(F32)).astype(BF16)
        return la_all, _dot(tri3, jnp.concatenate([la_hi, la_mid, la_lo, jnp.zeros_like(la_hi)], axis=0))

    def scores(c, hd, la_all, G_all):
        rows = slice(c * C, (c + 1) * C)
        ks_ = slice(hd * dk, (hd + 1) * dk)
        q = proj_ref[rows, hd * dk:(hd + 1) * dk] * scale
        k = proj_ref[rows, kd + hd * dk:kd + (hd + 1) * dk]
        la = la_all[:, ks_]
        G = G_all[:, ks_]
        g_last = G[C - 1:C, :]
        qd = (q * jnp.exp(G)).astype(BF16)
        kdec = (k * jnp.exp(g_last - G)).astype(BF16)
        dec = jnp.exp(g_last)
        if fast:
            return [_dot_nt(qd, (k * jnp.exp(-G)).astype(BF16))], qd, kdec, dec
        ps = [jnp.sum(q * k, axis=-1, keepdims=True)]
        for lvl, hf in enumerate(halves):
            if hf == 1:
                e = jnp.exp(jnp.where(upper[lvl], la, 0.0))
            else:
                e = jnp.exp(-jnp.abs(G - _ref_rows(G, hf)))
            xl = (jnp.where(upper[lvl], q, k) * e).astype(BF16)
            ps.append(_dot_nt(xl, xl))
        return ps, qd, kdec, dec

    def outputs(c, hd, ps, qd, kdec, dec):
        rows = slice(c * C, (c + 1) * C)
        vs_ = slice(2 * kd + hd * dv, 2 * kd + (hd + 1) * dv)
        rs_ = slice(2 * kd + vd + hd * dv, 2 * kd + vd + (hd + 1) * dv)
        if fast:
            a = jnp.where(causal, ps[0], 0.0)
        else:
            a = jnp.where(diag, ps[0], 0.0)
            for lvl in range(len(halves)):
                a = jnp.where(lvl_mask[lvl], ps[lvl + 1], a)
        v = proj_ref[rows, vs_].astype(BF16)
        st = st_s[hd]
        pad = jnp.zeros((C, MXU_N - dk - C), BF16)
        lhs = jnp.concatenate([qd, a.astype(BF16), pad], axis=1)
        rhs = jnp.concatenate([st.astype(BF16), v, jnp.zeros((MXU_N - dk - C, dv), BF16)], axis=0)
        o = _dot(lhs, rhs)
        dcol = jnp.transpose(jnp.broadcast_to(dec, (dk, dk)))
        st_s[hd] = st * jnp.concatenate([dcol] * (dv // dk), axis=1) + _dot_tn(kdec, v)
        o = o * lax.rsqrt(jnp.mean(o * o, axis=-1, keepdims=True) + EPS) * gn_ref[hd:hd + 1, :]
        r = proj_ref[rows, rs_]
        og_s[rows, hd * dv:(hd + 1) * dv] = (jax.nn.silu(r) * o).astype(BF16)

    decay = {0: chunk_decay(0)}
    pending = []
    for slot in range(nslots):
        c, hd = divmod(slot, GLA_HEADS)
        if hd == 0 and c + 1 < nchunks:
            decay[c + 1] = chunk_decay(c + 1)
        pending.append((c, hd) + scores(c, hd, *decay[c]))
        for task in between[slot * len(between) // nslots:(slot + 1) * len(between) // nslots]:
            task()
        if len(pending) > GLA_SKEW:
            outputs(*pending.pop(0))
    for item in pending:
        outputs(*item)


def _gla_kernel(xc_ref, xn_ref, g_ref, win_ref, wg2_ref, bg_ref, gn_ref, wout_ref,
                *rest, tm, kd, vd, tiles_per_seq, ncast):
    o_ref = rest[ncast]
    pa_s, la_a, pb_s, la_b, og_s, st_s, flag_s = rest[2 * ncast + 1:]
    _run_casts(rest[:ncast], rest[ncast + 1:2 * ncast + 1])
    C = CHUNK
    step = pl.program_id(0)
    wts = (g_ref, win_ref, wg2_ref, bg_ref)

    @pl.when(step == 0)
    def _():
        for task in _gla_project(xc_ref, 0, tm, *wts, pa_s, la_a, flag_s, 0):
            task()

    @pl.when((2 * step) % tiles_per_seq == 0)
    def _():
        st_s[...] = jnp.zeros_like(st_s)

    ri = lax.broadcasted_iota(jnp.int32, (C, C), 0)
    ci = lax.broadcasted_iota(jnp.int32, (C, C), 1)
    tri = (ri >= ci).astype(BF16)
    tri3 = jnp.concatenate([tri, tri, tri, jnp.zeros_like(tri)], axis=1)
    halves = (32, 16, 8, 4, 2, 1)
    lvl_mask = [((ri // (2 * hf)) == (ci // (2 * hf))) & ((ri % (2 * hf)) >= hf) & ((ci % (2 * hf)) < hf)
                for hf in halves]
    diag = ri == ci
    rowc = lax.broadcasted_iota(jnp.int32, (C, kd // GLA_HEADS), 0)
    upper = [(rowc % (2 * hf)) >= hf for hf in halves]
    consts = (tri3, halves, lvl_mask, diag, upper, ri >= ci)

    def tile(proj_ref, la_ref, og_ref, flag_idx, nxt_x, nxt_row0, nxt_proj, nxt_la, nxt_flag):
        def run(fast):
            def body():
                nxt = _gla_project(nxt_x, nxt_row0, tm, *wts, nxt_proj, nxt_la, flag_s, nxt_flag)
                _gla_chunks(proj_ref, la_ref, og_ref, st_s, gn_ref, consts, nxt, tm=tm, kd=kd, vd=vd, fast=fast)
            return body
        mild = flag_s[flag_idx] > -GLA_FAST_RANGE
        pl.when(mild)(run(True))
        pl.when(jnp.logical_not(mild))(run(False))

    tile(pa_s, la_a, og_s.at[pl.ds(0, tm)], 0, xc_ref, tm, pb_s, la_b, 1)
    tile(pb_s, la_b, og_s.at[pl.ds(tm, tm)], 1, xn_ref, 0, pa_s, la_a, 0)
    o_ref[...] = xc_ref[...] + _dot(og_s[...], wout_ref[...])


def _gla(x2, g, win, wg2, bg, gn, wout, *, tm, seq_len, cast=()):
    m, d = x2.shape
    kd = wg2.shape[1]
    vd = wout.shape[0]
    dk = kd // GLA_HEADS
    dv = vd // GLA_HEADS
    nproj = win.shape[1] - wg2.shape[0]
    ntiles = m // tm
    assert seq_len % (2 * tm) == 0 and nproj % GLA_PROJ_PIECE[1] == 0 and tm % GLA_PROJ_PIECE[0] == 0
    c_in, c_out, c_shapes = _cast_through(cast, ntiles // 2)
    return pl.pallas_call(
        functools.partial(_gla_kernel, tm=tm, kd=kd, vd=vd, tiles_per_seq=seq_len // tm, ncast=len(cast)),
        grid=(ntiles // 2,),
        in_specs=[
            pl.BlockSpec((2 * tm, d), lambda i: (i, 0)),
            pl.BlockSpec((tm, d), lambda i: (jnp.minimum(2 * i + 2, ntiles - 1), 0)),
            _resident((1, d)),
            _resident(win.shape),
            _resident(wg2.shape),
            _resident((1, kd)),
            _resident(gn.shape),
            _resident((vd, d)),
        ] + c_in,
        out_specs=[pl.BlockSpec((2 * tm, d), lambda i: (i, 0))] + c_out,
        out_shape=[jax.ShapeDtypeStruct((m, d), F32)] + c_shapes,
        scratch_shapes=[
            pltpu.VMEM((tm, nproj), F32),
            pltpu.VMEM((tm, kd), F32),
            pltpu.VMEM((tm, nproj), F32),
            pltpu.VMEM((tm, kd), F32),
            pltpu.VMEM((2 * tm, vd), BF16),
            pltpu.VMEM((GLA_HEADS, dk, dv), F32),
            pltpu.SMEM((2,), F32),
        ],
        compiler_params=pltpu.CompilerParams(
            dimension_semantics=("arbitrary",), vmem_limit_bytes=VMEM_LIMIT_BYTES),
        name="gla",
    )(x2, x2, g, win, wg2, bg, gn, wout, *[w for w, _ in cast])


def kernel(x, mix_norm, a_w_in, a_ln_g, a_ln_b, a_w_s, a_b_s, a_w_out, b_w_in, b_w_g2, b_b_g, b_gn_g, b_w_out, ffn_norm, ffn_w_gate, ffn_w_up, ffn_w_down, final_norm):
    b, t, d = x.shape
    depth = mix_norm.shape[0]
    assert depth == 2 and a_w_in.shape[0] == 1 and b_w_in.shape[0] == 1
    m = b * t
    bf = lambda w: w.astype(BF16)
    row = lambda v: v.reshape(1, -1)

    x2 = x.reshape(m, d)
    x2, wg0, wu0, wd0 = _sgu(
        x2, row(mix_norm[0]), bf(a_w_in[0]), row(a_ln_g[0]), row(a_ln_b[0]),
        a_w_s[0], a_b_s[0].T, bf(a_w_out[0]), tm=SGU_TM,
        cast=((ffn_w_gate, 0), (ffn_w_up, 0), (ffn_w_down, 0)))
    x2, b_wout = _ffn(
        x2, row(ffn_norm[0]), wg0, wu0, wd0, row(final_norm), final_norm=False, tm=FFN_TM,
        cast=((b_w_out, 0),))
    x2, wg1, wu1, wd1 = _gla(
        x2, row(mix_norm[1]), bf(b_w_in[0]), b_w_g2[0], row(b_b_g[0]), b_gn_g[0], b_wout, tm=GLA_TM, seq_len=t,
        cast=((ffn_w_gate, 1), (ffn_w_up, 1), (ffn_w_down, 1)))
    (x2,) = _ffn(x2, row(ffn_norm[1]), wg1, wu1, wd1, row(final_norm), final_norm=True, tm=FFN_TM)
    return x2.reshape(b, t, d)
```

```python
import functools

import jax
import jax.numpy as jnp
from jax import lax
from jax.experimental import pallas as pl
from jax.experimental.pallas import tpu as pltpu

F32 = jnp.float32
BF16 = jnp.bfloat16

EPS = 1e-6
CHUNK = 64
SG_BLOCK = 128
SG_GROUPS = 8
SG_GROUPS_PER_CHUNK = 2
GLA_HEADS = 4
GLA_TAU = 16.0
GLA_FAST_RANGE = 60.0
GLA_SKEW = 1
GLA_PROJ_PIECE = (256, 512)

BF16_SUBLANES = 16
MXU_N = 256
FFN_HIDDEN_CHUNKS = 4

SGU_TM = 512
FFN_TM = 1024
GLA_TM = 256

VMEM_LIMIT_BYTES = 56 * 1024 * 1024


def _rms(x, g):
    return x * lax.rsqrt(jnp.mean(x * x, axis=-1, keepdims=True) + EPS) * g


def _gelu(x):
    return 0.5 * x * (1.0 + lax.erf(x * (2.0 ** -0.5)))


def _dot(a, b):
    return jnp.dot(a, b, preferred_element_type=F32)


def _dot_nt(a, b):
    return lax.dot_general(a, b, (((1,), (1,)), ((), ())), preferred_element_type=F32)


def _dot_tn(a, b):
    return lax.dot_general(a, b, (((0,), (0,)), ((), ())), preferred_element_type=F32)


def _resident(shape):
    return pl.BlockSpec(shape, lambda *_: (0,) * len(shape))


def _cast_through(weights, nsteps):
    in_specs, out_specs, out_shapes = [], [], []
    for w, layer in weights:
        _, r, c = w.shape
        per = 1 if r % (nsteps * BF16_SUBLANES) == 0 else 2
        rb = r * per // nsteps
        assert rb % BF16_SUBLANES == 0 and rb * nsteps == r * per, (r, c, nsteps)
        in_specs.append(pl.BlockSpec((None, rb, c), lambda i, layer=layer, per=per: (layer, i // per, 0)))
        out_specs.append(pl.BlockSpec((rb, c), lambda i, per=per: (i // per, 0)))
        out_shapes.append(jax.ShapeDtypeStruct((r, c), BF16))
    return in_specs, out_specs, out_shapes


def _run_casts(src_refs, dst_refs):
    for src, dst in zip(src_refs, dst_refs):
        dst[...] = src[...].astype(BF16)


def _ffn_kernel(x_ref, g_ref, wg_ref, wu_ref, wd_ref, fin_ref, *rest, final_norm, ncast):
    o_ref = rest[ncast]
    _run_casts(rest[:ncast], rest[ncast + 1:])
    x = x_ref[...]
    h = _rms(x, g_ref[...]).astype(BF16)
    hid = wg_ref.shape[1]
    ntiles = hid // MXU_N
    bounds = [MXU_N * (ntiles * j // FFN_HIDDEN_CHUNKS) for j in range(FFN_HIDDEN_CHUNKS + 1)]
    y = x
    for c0, c1 in zip(bounds[:-1], bounds[1:]):
        gate = _dot(h, wg_ref[:, c0:c1])
        up = _dot(h, wu_ref[:, c0:c1])
        a = (jax.nn.silu(gate) * up).astype(BF16)
        y = y + _dot(a, wd_ref[c0:c1, :])
    if final_norm:
        y = _rms(y, fin_ref[...])
    o_ref[...] = y


def _ffn(x2, g, wg, wu, wd, fin, *, final_norm, tm, cast=()):
    m, d = x2.shape
    hid = wg.shape[1]
    c_in, c_out, c_shapes = _cast_through(cast, m // tm)
    return pl.pallas_call(
        functools.partial(_ffn_kernel, final_norm=final_norm, ncast=len(cast)),
        grid=(m // tm,),
        in_specs=[
            pl.BlockSpec((tm, d), lambda i: (i, 0)),
            _resident((1, d)),
            _resident((d, hid)),
            _resident((d, hid)),
            _resident((hid, d)),
            _resident((1, d)),
        ] + c_in,
        out_specs=[pl.BlockSpec((tm, d), lambda i: (i, 0))] + c_out,
        out_shape=[jax.ShapeDtypeStruct((m, d), F32)] + c_shapes,
        compiler_params=pltpu.CompilerParams(
            dimension_semantics=("arbitrary",), vmem_limit_bytes=VMEM_LIMIT_BYTES),
        name="swiglu",
    )(x2, g, wg, wu, wd, fin, *[w for w, _ in cast])


def _sgu_kernel(x_ref, g_ref, win_ref, lng_ref, lnb_ref, ws_ref, bst_ref, wout_ref,
                *rest, tm, ncast):
    o_ref = rest[ncast]
    v_s, y_s = rest[2 * ncast + 1:]
    _run_casts(rest[:ncast], rest[ncast + 1:2 * ncast + 1])
    width = v_s.shape[1]
    gc = width // SG_GROUPS
    lanes = 128
    x = x_ref[...]
    h = _rms(x, g_ref[...]).astype(BF16)

    gpc = SG_GROUPS_PER_CHUNK
    cw = gpc * gc
    p1 = jnp.zeros((tm, lanes), F32)
    p2 = jnp.zeros((tm, lanes), F32)
    for c in range(SG_GROUPS // gpc):
        v = _gelu(_dot(h, win_ref[:, width + c * cw:width + (c + 1) * cw]))
        v_s[:, c * cw:(c + 1) * cw] = v
        for l in range(cw // lanes):
            vl = v[:, l * lanes:(l + 1) * lanes]
            p1 = p1 + vl
            p2 = p2 + vl * vl
    mu = jnp.sum(p1, axis=-1, keepdims=True) * (1.0 / width)
    var = jnp.sum(p2, axis=-1, keepdims=True) * (1.0 / width) - mu * mu
    rstd = lax.rsqrt(var + EPS)
    shift = -mu * rstd

    ti = lax.broadcasted_iota(jnp.int32, (SG_BLOCK, SG_BLOCK), 0) // CHUNK
    si = lax.broadcasted_iota(jnp.int32, (SG_BLOCK, SG_BLOCK), 1) // CHUNK
    causal = ti >= si
    for c in range(SG_GROUPS // gpc):
        ccols = slice(c * cw, (c + 1) * cw)
        u = _gelu(_dot(h, win_ref[:, ccols]))
        vn = ((v_s[:, ccols] * rstd + shift) * lng_ref[:, ccols] + lnb_ref[:, ccols]).astype(BF16)
        for gg in range(gpc):
            g = c * gpc + gg
            gcols = slice(gg * gc, (gg + 1) * gc)
            ws = jnp.where(causal, ws_ref[g], 0.0).astype(BF16)
            bias = bst_ref[:, g:g + 1]
            for w in range(tm // SG_BLOCK):
                rows = slice(w * SG_BLOCK, (w + 1) * SG_BLOCK)
                s = _dot(ws, vn[rows, gcols]) + bias
                y_s[rows, g * gc:(g + 1) * gc] = (u[rows, gcols] * s).astype(BF16)
    o_ref[...] = x + _dot(y_s[...], wout_ref[...])


def _sgu(x2, g, win, lng, lnb, ws, bst, wout, *, tm, cast=()):
    m, d = x2.shape
    width = wout.shape[0]
    c_in, c_out, c_shapes = _cast_through(cast, m // tm)
    return pl.pallas_call(
        functools.partial(_sgu_kernel, tm=tm, ncast=len(cast)),
        grid=(m // tm,),
        in_specs=[
            pl.BlockSpec((tm, d), lambda i: (i, 0)),
            _resident((1, d)),
            _resident((d, 2 * width)),
            _resident((1, width)),
            _resident((1, width)),
            _resident(ws.shape),
            _resident(bst.shape),
            _resident((width, d)),
        ] + c_in,
        out_specs=[pl.BlockSpec((tm, d), lambda i: (i, 0))] + c_out,
        out_shape=[jax.ShapeDtypeStruct((m, d), F32)] + c_shapes,
        scratch_shapes=[
            pltpu.VMEM((tm, width), F32),
            pltpu.VMEM((tm, width), BF16),
        ],
        compiler_params=pltpu.CompilerParams(
            dimension_semantics=("arbitrary",), vmem_limit_bytes=VMEM_LIMIT_BYTES),
        name="sgu",
    )(x2, g, win, lng, lnb, ws, bst, wout, *[w for w, _ in cast])


def _ref_rows(G, half):
    n, w = G.shape
    blk = 2 * half
    if blk >= 8:
        return jnp.concatenate(
            [jnp.broadcast_to(G[b * blk + half - 1:b * blk + half, :], (blk, w))
             for b in range(n // blk)], axis=0)
    assert blk == 4
    sub = lax.broadcasted_iota(jnp.int32, (8, w), 0)
    tiles = []
    for t in range(n // 8):
        lo = jnp.broadcast_to(G[8 * t + 1:8 * t + 2, :], (8, w))
        hi = jnp.broadcast_to(G[8 * t + 5:8 * t + 6, :], (8, w))
        tiles.append(jnp.where(sub < 4, lo, hi))
    return jnp.concatenate(tiles, axis=0)


def _gla_project(x_ref, row0, tm, g_ref, win_ref, wg2_ref, bg_ref, proj_ref, la_ref, flag_ref, flag_idx):
    state = {}
    ncol = proj_ref.shape[1]
    pr, pc = GLA_PROJ_PIECE
    tasks = []

    def norm(r):
        def run():
            x = x_ref[row0 + r * pr:row0 + (r + 1) * pr, :]
            state[r] = _rms(x, g_ref[...]).astype(BF16)
            if r == 0:
                flag_ref[flag_idx] = jnp.float32(0.0)
        return run

    def cols(r, p):
        def run():
            proj_ref[r * pr:(r + 1) * pr, p * pc:(p + 1) * pc] = _dot(state[r], win_ref[:, p * pc:(p + 1) * pc])
        return run

    def gate(r):
        def run():
            gz = _dot(state[r], win_ref[:, ncol:])
            z = _dot(gz.astype(BF16), wg2_ref[...].astype(BF16)) + bg_ref[...]
            la = (jnp.minimum(z, 0.0) - jnp.log1p(jnp.exp(-jnp.abs(z)))) / GLA_TAU
            la_ref[r * pr:(r + 1) * pr, :] = la
            tot = [jnp.sum(la[c * CHUNK:(c + 1) * CHUNK, :], axis=0, keepdims=True) for c in range(pr // CHUNK)]
            flag_ref[flag_idx] = jnp.minimum(flag_ref[flag_idx], jnp.min(functools.reduce(jnp.minimum, tot)))
        return run

    for r in range(tm // pr):
        tasks += [norm(r), gate(r)] + [cols(r, p) for p in range(ncol // pc)]
    return tasks


def _gla_chunks(proj_ref, la_ref, og_s, st_s, gn_ref, consts, between, *, tm, kd, vd, fast):
    dk = kd // GLA_HEADS
    dv = vd // GLA_HEADS
    C = CHUNK
    tri2, halves, lvl_mask, diag, upper, causal = consts
    scale = dk ** -0.5
    between = list(between)
    nchunks = tm // C
    nslots = nchunks * GLA_HEADS

    def chunk_decay(c):
        la_all = la_ref[c * C:(c + 1) * C, :]
        la_hi = la_all.astype(BF16)
        la_lo = (la_all - la_hi.astype(F32)).astype(BF16)
        return la_all, _dot(tri2, jnp.concatenate([la_hi, la_lo], axis=0))

    def scores(c, hd, la_all, G_all):
        rows = slice(c * C, (c + 1) * C)
        ks_ = slice(hd * dk, (hd + 1) * dk)
        q = proj_ref[rows, hd * dk:(hd + 1) * dk] * scale
        k = proj_ref[rows, kd + hd * dk:kd + (hd + 1) * dk]
        la = la_all[:, ks_]
        G = G_all[:, ks_]
        g_last = G[C - 1:C, :]
        qd = (q * jnp.exp(G)).astype(BF16)
        kdec = (k * jnp.exp(g_last - G)).astype(BF16)
        dec = jnp.exp(g_last)
        if fast:
            return [_dot_nt(qd, (k * jnp.exp(-G)).astype(BF16))], qd, kdec, dec
        ps = [jnp.sum(q * k, axis=-1, keepdims=True)]
        for lvl, hf in enumerate(halves):
            if hf == 1:
                e = jnp.exp(jnp.where(upper[lvl], la, 0.0))
            else:
                e = jnp.exp(-jnp.abs(G - _ref_rows(G, hf)))
            xl = (jnp.where(upper[lvl], q, k) * e).astype(BF16)
            ps.append(_dot_nt(xl, xl))
        return ps, qd, kdec, dec

    def outputs(c, hd, ps, qd, kdec, dec):
        rows = slice(c * C, (c + 1) * C)
        vs_ = slice(2 * kd + hd * dv, 2 * kd + (hd + 1) * dv)
        rs_ = slice(2 * kd + vd + hd * dv, 2 * kd + vd + (hd + 1) * dv)
        if fast:
            a = jnp.where(causal, ps[0], 0.0)
        else:
            a = jnp.where(diag, ps[0], 0.0)
            for lvl in range(len(halves)):
                a = jnp.where(lvl_mask[lvl], ps[lvl + 1], a)
        v = proj_ref[rows, vs_].astype(BF16)
        st = st_s[hd]
        lhs = jnp.concatenate([qd, a.astype(BF16)], axis=1)
        rhs = jnp.concatenate([st.astype(BF16), v], axis=0)
        o = _dot(lhs, rhs)
        dcol = jnp.transpose(jnp.broadcast_to(dec, (dk, dk)))
        st_s[hd] = st * jnp.concatenate([dcol] * (dv // dk), axis=1) + _dot_tn(kdec, v)
        o = o * lax.rsqrt(jnp.mean(o * o, axis=-1, keepdims=True) + EPS) * gn_ref[hd:hd + 1, :]
        r = proj_ref[rows, rs_]
        og_s[rows, hd * dv:(hd + 1) * dv] = (jax.nn.silu(r) * o).astype(BF16)

    decay = {0: chunk_decay(0)}
    pending = []
    for slot in range(nslots):
        c, hd = divmod(slot, GLA_HEADS)
        if hd == 0 and c + 1 < nchunks:
            decay[c + 1] = chunk_decay(c + 1)
        pending.append((c, hd) + scores(c, hd, *decay[c]))
        for task in between[slot * len(between) // nslots:(slot + 1) * len(between) // nslots]:
            task()
        if len(pending) > GLA_SKEW:
            outputs(*pending.pop(0))
    for item in pending:
        outputs(*item)


def _gla_kernel(xc_ref, xn_ref, g_ref, win_ref, wg2_ref, bg_ref, gn_ref, wout_ref,
                *rest, tm, kd, vd, tiles_per_seq, ncast):
    o_ref = rest[ncast]
    pa_s, la_a, pb_s, la_b, og_s, st_s, flag_s = rest[2 * ncast + 1:]
    _run_casts(rest[:ncast], rest[ncast + 1:2 * ncast + 1])
    C = CHUNK
    step = pl.program_id(0)
    wts = (g_ref, win_ref, wg2_ref, bg_ref)

    @pl.when(step == 0)
    def _():
        for task in _gla_project(xc_ref, 0, tm, *wts, pa_s, la_a, flag_s, 0):
            task()

    @pl.when((2 * step) % tiles_per_seq == 0)
    def _():
        st_s[...] = jnp.zeros_like(st_s)

    ri = lax.broadcasted_iota(jnp.int32, (C, C), 0)
    ci = lax.broadcasted_iota(jnp.int32, (C, C), 1)
    tri = (ri >= ci).astype(BF16)
    tri2 = jnp.concatenate([tri, tri], axis=1)
    halves = (32, 16, 8, 4, 2, 1)
    lvl_mask = [((ri // (2 * hf)) == (ci // (2 * hf))) & ((ri % (2 * hf)) >= hf) & ((ci % (2 * hf)) < hf)
                for hf in halves]
    diag = ri == ci
    rowc = lax.broadcasted_iota(jnp.int32, (C, kd // GLA_HEADS), 0)
    upper = [(rowc % (2 * hf)) >= hf for hf in halves]
    consts = (tri2, halves, lvl_mask, diag, upper, ri >= ci)

    def tile(proj_ref, la_ref, og_ref, flag_idx, nxt_x, nxt_row0, nxt_proj, nxt_la, nxt_flag):
        def run(fast):
            def body():
                nxt = _gla_project(nxt_x, nxt_row0, tm, *wts, nxt_proj, nxt_la, flag_s, nxt_flag)
                _gla_chunks(proj_ref, la_ref, og_ref, st_s, gn_ref, consts, nxt, tm=tm, kd=kd, vd=vd, fast=fast)
            return body
        mild = flag_s[flag_idx] > -GLA_FAST_RANGE
        pl.when(mild)(run(True))
        pl.when(jnp.logical_not(mild))(run(False))

    tile(pa_s, la_a, og_s.at[pl.ds(0, tm)], 0, xc_ref, tm, pb_s, la_b, 1)
    tile(pb_s, la_b, og_s.at[pl.ds(tm, tm)], 1, xn_ref, 0, pa_s, la_a, 0)
    o_ref[...] = xc_ref[...] + _dot(og_s[...], wout_ref[...])


def _gla(x2, g, win, wg2, bg, gn, wout, *, tm, seq_len, cast=()):
    m, d = x2.shape
    kd = wg2.shape[1]
    vd = wout.shape[0]
    dk = kd // GLA_HEADS
    dv = vd // GLA_HEADS
    nproj = win.shape[1] - wg2.shape[0]
    ntiles = m // tm
    assert seq_len % (2 * tm) == 0 and nproj % GLA_PROJ_PIECE[1] == 0 and tm % GLA_PROJ_PIECE[0] == 0
    c_in, c_out, c_shapes = _cast_through(cast, ntiles // 2)
    return pl.pallas_call(
        functools.partial(_gla_kernel, tm=tm, kd=kd, vd=vd, tiles_per_seq=seq_len // tm, ncast=len(cast)),
        grid=(ntiles // 2,),
        in_specs=[
            pl.BlockSpec((2 * tm, d), lambda i: (i, 0)),
            pl.BlockSpec((tm, d), lambda i: (jnp.minimum(2 * i + 2, ntiles - 1), 0)),
            _resident((1, d)),
            _resident(win.shape),
            _resident(wg2.shape),
            _resident((1, kd)),
            _resident(gn.shape),
            _resident((vd, d)),
        ] + c_in,
        out_specs=[pl.BlockSpec((2 * tm, d), lambda i: (i, 0))] + c_out,
        out_shape=[jax.ShapeDtypeStruct((m, d), F32)] + c_shapes,
        scratch_shapes=[
            pltpu.VMEM((tm, nproj), F32),
            pltpu.VMEM((tm, kd), F32),
            pltpu.VMEM((tm, nproj), F32),
            pltpu.VMEM((tm, kd), F32),
            pltpu.VMEM((2 * tm, vd), BF16),
            pltpu.VMEM((GLA_HEADS, dk, dv), F32),
            pltpu.SMEM((2,), F32),
        ],
        compiler_params=pltpu.CompilerParams(
            dimension_semantics=("arbitrary",), vmem_limit_bytes=VMEM_LIMIT_BYTES),
        name="gla",
    )(x2, x2, g, win, wg2, bg, gn, wout, *[w for w, _ in cast])


def kernel(x, mix_norm, a_w_in, a_ln_g, a_ln_b, a_w_s, a_b_s, a_w_out, b_w_in, b_w_g2, b_b_g, b_gn_g, b_w_out, ffn_norm, ffn_w_gate, ffn_w_up, ffn_w_down, final_norm):
    b, t, d = x.shape
    depth = mix_norm.shape[0]
    assert depth == 2 and a_w_in.shape[0] == 1 and b_w_in.shape[0] == 1
    m = b * t
    bf = lambda w: w.astype(BF16)
    row = lambda v: v.reshape(1, -1)

    x2 = x.reshape(m, d)
    x2, wg0, wu0, wd0 = _sgu(
        x2, row(mix_norm[0]), bf(a_w_in[0]), row(a_ln_g[0]), row(a_ln_b[0]),
        a_w_s[0], a_b_s[0].T, bf(a_w_out[0]), tm=SGU_TM,
        cast=((ffn_w_gate, 0), (ffn_w_up, 0), (ffn_w_down, 0)))
    x2, b_wout = _ffn(
        x2, row(ffn_norm[0]), wg0, wu0, wd0, row(final_norm), final_norm=False, tm=FFN_TM,
        cast=((b_w_out, 0),))
    x2, wg1, wu1, wd1 = _gla(
        x2, row(mix_norm[1]), bf(b_w_in[0]), b_w_g2[0], row(b_b_g[0]), b_gn_g[0], b_wout, tm=GLA_TM, seq_len=t,
        cast=((ffn_w_gate, 1), (ffn_w_up, 1), (ffn_w_down, 1)))
    (x2,) = _ffn(x2, row(ffn_norm[1]), wg1, wu1, wd1, row(final_norm), final_norm=True, tm=FFN_TM)
    return x2.reshape(b, t, d)
```

```python
import functools

import jax
import jax.numpy as jnp
from jax import lax
from jax.experimental import pallas as pl
from jax.experimental.pallas import tpu as pltpu

F32 = jnp.float32
BF16 = jnp.bfloat16

EPS = 1e-6
CHUNK = 64
SG_BLOCK = 128
SG_GROUPS = 8
SG_GROUPS_PER_CHUNK = 2
GLA_HEADS = 4
GLA_TAU = 16.0
GLA_FAST_RANGE = 60.0
GLA_SKEW = 1
GLA_PROJ_PIECE = (256, 512)

BF16_SUBLANES = 16
MXU_N = 256
FFN_HIDDEN_CHUNKS = 4

SGU_TM = 512
FFN_TM = 1024
GLA_TM = 256

VMEM_LIMIT_BYTES = 56 * 1024 * 1024


def _rms(x, g):
    return x * lax.rsqrt(jnp.mean(x * x, axis=-1, keepdims=True) + EPS) * g


def _gelu(x):
    return 0.5 * x * (1.0 + lax.erf(x * (2.0 ** -0.5)))


def _dot(a, b):
    return jnp.dot(a, b, preferred_element_type=F32)


def _dot_nt(a, b):
    return lax.dot_general(a, b, (((1,), (1,)), ((), ())), preferred_element_type=F32)


def _dot_tn(a, b):
    return lax.dot_general(a, b, (((0,), (0,)), ((), ())), preferred_element_type=F32)


def _resident(shape):
    return pl.BlockSpec(shape, lambda *_: (0,) * len(shape))


def _cast_through(weights, nsteps):
    in_specs, out_specs, out_shapes = [], [], []
    for w, layer in weights:
        _, r, c = w.shape
        per = 1 if r % (nsteps * BF16_SUBLANES) == 0 else 2
        rb = r * per // nsteps
        assert rb % BF16_SUBLANES == 0 and rb * nsteps == r * per, (r, c, nsteps)
        in_specs.append(pl.BlockSpec((None, rb, c), lambda i, layer=layer, per=per: (layer, i // per, 0)))
        out_specs.append(pl.BlockSpec((rb, c), lambda i, per=per: (i // per, 0)))
        out_shapes.append(jax.ShapeDtypeStruct((r, c), BF16))
    return in_specs, out_specs, out_shapes


def _run_casts(src_refs, dst_refs):
    for src, dst in zip(src_refs, dst_refs):
        dst[...] = src[...].astype(BF16)


def _ffn_kernel(x_ref, g_ref, wg_ref, wu_ref, wd_ref, fin_ref, *rest, final_norm, ncast):
    o_ref = rest[ncast]
    _run_casts(rest[:ncast], rest[ncast + 1:])
    x = x_ref[...]
    h = _rms(x, g_ref[...]).astype(BF16)
    hid = wg_ref.shape[1]
    ntiles = hid // MXU_N
    bounds = [MXU_N * (ntiles * j // FFN_HIDDEN_CHUNKS) for j in range(FFN_HIDDEN_CHUNKS + 1)]
    y = x
    for c0, c1 in zip(bounds[:-1], bounds[1:]):
        gate = _dot(h, wg_ref[:, c0:c1])
        up = _dot(h, wu_ref[:, c0:c1])
        a = (jax.nn.silu(gate) * up).astype(BF16)
        y = y + _dot(a, wd_ref[c0:c1, :])
    if final_norm:
        y = _rms(y, fin_ref[...])
    o_ref[...] = y


def _ffn(x2, g, wg, wu, wd, fin, *, final_norm, tm, cast=()):
    m, d = x2.shape
    hid = wg.shape[1]
    c_in, c_out, c_shapes = _cast_through(cast, m // tm)
    return pl.pallas_call(
        functools.partial(_ffn_kernel, final_norm=final_norm, ncast=len(cast)),
        grid=(m // tm,),
        in_specs=[
            pl.BlockSpec((tm, d), lambda i: (i, 0)),
            _resident((1, d)),
            _resident((d, hid)),
            _resident((d, hid)),
            _resident((hid, d)),
            _resident((1, d)),
        ] + c_in,
        out_specs=[pl.BlockSpec((tm, d), lambda i: (i, 0))] + c_out,
        out_shape=[jax.ShapeDtypeStruct((m, d), F32)] + c_shapes,
        compiler_params=pltpu.CompilerParams(
            dimension_semantics=("arbitrary",), vmem_limit_bytes=VMEM_LIMIT_BYTES),
        name="swiglu",
    )(x2, g, wg, wu, wd, fin, *[w for w, _ in cast])


def _sgu_kernel(x_ref, g_ref, win_ref, lng_ref, lnb_ref, ws_ref, bst_ref, wout_ref,
                *rest, tm, ncast):
    o_ref = rest[ncast]
    v_s, y_s = rest[2 * ncast + 1:]
    _run_casts(rest[:ncast], rest[ncast + 1:2 * ncast + 1])
    width = v_s.shape[1]
    gc = width // SG_GROUPS
    lanes = 128
    x = x_ref[...]
    h = _rms(x, g_ref[...]).astype(BF16)

    gpc = SG_GROUPS_PER_CHUNK
    cw = gpc * gc
    p1 = jnp.zeros((tm, lanes), F32)
    p2 = jnp.zeros((tm, lanes), F32)
    for c in range(SG_GROUPS // gpc):
        v = _gelu(_dot(h, win_ref[:, width + c * cw:width + (c + 1) * cw]))
        v_s[:, c * cw:(c + 1) * cw] = v
        for l in range(cw // lanes):
            vl = v[:, l * lanes:(l + 1) * lanes]
            p1 = p1 + vl
            p2 = p2 + vl * vl
    mu = jnp.sum(p1, axis=-1, keepdims=True) * (1.0 / width)
    var = jnp.sum(p2, axis=-1, keepdims=True) * (1.0 / width) - mu * mu
    rstd = lax.rsqrt(var + EPS)
    shift = -mu * rstd

    ti = lax.broadcasted_iota(jnp.int32, (SG_BLOCK, SG_BLOCK), 0) // CHUNK
    si = lax.broadcasted_iota(jnp.int32, (SG_BLOCK, SG_BLOCK), 1) // CHUNK
    causal = ti >= si
    for c in range(SG_GROUPS // gpc):
        ccols = slice(c * cw, (c + 1) * cw)
        u = _gelu(_dot(h, win_ref[:, ccols]))
        vn = ((v_s[:, ccols] * rstd + shift) * lng_ref[:, ccols] + lnb_ref[:, ccols]).astype(BF16)
        for gg in range(gpc):
            g = c * gpc + gg
            gcols = slice(gg * gc, (gg + 1) * gc)
            ws = jnp.where(causal, ws_ref[g], 0.0).astype(BF16)
            bias = bst_ref[:, g:g + 1]
            for w in range(tm // SG_BLOCK):
                rows = slice(w * SG_BLOCK, (w + 1) * SG_BLOCK)
                s = _dot(ws, vn[rows, gcols]) + bias
                y_s[rows, g * gc:(g + 1) * gc] = (u[rows, gcols] * s).astype(BF16)
    o_ref[...] = x + _dot(y_s[...], wout_ref[...])


def _sgu(x2, g, win, lng, lnb, ws, bst, wout, *, tm, cast=()):
    m, d = x2.shape
    width = wout.shape[0]
    c_in, c_out, c_shapes = _cast_through(cast, m // tm)
    return pl.pallas_call(
        functools.partial(_sgu_kernel, tm=tm, ncast=len(cast)),
        grid=(m // tm,),
        in_specs=[
            pl.BlockSpec((tm, d), lambda i: (i, 0)),
            _resident((1, d)),
            _resident((d, 2 * width)),
            _resident((1, width)),
            _resident((1, width)),
            _resident(ws.shape),
            _resident(bst.shape),
            _resident((width, d)),
        ] + c_in,
        out_specs=[pl.BlockSpec((tm, d), lambda i: (i, 0))] + c_out,
        out_shape=[jax.ShapeDtypeStruct((m, d), F32)] + c_shapes,
        scratch_shapes=[
            pltpu.VMEM((tm, width), F32),
            pltpu.VMEM((tm, width), BF16),
        ],
        compiler_params=pltpu.CompilerParams(
            dimension_semantics=("arbitrary",), vmem_limit_bytes=VMEM_LIMIT_BYTES),
        name="sgu",
    )(x2, g, win, lng, lnb, ws, bst, wout, *[w for w, _ in cast])


def _ref_rows(G, half):
    n, w = G.shape
    blk = 2 * half
    if blk >= 8:
        return jnp.concatenate(
            [jnp.broadcast_to(G[b * blk + half - 1:b * blk + half, :], (blk, w))
             for b in range(n // blk)], axis=0)
    assert blk == 4
    sub = lax.broadcasted_iota(jnp.int32, (8, w), 0)
    tiles = []
    for t in range(n // 8):
        lo = jnp.broadcast_to(G[8 * t + 1:8 * t + 2, :], (8, w))
        hi = jnp.broadcast_to(G[8 * t + 5:8 * t + 6, :], (8, w))
        tiles.append(jnp.where(sub < 4, lo, hi))
    return jnp.concatenate(tiles, axis=0)


def _gla_project(x_ref, row0, tm, g_ref, win_ref, wg2_ref, bg_ref, proj_ref, la_ref, flag_ref, flag_idx):
    state = {}
    ncol = proj_ref.shape[1]
    pr, pc = GLA_PROJ_PIECE
    tasks = []

    def norm(r):
        def run():
            x = x_ref[row0 + r * pr:row0 + (r + 1) * pr, :]
            state[r] = _rms(x, g_ref[...]).astype(BF16)
            if r == 0:
                flag_ref[flag_idx] = jnp.float32(0.0)
        return run

    def cols(r, p):
        def run():
            proj_ref[r * pr:(r + 1) * pr, p * pc:(p + 1) * pc] = _dot(state[r], win_ref[:, p * pc:(p + 1) * pc])
        return run

    def gate(r):
        def run():
            gz = _dot(state[r], win_ref[:, ncol:])
            z = _dot(gz.astype(BF16), wg2_ref[...].astype(BF16)) + bg_ref[...]
            la = (jnp.minimum(z, 0.0) - jnp.log1p(jnp.exp(-jnp.abs(z)))) / GLA_TAU
            la_ref[r * pr:(r + 1) * pr, :] = la
            tot = [jnp.sum(la[c * CHUNK:(c + 1) * CHUNK, :], axis=0, keepdims=True) for c in range(pr // CHUNK)]
            flag_ref[flag_idx] = jnp.minimum(flag_ref[flag_idx], jnp.min(functools.reduce(jnp.minimum, tot)))
        return run

    for r in range(tm // pr):
        tasks += [norm(r), gate(r)] + [cols(r, p) for p in range(ncol // pc)]
    return tasks


def _gla_chunks(proj_ref, la_ref, og_s, st_s, gn_ref, consts, between, *, tm, kd, vd, fast):
    dk = kd // GLA_HEADS
    dv = vd // GLA_HEADS
    C = CHUNK
    tri2, halves, lvl_mask, diag, upper, causal = consts
    scale = dk ** -0.5
    between = list(between)
    span = 2 if fast else 1
    B = span * C
    nblocks = tm // B
    nslots = nblocks * GLA_HEADS

    def chunk_decay(c):
        la_all = la_ref[c * C:(c + 1) * C, :]
        la_hi = la_all.astype(BF16)
        la_lo = (la_all - la_hi.astype(F32)).astype(BF16)
        return la_all, _dot(tri2, jnp.concatenate([la_hi, la_lo], axis=0))

    def scores(b, hd, decays):
        rows = slice(b * B, (b + 1) * B)
        ks_ = slice(hd * dk, (hd + 1) * dk)
        q = proj_ref[rows, hd * dk:(hd + 1) * dk] * scale
        k = proj_ref[rows, kd + hd * dk:kd + (hd + 1) * dk]
        if fast:
            (_, ga0), (_, ga1) = decays
            g0, g1 = ga0[:, ks_], ga1[:, ks_]
            e0, e1 = g0[C - 1:C, :], g1[C - 1:C, :]
            q0, q1, k0, k1 = q[:C], q[C:], k[:C], k[C:]
            k0r = k0 * jnp.exp(e0 - g0)
            x1 = jnp.exp(g1)
            qd = jnp.concatenate([q0 * jnp.exp(g0), q1 * (x1 * jnp.exp(e0))], axis=0)
            qs = jnp.concatenate([q0 * jnp.exp(g0 - e0), q1 * x1], axis=0)
            kr = jnp.concatenate([k0r, k1 * jnp.exp(-g1)], axis=0)
            kdec = jnp.concatenate([k0r * jnp.exp(e1), k1 * jnp.exp(e1 - g1)], axis=0)
            return ([_dot_nt(qs.astype(BF16), kr.astype(BF16))], qd.astype(BF16), kdec.astype(BF16),
                    jnp.exp(e0 + e1))
        (la_all, g_all), = decays
        la = la_all[:, ks_]
        G = g_all[:, ks_]
        g_last = G[C - 1:C, :]
        qd = (q * jnp.exp(G)).astype(BF16)
        kdec = (k * jnp.exp(g_last - G)).astype(BF16)
        dec = jnp.exp(g_last)
        ps = [jnp.sum(q * k, axis=-1, keepdims=True)]
        for lvl, hf in enumerate(halves):
            if hf == 1:
                e = jnp.exp(jnp.where(upper[lvl], la, 0.0))
            else:
                e = jnp.exp(-jnp.abs(G - _ref_rows(G, hf)))
            xl = (jnp.where(upper[lvl], q, k) * e).astype(BF16)
            ps.append(_dot_nt(xl, xl))
        return ps, qd, kdec, dec

    def outputs(b, hd, ps, qd, kdec, dec):
        rows = slice(b * B, (b + 1) * B)
        vs_ = slice(2 * kd + hd * dv, 2 * kd + (hd + 1) * dv)
        rs_ = slice(2 * kd + vd + hd * dv, 2 * kd + vd + (hd + 1) * dv)
        if fast:
            a = jnp.where(causal, ps[0], 0.0)
        else:
            a = jnp.where(diag, ps[0], 0.0)
            for lvl in range(len(halves)):
                a = jnp.where(lvl_mask[lvl], ps[lvl + 1], a)
        v = proj_ref[rows, vs_].astype(BF16)
        st = st_s[hd]
        lhs = jnp.concatenate([qd, a.astype(BF16)], axis=1)
        rhs = jnp.concatenate([st.astype(BF16), v], axis=0)
        o = _dot(lhs, rhs)
        dcol = jnp.transpose(jnp.broadcast_to(dec, (dk, dk)))
        st_s[hd] = st * jnp.concatenate([dcol] * (dv // dk), axis=1) + _dot_tn(kdec, v)
        o = o * lax.rsqrt(jnp.mean(o * o, axis=-1, keepdims=True) + EPS) * gn_ref[hd:hd + 1, :]
        r = proj_ref[rows, rs_]
        og_s[rows, hd * dv:(hd + 1) * dv] = (jax.nn.silu(r) * o).astype(BF16)

    block_decay = lambda b: [chunk_decay(b * span + j) for j in range(span)]
    decay = {0: block_decay(0)}
    pending = []
    for slot in range(nslots):
        b, hd = divmod(slot, GLA_HEADS)
        if hd == 0 and b + 1 < nblocks:
            decay[b + 1] = block_decay(b + 1)
        pending.append((b, hd) + scores(b, hd, decay[b]))
        for task in between[slot * len(between) // nslots:(slot + 1) * len(between) // nslots]:
            task()
        if len(pending) > GLA_SKEW:
            outputs(*pending.pop(0))
    for item in pending:
        outputs(*item)


def _gla_kernel(xc_ref, xn_ref, g_ref, win_ref, wg2_ref, bg_ref, gn_ref, wout_ref,
                *rest, tm, kd, vd, tiles_per_seq, ncast):
    o_ref = rest[ncast]
    pa_s, la_a, pb_s, la_b, og_s, st_s, flag_s = rest[2 * ncast + 1:]
    _run_casts(rest[:ncast], rest[ncast + 1:2 * ncast + 1])
    C = CHUNK
    step = pl.program_id(0)
    wts = (g_ref, win_ref, wg2_ref, bg_ref)

    @pl.when(step == 0)
    def _():
        for task in _gla_project(xc_ref, 0, tm, *wts, pa_s, la_a, flag_s, 0):
            task()

    @pl.when((2 * step) % tiles_per_seq == 0)
    def _():
        st_s[...] = jnp.zeros_like(st_s)

    ri = lax.broadcasted_iota(jnp.int32, (C, C), 0)
    ci = lax.broadcasted_iota(jnp.int32, (C, C), 1)
    tri = (ri >= ci).astype(BF16)
    tri2 = jnp.concatenate([tri, tri], axis=1)
    halves = (32, 16, 8, 4, 2, 1)
    lvl_mask = [((ri // (2 * hf)) == (ci // (2 * hf))) & ((ri % (2 * hf)) >= hf) & ((ci % (2 * hf)) < hf)
                for hf in halves]
    diag = ri == ci
    rowc = lax.broadcasted_iota(jnp.int32, (C, kd // GLA_HEADS), 0)
    upper = [(rowc % (2 * hf)) >= hf for hf in halves]
    bi = lax.broadcasted_iota(jnp.int32, (2 * C, 2 * C), 0)
    bj = lax.broadcasted_iota(jnp.int32, (2 * C, 2 * C), 1)
    consts = (tri2, halves, lvl_mask, diag, upper, bi >= bj)

    def tile(proj_ref, la_ref, og_ref, flag_idx, nxt_x, nxt_row0, nxt_proj, nxt_la, nxt_flag):
        def run(fast):
            def body():
                nxt = _gla_project(nxt_x, nxt_row0, tm, *wts, nxt_proj, nxt_la, flag_s, nxt_flag)
                _gla_chunks(proj_ref, la_ref, og_ref, st_s, gn_ref, consts, nxt, tm=tm, kd=kd, vd=vd, fast=fast)
            return body
        mild = flag_s[flag_idx] > -GLA_FAST_RANGE
        pl.when(mild)(run(True))
        pl.when(jnp.logical_not(mild))(run(False))

    tile(pa_s, la_a, og_s.at[pl.ds(0, tm)], 0, xc_ref, tm, pb_s, la_b, 1)
    tile(pb_s, la_b, og_s.at[pl.ds(tm, tm)], 1, xn_ref, 0, pa_s, la_a, 0)
    o_ref[...] = xc_ref[...] + _dot(og_s[...], wout_ref[...])


def _gla(x2, g, win, wg2, bg, gn, wout, *, tm, seq_len, cast=()):
    m, d = x2.shape
    kd = wg2.shape[1]
    vd = wout.shape[0]
    dk = kd // GLA_HEADS
    dv = vd // GLA_HEADS
    nproj = win.shape[1] - wg2.shape[0]
    ntiles = m // tm
    assert seq_len % (2 * tm) == 0 and nproj % GLA_PROJ_PIECE[1] == 0 and tm % GLA_PROJ_PIECE[0] == 0
    c_in, c_out, c_shapes = _cast_through(cast, ntiles // 2)
    return pl.pallas_call(
        functools.partial(_gla_kernel, tm=tm, kd=kd, vd=vd, tiles_per_seq=seq_len // tm, ncast=len(cast)),
        grid=(ntiles // 2,),
        in_specs=[
            pl.BlockSpec((2 * tm, d), lambda i: (i, 0)),
            pl.BlockSpec((tm, d), lambda i: (jnp.minimum(2 * i + 2, ntiles - 1), 0)),
            _resident((1, d)),
            _resident(win.shape),
            _resident(wg2.shape),
            _resident((1, kd)),
            _resident(gn.shape),
            _resident((vd, d)),
        ] + c_in,
        out_specs=[pl.BlockSpec((2 * tm, d), lambda i: (i, 0))] + c_out,
        out_shape=[jax.ShapeDtypeStruct((m, d), F32)] + c_shapes,
        scratch_shapes=[
            pltpu.VMEM((tm, nproj), F32),
            pltpu.VMEM((tm, kd), F32),
            pltpu.VMEM((tm, nproj), F32),
            pltpu.VMEM((tm, kd), F32),
            pltpu.VMEM((2 * tm, vd), BF16),
            pltpu.VMEM((GLA_HEADS, dk, dv), F32),
            pltpu.SMEM((2,), F32),
        ],
        compiler_params=pltpu.CompilerParams(
            dimension_semantics=("arbitrary",), vmem_limit_bytes=VMEM_LIMIT_BYTES),
        name="gla",
    )(x2, x2, g, win, wg2, bg, gn, wout, *[w for w, _ in cast])


def kernel(x, mix_norm, a_w_in, a_ln_g, a_ln_b, a_w_s, a_b_s, a_w_out, b_w_in, b_w_g2, b_b_g, b_gn_g, b_w_out, ffn_norm, ffn_w_gate, ffn_w_up, ffn_w_down, final_norm):
    b, t, d = x.shape
    depth = mix_norm.shape[0]
    assert depth == 2 and a_w_in.shape[0] == 1 and b_w_in.shape[0] == 1
    m = b * t
    bf = lambda w: w.astype(BF16)
    row = lambda v: v.reshape(1, -1)

    x2 = x.reshape(m, d)
    x2, wg0, wu0, wd0 = _sgu(
        x2, row(mix_norm[0]), bf(a_w_in[0]), row(a_ln_g[0]), row(a_ln_b[0]),
        a_w_s[0], a_b_s[0].T, bf(a_w_out[0]), tm=SGU_TM,
        cast=((ffn_w_gate, 0), (ffn_w_up, 0), (ffn_w_down, 0)))
    x2, b_wout = _ffn(
        x2, row(ffn_norm[0]), wg0, wu0, wd0, row(final_norm), final_norm=False, tm=FFN_TM,
        cast=((b_w_out, 0),))
    x2, wg1, wu1, wd1 = _gla(
        x2, row(mix_norm[1]), bf(b_w_in[0]), b_w_g2[0], row(b_b_g[0]), b_gn_g[0], b_wout, tm=GLA_TM, seq_len=t,
        cast=((ffn_w_gate, 1), (ffn_w_up, 1), (ffn_w_down, 1)))
    (x2,) = _ffn(x2, row(ffn_norm[1]), wg1, wu1, wd1, row(final_norm), final_norm=True, tm=FFN_TM)
    return x2.reshape(b, t, d)
```

```python
import functools

import jax
import jax.numpy as jnp
from jax import lax
from jax.experimental import pallas as pl
from jax.experimental.pallas import tpu as pltpu

F32 = jnp.float32
BF16 = jnp.bfloat16

EPS = 1e-6
CHUNK = 64
SG_BLOCK = 128
SG_GROUPS = 8
SG_GROUPS_PER_CHUNK = 2
GLA_HEADS = 4
GLA_TAU = 16.0
GLA_FAST_RANGE = 60.0
GLA_SKEW = 1
GLA_PROJ_PIECE = (256, 512)

BF16_SUBLANES = 16
MXU_N = 256
FFN_HIDDEN_CHUNKS = 4

SGU_TM = 512
SGU_STAGE_ROWS = (64, 384)
FFN_TM = 1024
GLA_TM = 256

VMEM_LIMIT_BYTES = 56 * 1024 * 1024


def _rms(x, g):
    return x * lax.rsqrt(jnp.mean(x * x, axis=-1, keepdims=True) + EPS) * g


def _gelu(x):
    return 0.5 * x * (1.0 + lax.erf(x * (2.0 ** -0.5)))


def _dot(a, b):
    return jnp.dot(a, b, preferred_element_type=F32)


def _dot_nt(a, b):
    return lax.dot_general(a, b, (((1,), (1,)), ((), ())), preferred_element_type=F32)


def _dot_tn(a, b):
    return lax.dot_general(a, b, (((0,), (0,)), ((), ())), preferred_element_type=F32)


def _resident(shape):
    return pl.BlockSpec(shape, lambda *_: (0,) * len(shape))


def _cast_through(weights, nsteps):
    in_specs, out_specs, out_shapes = [], [], []
    for w, layer in weights:
        _, r, c = w.shape
        per = 1 if r % (nsteps * BF16_SUBLANES) == 0 else 2
        rb = r * per // nsteps
        assert rb % BF16_SUBLANES == 0 and rb * nsteps == r * per, (r, c, nsteps)
        in_specs.append(pl.BlockSpec((None, rb, c), lambda i, layer=layer, per=per: (layer, i // per, 0)))
        out_specs.append(pl.BlockSpec((rb, c), lambda i, per=per: (i // per, 0)))
        out_shapes.append(jax.ShapeDtypeStruct((r, c), BF16))
    return in_specs, out_specs, out_shapes


def _run_casts(src_refs, dst_refs):
    for src, dst in zip(src_refs, dst_refs):
        dst[...] = src[...].astype(BF16)


def _ffn_kernel(x_ref, g_ref, wg_ref, wu_ref, wd_ref, fin_ref, *rest, final_norm, ncast):
    o_ref = rest[ncast]
    _run_casts(rest[:ncast], rest[ncast + 1:])
    x = x_ref[...]
    h = _rms(x, g_ref[...]).astype(BF16)
    hid = wg_ref.shape[1]
    ntiles = hid // MXU_N
    bounds = [MXU_N * (ntiles * j // FFN_HIDDEN_CHUNKS) for j in range(FFN_HIDDEN_CHUNKS + 1)]
    y = x
    for c0, c1 in zip(bounds[:-1], bounds[1:]):
        gate = _dot(h, wg_ref[:, c0:c1])
        up = _dot(h, wu_ref[:, c0:c1])
        a = (jax.nn.silu(gate) * up).astype(BF16)
        y = y + _dot(a, wd_ref[c0:c1, :])
    if final_norm:
        y = _rms(y, fin_ref[...])
    o_ref[...] = y


def _ffn(x2, g, wg, wu, wd, fin, *, final_norm, tm, cast=()):
    m, d = x2.shape
    hid = wg.shape[1]
    c_in, c_out, c_shapes = _cast_through(cast, m // tm)
    return pl.pallas_call(
        functools.partial(_ffn_kernel, final_norm=final_norm, ncast=len(cast)),
        grid=(m // tm,),
        in_specs=[
            pl.BlockSpec((tm, d), lambda i: (i, 0)),
            _resident((1, d)),
            _resident((d, hid)),
            _resident((d, hid)),
            _resident((hid, d)),
            _resident((1, d)),
        ] + c_in,
        out_specs=[pl.BlockSpec((tm, d), lambda i: (i, 0))] + c_out,
        out_shape=[jax.ShapeDtypeStruct((m, d), F32)] + c_shapes,
        compiler_params=pltpu.CompilerParams(
            dimension_semantics=("arbitrary",), vmem_limit_bytes=VMEM_LIMIT_BYTES),
        name="swiglu",
    )(x2, g, wg, wu, wd, fin, *[w for w, _ in cast])


def _load_cast(w_hbm, w_s, stage_s, sem):
    rb = stage_s.shape[1]
    n = w_hbm.shape[0] // rb

    def copy(i, slot):
        return pltpu.make_async_copy(w_hbm.at[pl.ds(i * rb, rb)], stage_s.at[slot], sem.at[slot])

    copy(0, 0).start()

    def body(i, carry):
        slot = i % 2

        @pl.when(i + 1 < n)
        def _():
            copy(i + 1, 1 - slot).start()

        copy(i, slot).wait()
        w_s[pl.ds(pl.multiple_of(i * rb, rb), rb), :] = stage_s[slot].astype(BF16)
        return carry

    lax.fori_loop(0, n, body, 0)


def _sgu_kernel(x_ref, g_ref, win_hbm, lng_ref, lnb_ref, ws_ref, bst_ref, wout_hbm,
                *rest, tm, ncast):
    o_ref = rest[ncast]
    v_s, y_s, win_ref, wout_ref, stage_in, stage_out, sem = rest[2 * ncast + 1:]
    _run_casts(rest[:ncast], rest[ncast + 1:2 * ncast + 1])

    @pl.when(pl.program_id(0) == 0)
    def _():
        _load_cast(win_hbm.at[0], win_ref, stage_in, sem)
        _load_cast(wout_hbm.at[0], wout_ref, stage_out, sem)

    width = v_s.shape[1]
    gc = width // SG_GROUPS
    lanes = 128
    x = x_ref[...]
    h = _rms(x, g_ref[...]).astype(BF16)

    gpc = SG_GROUPS_PER_CHUNK
    cw = gpc * gc
    p1 = jnp.zeros((tm, lanes), F32)
    p2 = jnp.zeros((tm, lanes), F32)
    for c in range(SG_GROUPS // gpc):
        v = _gelu(_dot(h, win_ref[:, width + c * cw:width + (c + 1) * cw]))
        v_s[:, c * cw:(c + 1) * cw] = v
        for l in range(cw // lanes):
            vl = v[:, l * lanes:(l + 1) * lanes]
            p1 = p1 + vl
            p2 = p2 + vl * vl
    mu = jnp.sum(p1, axis=-1, keepdims=True) * (1.0 / width)
    var = jnp.sum(p2, axis=-1, keepdims=True) * (1.0 / width) - mu * mu
    rstd = lax.rsqrt(var + EPS)
    shift = -mu * rstd

    ti = lax.broadcasted_iota(jnp.int32, (SG_BLOCK, SG_BLOCK), 0) // CHUNK
    si = lax.broadcasted_iota(jnp.int32, (SG_BLOCK, SG_BLOCK), 1) // CHUNK
    causal = ti >= si
    for c in range(SG_GROUPS // gpc):
        ccols = slice(c * cw, (c + 1) * cw)
        u = _gelu(_dot(h, win_ref[:, ccols]))
        vn = ((v_s[:, ccols] * rstd + shift) * lng_ref[:, ccols] + lnb_ref[:, ccols]).astype(BF16)
        for gg in range(gpc):
            g = c * gpc + gg
            gcols = slice(gg * gc, (gg + 1) * gc)
            ws = jnp.where(causal, ws_ref[g], 0.0).astype(BF16)
            bias = bst_ref[:, g:g + 1]
            for w in range(tm // SG_BLOCK):
                rows = slice(w * SG_BLOCK, (w + 1) * SG_BLOCK)
                s = _dot(ws, vn[rows, gcols]) + bias
                y_s[rows, g * gc:(g + 1) * gc] = (u[rows, gcols] * s).astype(BF16)
    o_ref[...] = x + _dot(y_s[...], wout_ref[...])


def _sgu(x2, g, win, lng, lnb, ws, bst, wout, *, tm, cast=()):
    m, d = x2.shape
    width = wout.shape[1]
    assert d % SGU_STAGE_ROWS[0] == 0 and width % SGU_STAGE_ROWS[1] == 0
    c_in, c_out, c_shapes = _cast_through(cast, m // tm)
    return pl.pallas_call(
        functools.partial(_sgu_kernel, tm=tm, ncast=len(cast)),
        grid=(m // tm,),
        in_specs=[
            pl.BlockSpec((tm, d), lambda i: (i, 0)),
            _resident((1, d)),
            pl.BlockSpec(memory_space=pl.ANY),
            _resident((1, width)),
            _resident((1, width)),
            _resident(ws.shape),
            _resident(bst.shape),
            pl.BlockSpec(memory_space=pl.ANY),
        ] + c_in,
        out_specs=[pl.BlockSpec((tm, d), lambda i: (i, 0))] + c_out,
        out_shape=[jax.ShapeDtypeStruct((m, d), F32)] + c_shapes,
        scratch_shapes=[
            pltpu.VMEM((tm, width), F32),
            pltpu.VMEM((tm, width), BF16),
            pltpu.VMEM((d, 2 * width), BF16),
            pltpu.VMEM((width, d), BF16),
            pltpu.VMEM((2, SGU_STAGE_ROWS[0], 2 * width), F32),
            pltpu.VMEM((2, SGU_STAGE_ROWS[1], d), F32),
            pltpu.SemaphoreType.DMA((2,)),
        ],
        compiler_params=pltpu.CompilerParams(
            dimension_semantics=("arbitrary",), vmem_limit_bytes=VMEM_LIMIT_BYTES),
        name="sgu",
    )(x2, g, win, lng, lnb, ws, bst, wout, *[w for w, _ in cast])


def _ref_rows(G, half):
    n, w = G.shape
    blk = 2 * half
    if blk >= 8:
        return jnp.concatenate(
            [jnp.broadcast_to(G[b * blk + half - 1:b * blk + half, :], (blk, w))
             for b in range(n // blk)], axis=0)
    assert blk == 4
    sub = lax.broadcasted_iota(jnp.int32, (8, w), 0)
    tiles = []
    for t in range(n // 8):
        lo = jnp.broadcast_to(G[8 * t + 1:8 * t + 2, :], (8, w))
        hi = jnp.broadcast_to(G[8 * t + 5:8 * t + 6, :], (8, w))
        tiles.append(jnp.where(sub < 4, lo, hi))
    return jnp.concatenate(tiles, axis=0)


def _gla_project(x_ref, row0, tm, g_ref, win_ref, wg2_ref, bg_ref, proj_ref, la_ref, flag_ref, flag_idx):
    state = {}
    ncol = proj_ref.shape[1]
    pr, pc = GLA_PROJ_PIECE
    tasks = []

    def norm(r):
        def run():
            x = x_ref[row0 + r * pr:row0 + (r + 1) * pr, :]
            state[r] = _rms(x, g_ref[...]).astype(BF16)
            if r == 0:
                flag_ref[flag_idx] = jnp.float32(0.0)
        return run

    def cols(r, p):
        def run():
            proj_ref[r * pr:(r + 1) * pr, p * pc:(p + 1) * pc] = _dot(state[r], win_ref[:, p * pc:(p + 1) * pc])
        return run

    def gate(r):
        def run():
            gz = _dot(state[r], win_ref[:, ncol:])
            z = _dot(gz.astype(BF16), wg2_ref[...].astype(BF16)) + bg_ref[...]
            la = (jnp.minimum(z, 0.0) - jnp.log1p(jnp.exp(-jnp.abs(z)))) / GLA_TAU
            la_ref[r * pr:(r + 1) * pr, :] = la
            tot = [jnp.sum(la[c * CHUNK:(c + 1) * CHUNK, :], axis=0, keepdims=True) for c in range(pr // CHUNK)]
            flag_ref[flag_idx] = jnp.minimum(flag_ref[flag_idx], jnp.min(functools.reduce(jnp.minimum, tot)))
        return run

    for r in range(tm // pr):
        tasks += [norm(r), gate(r)] + [cols(r, p) for p in range(ncol // pc)]
    return tasks


def _gla_chunks(proj_ref, la_ref, og_s, st_s, gn_ref, consts, between, *, tm, kd, vd, fast):
    dk = kd // GLA_HEADS
    dv = vd // GLA_HEADS
    C = CHUNK
    tri2, halves, lvl_mask, diag, upper, causal = consts
    scale = dk ** -0.5
    between = list(between)
    span = 2 if fast else 1
    B = span * C
    nblocks = tm // B
    nslots = nblocks * GLA_HEADS

    def chunk_decay(c):
        la_all = la_ref[c * C:(c + 1) * C, :]
        la_hi = la_all.astype(BF16)
        la_lo = (la_all - la_hi.astype(F32)).astype(BF16)
        return la_all, _dot(tri2, jnp.concatenate([la_hi, la_lo], axis=0))

    def scores(b, hd, decays):
        rows = slice(b * B, (b + 1) * B)
        ks_ = slice(hd * dk, (hd + 1) * dk)
        q = proj_ref[rows, hd * dk:(hd + 1) * dk] * scale
        k = proj_ref[rows, kd + hd * dk:kd + (hd + 1) * dk]
        if fast:
            (_, ga0), (_, ga1) = decays
            g0, g1 = ga0[:, ks_], ga1[:, ks_]
            e0, e1 = g0[C - 1:C, :], g1[C - 1:C, :]
            q0, q1, k0, k1 = q[:C], q[C:], k[:C], k[C:]
            k0r = k0 * jnp.exp(e0 - g0)
            x1 = jnp.exp(g1)
            qd = jnp.concatenate([q0 * jnp.exp(g0), q1 * (x1 * jnp.exp(e0))], axis=0)
            qs = jnp.concatenate([q0 * jnp.exp(g0 - e0), q1 * x1], axis=0)
            kr = jnp.concatenate([k0r, k1 * jnp.exp(-g1)], axis=0)
            kdec = jnp.concatenate([k0r * jnp.exp(e1), k1 * jnp.exp(e1 - g1)], axis=0)
            return ([_dot_nt(qs.astype(BF16), kr.astype(BF16))], qd.astype(BF16), kdec.astype(BF16),
                    jnp.exp(e0 + e1))
        (la_all, g_all), = decays
        la = la_all[:, ks_]
        G = g_all[:, ks_]
        g_last = G[C - 1:C, :]
        qd = (q * jnp.exp(G)).astype(BF16)
        kdec = (k * jnp.exp(g_last - G)).astype(BF16)
        dec = jnp.exp(g_last)
        ps = [jnp.sum(q * k, axis=-1, keepdims=True)]
        for lvl, hf in enumerate(halves):
            if hf == 1:
                e = jnp.exp(jnp.where(upper[lvl], la, 0.0))
            else:
                e = jnp.exp(-jnp.abs(G - _ref_rows(G, hf)))
            xl = (jnp.where(upper[lvl], q, k) * e).astype(BF16)
            ps.append(_dot_nt(xl, xl))
        return ps, qd, kdec, dec

    def outputs(b, hd, ps, qd, kdec, dec):
        rows = slice(b * B, (b + 1) * B)
        vs_ = slice(2 * kd + hd * dv, 2 * kd + (hd + 1) * dv)
        rs_ = slice(2 * kd + vd + hd * dv, 2 * kd + vd + (hd + 1) * dv)
        if fast:
            a = jnp.where(causal, ps[0], 0.0)
        else:
            a = jnp.where(diag, ps[0], 0.0)
            for lvl in range(len(halves)):
                a = jnp.where(lvl_mask[lvl], ps[lvl + 1], a)
        v = proj_ref[rows, vs_].astype(BF16)
        st = st_s[hd]
        lhs = jnp.concatenate([qd, a.astype(BF16)], axis=1)
        rhs = jnp.concatenate([st.astype(BF16), v], axis=0)
        o = _dot(lhs, rhs)
        dcol = jnp.transpose(jnp.broadcast_to(dec, (dk, dk)))
        st_s[hd] = st * jnp.concatenate([dcol] * (dv // dk), axis=1) + _dot_tn(kdec, v)
        o = o * lax.rsqrt(jnp.mean(o * o, axis=-1, keepdims=True) + EPS) * gn_ref[hd:hd + 1, :]
        r = proj_ref[rows, rs_]
        og_s[rows, hd * dv:(hd + 1) * dv] = (jax.nn.silu(r) * o).astype(BF16)

    block_decay = lambda b: [chunk_decay(b * span + j) for j in range(span)]
    decay = {0: block_decay(0)}
    pending = []
    for slot in range(nslots):
        b, hd = divmod(slot, GLA_HEADS)
        if hd == 0 and b + 1 < nblocks:
            decay[b + 1] = block_decay(b + 1)
        pending.append((b, hd) + scores(b, hd, decay[b]))
        for task in between[slot * len(between) // nslots:(slot + 1) * len(between) // nslots]:
            task()
        if len(pending) > GLA_SKEW:
            outputs(*pending.pop(0))
    for item in pending:
        outputs(*item)


def _gla_kernel(xc_ref, xn_ref, g_ref, win_ref, wg2_ref, bg_ref, gn_ref, wout_ref,
                *rest, tm, kd, vd, tiles_per_seq, ncast):
    o_ref = rest[ncast]
    pa_s, la_a, pb_s, la_b, og_s, st_s, flag_s = rest[2 * ncast + 1:]
    _run_casts(rest[:ncast], rest[ncast + 1:2 * ncast + 1])
    C = CHUNK
    step = pl.program_id(0)
    wts = (g_ref, win_ref, wg2_ref, bg_ref)

    @pl.when(step == 0)
    def _():
        for task in _gla_project(xc_ref, 0, tm, *wts, pa_s, la_a, flag_s, 0):
            task()

    @pl.when((2 * step) % tiles_per_seq == 0)
    def _():
        st_s[...] = jnp.zeros_like(st_s)

    ri = lax.broadcasted_iota(jnp.int32, (C, C), 0)
    ci = lax.broadcasted_iota(jnp.int32, (C, C), 1)
    tri = (ri >= ci).astype(BF16)
    tri2 = jnp.concatenate([tri, tri], axis=1)
    halves = (32, 16, 8, 4, 2, 1)
    lvl_mask = [((ri // (2 * hf)) == (ci // (2 * hf))) & ((ri % (2 * hf)) >= hf) & ((ci % (2 * hf)) < hf)
                for hf in halves]
    diag = ri == ci
    rowc = lax.broadcasted_iota(jnp.int32, (C, kd // GLA_HEADS), 0)
    upper = [(rowc % (2 * hf)) >= hf for hf in halves]
    bi = lax.broadcasted_iota(jnp.int32, (2 * C, 2 * C), 0)
    bj = lax.broadcasted_iota(jnp.int32, (2 * C, 2 * C), 1)
    consts = (tri2, halves, lvl_mask, diag, upper, bi >= bj)

    def tile(proj_ref, la_ref, og_ref, flag_idx, nxt_x, nxt_row0, nxt_proj, nxt_la, nxt_flag):
        def run(fast):
            def body():
                nxt = _gla_project(nxt_x, nxt_row0, tm, *wts, nxt_proj, nxt_la, flag_s, nxt_flag)
                _gla_chunks(proj_ref, la_ref, og_ref, st_s, gn_ref, consts, nxt, tm=tm, kd=kd, vd=vd, fast=fast)
            return body
        mild = flag_s[flag_idx] > -GLA_FAST_RANGE
        pl.when(mild)(run(True))
        pl.when(jnp.logical_not(mild))(run(False))

    tile(pa_s, la_a, og_s.at[pl.ds(0, tm)], 0, xc_ref, tm, pb_s, la_b, 1)
    tile(pb_s, la_b, og_s.at[pl.ds(tm, tm)], 1, xn_ref, 0, pa_s, la_a, 0)
    o_ref[...] = xc_ref[...] + _dot(og_s[...], wout_ref[...])


def _gla(x2, g, win, wg2, bg, gn, wout, *, tm, seq_len, cast=()):
    m, d = x2.shape
    kd = wg2.shape[1]
    vd = wout.shape[0]
    dk = kd // GLA_HEADS
    dv = vd // GLA_HEADS
    nproj = win.shape[1] - wg2.shape[0]
    ntiles = m // tm
    assert seq_len % (2 * tm) == 0 and nproj % GLA_PROJ_PIECE[1] == 0 and tm % GLA_PROJ_PIECE[0] == 0
    c_in, c_out, c_shapes = _cast_through(cast, ntiles // 2)
    return pl.pallas_call(
        functools.partial(_gla_kernel, tm=tm, kd=kd, vd=vd, tiles_per_seq=seq_len // tm, ncast=len(cast)),
        grid=(ntiles // 2,),
        in_specs=[
            pl.BlockSpec((2 * tm, d), lambda i: (i, 0)),
            pl.BlockSpec((tm, d), lambda i: (jnp.minimum(2 * i + 2, ntiles - 1), 0)),
            _resident((1, d)),
            _resident(win.shape),
            _resident(wg2.shape),
            _resident((1, kd)),
            _resident(gn.shape),
            _resident((vd, d)),
        ] + c_in,
        out_specs=[pl.BlockSpec((2 * tm, d), lambda i: (i, 0))] + c_out,
        out_shape=[jax.ShapeDtypeStruct((m, d), F32)] + c_shapes,
        scratch_shapes=[
            pltpu.VMEM((tm, nproj), F32),
            pltpu.VMEM((tm, kd), F32),
            pltpu.VMEM((tm, nproj), F32),
            pltpu.VMEM((tm, kd), F32),
            pltpu.VMEM((2 * tm, vd), BF16),
            pltpu.VMEM((GLA_HEADS, dk, dv), F32),
            pltpu.SMEM((2,), F32),
        ],
        compiler_params=pltpu.CompilerParams(
            dimension_semantics=("arbitrary",), vmem_limit_bytes=VMEM_LIMIT_BYTES),
        name="gla",
    )(x2, x2, g, win, wg2, bg, gn, wout, *[w for w, _ in cast])


def kernel(x, mix_norm, a_w_in, a_ln_g, a_ln_b, a_w_s, a_b_s, a_w_out, b_w_in, b_w_g2, b_b_g, b_gn_g, b_w_out, ffn_norm, ffn_w_gate, ffn_w_up, ffn_w_down, final_norm):
    b, t, d = x.shape
    depth = mix_norm.shape[0]
    assert depth == 2 and a_w_in.shape[0] == 1 and b_w_in.shape[0] == 1
    m = b * t
    bf = lambda w: w.astype(BF16)
    row = lambda v: v.reshape(1, -1)

    x2 = x.reshape(m, d)
    x2, wg0, wu0, wd0 = _sgu(
        x2, row(mix_norm[0]), a_w_in, row(a_ln_g[0]), row(a_ln_b[0]),
        a_w_s[0], a_b_s[0].T, a_w_out, tm=SGU_TM,
        cast=((ffn_w_gate, 0), (ffn_w_up, 0), (ffn_w_down, 0)))
    x2, b_wout = _ffn(
        x2, row(ffn_norm[0]), wg0, wu0, wd0, row(final_norm), final_norm=False, tm=FFN_TM,
        cast=((b_w_out, 0),))
    x2, wg1, wu1, wd1 = _gla(
        x2, row(mix_norm[1]), bf(b_w_in[0]), b_w_g2[0], row(b_b_g[0]), b_gn_g[0], b_wout, tm=GLA_TM, seq_len=t,
        cast=((ffn_w_gate, 1), (ffn_w_up, 1), (ffn_w_down, 1)))
    (x2,) = _ffn(x2, row(ffn_norm[1]), wg1, wu1, wd1, row(final_norm), final_norm=True, tm=FFN_TM)
    return x2.reshape(b, t, d)
```

```python
import functools

import jax
import jax.numpy as jnp
from jax import lax
from jax.experimental import pallas as pl
from jax.experimental.pallas import tpu as pltpu

F32 = jnp.float32
BF16 = jnp.bfloat16

EPS = 1e-6
CHUNK = 64
SG_BLOCK = 128
SG_GROUPS = 8
SG_GROUPS_PER_CHUNK = 2
GLA_HEADS = 4
GLA_TAU = 16.0
GLA_FAST_RANGE = 60.0
GLA_SKEW = 1
GLA_PROJ_PIECE = (256, 512)

BF16_SUBLANES = 16
MXU_N = 256
FFN_HIDDEN_CHUNKS = 4

SGU_TM = 512
SGU_STAGE_ROWS = (32, 192)
SGU_STAGE_DEPTH = 4
FFN_TM = 1024
GLA_TM = 256

VMEM_LIMIT_BYTES = 56 * 1024 * 1024


def _rms(x, g):
    return x * lax.rsqrt(jnp.mean(x * x, axis=-1, keepdims=True) + EPS) * g


def _gelu(x):
    return 0.5 * x * (1.0 + lax.erf(x * (2.0 ** -0.5)))


def _dot(a, b):
    return jnp.dot(a, b, preferred_element_type=F32)


def _dot_nt(a, b):
    return lax.dot_general(a, b, (((1,), (1,)), ((), ())), preferred_element_type=F32)


def _dot_tn(a, b):
    return lax.dot_general(a, b, (((0,), (0,)), ((), ())), preferred_element_type=F32)


def _resident(shape):
    return pl.BlockSpec(shape, lambda *_: (0,) * len(shape))


def _cast_through(weights, nsteps):
    in_specs, out_specs, out_shapes = [], [], []
    for w, layer in weights:
        _, r, c = w.shape
        per = 1 if r % (nsteps * BF16_SUBLANES) == 0 else 2
        rb = r * per // nsteps
        assert rb % BF16_SUBLANES == 0 and rb * nsteps == r * per, (r, c, nsteps)
        in_specs.append(pl.BlockSpec((None, rb, c), lambda i, layer=layer, per=per: (layer, i // per, 0)))
        out_specs.append(pl.BlockSpec((rb, c), lambda i, per=per: (i // per, 0)))
        out_shapes.append(jax.ShapeDtypeStruct((r, c), BF16))
    return in_specs, out_specs, out_shapes


def _run_casts(src_refs, dst_refs):
    for src, dst in zip(src_refs, dst_refs):
        dst[...] = src[...].astype(BF16)


def _ffn_kernel(x_ref, g_ref, wg_ref, wu_ref, wd_ref, fin_ref, *rest, final_norm, ncast):
    o_ref = rest[ncast]
    _run_casts(rest[:ncast], rest[ncast + 1:])
    x = x_ref[...]
    h = _rms(x, g_ref[...]).astype(BF16)
    hid = wg_ref.shape[1]
    ntiles = hid // MXU_N
    bounds = [MXU_N * (ntiles * j // FFN_HIDDEN_CHUNKS) for j in range(FFN_HIDDEN_CHUNKS + 1)]
    y = x
    for c0, c1 in zip(bounds[:-1], bounds[1:]):
        gate = _dot(h, wg_ref[:, c0:c1])
        up = _dot(h, wu_ref[:, c0:c1])
        a = (jax.nn.silu(gate) * up).astype(BF16)
        y = y + _dot(a, wd_ref[c0:c1, :])
    if final_norm:
        y = _rms(y, fin_ref[...])
    o_ref[...] = y


def _ffn(x2, g, wg, wu, wd, fin, *, final_norm, tm, cast=()):
    m, d = x2.shape
    hid = wg.shape[1]
    c_in, c_out, c_shapes = _cast_through(cast, m // tm)
    return pl.pallas_call(
        functools.partial(_ffn_kernel, final_norm=final_norm, ncast=len(cast)),
        grid=(m // tm,),
        in_specs=[
            pl.BlockSpec((tm, d), lambda i: (i, 0)),
            _resident((1, d)),
            _resident((d, hid)),
            _resident((d, hid)),
            _resident((hid, d)),
            _resident((1, d)),
        ] + c_in,
        out_specs=[pl.BlockSpec((tm, d), lambda i: (i, 0))] + c_out,
        out_shape=[jax.ShapeDtypeStruct((m, d), F32)] + c_shapes,
        compiler_params=pltpu.CompilerParams(
            dimension_semantics=("arbitrary",), vmem_limit_bytes=VMEM_LIMIT_BYTES),
        name="swiglu",
    )(x2, g, wg, wu, wd, fin, *[w for w, _ in cast])


def _load_cast(w_hbm, w_s, stage_s, sem):
    depth, rb = stage_s.shape[0], stage_s.shape[1]
    n = w_hbm.shape[0] // rb

    def copy(i, slot):
        return pltpu.make_async_copy(w_hbm.at[pl.ds(i * rb, rb)], stage_s.at[slot], sem.at[slot])

    for i in range(depth - 1):
        copy(i, i).start()

    def body(i, carry):
        slot = i % depth

        @pl.when(i + depth - 1 < n)
        def _():
            copy(i + depth - 1, (i + depth - 1) % depth).start()

        copy(i, slot).wait()
        w_s[pl.ds(pl.multiple_of(i * rb, rb), rb), :] = stage_s[slot].astype(BF16)
        return carry

    lax.fori_loop(0, n, body, 0)


def _sgu_kernel(x_ref, g_ref, win_hbm, lng_ref, lnb_ref, ws_ref, bst_ref, wout_hbm,
                *rest, tm, ncast):
    o_ref = rest[ncast]
    v_s, y_s, win_ref, wout_ref, stage_in, stage_out, sem = rest[2 * ncast + 1:]
    _run_casts(rest[:ncast], rest[ncast + 1:2 * ncast + 1])

    @pl.when(pl.program_id(0) == 0)
    def _():
        _load_cast(win_hbm.at[0], win_ref, stage_in, sem)
        _load_cast(wout_hbm.at[0], wout_ref, stage_out, sem)

    width = v_s.shape[1]
    gc = width // SG_GROUPS
    lanes = 128
    x = x_ref[...]
    h = _rms(x, g_ref[...]).astype(BF16)

    gpc = SG_GROUPS_PER_CHUNK
    cw = gpc * gc
    p1 = jnp.zeros((tm, lanes), F32)
    p2 = jnp.zeros((tm, lanes), F32)
    for c in range(SG_GROUPS // gpc):
        v = _gelu(_dot(h, win_ref[:, width + c * cw:width + (c + 1) * cw]))
        v_s[:, c * cw:(c + 1) * cw] = v
        for l in range(cw // lanes):
            vl = v[:, l * lanes:(l + 1) * lanes]
            p1 = p1 + vl
            p2 = p2 + vl * vl
    mu = jnp.sum(p1, axis=-1, keepdims=True) * (1.0 / width)
    var = jnp.sum(p2, axis=-1, keepdims=True) * (1.0 / width) - mu * mu
    rstd = lax.rsqrt(var + EPS)
    shift = -mu * rstd

    ti = lax.broadcasted_iota(jnp.int32, (SG_BLOCK, SG_BLOCK), 0) // CHUNK
    si = lax.broadcasted_iota(jnp.int32, (SG_BLOCK, SG_BLOCK), 1) // CHUNK
    causal = ti >= si
    for c in range(SG_GROUPS // gpc):
        ccols = slice(c * cw, (c + 1) * cw)
        u = _gelu(_dot(h, win_ref[:, ccols]))
        vn = ((v_s[:, ccols] * rstd + shift) * lng_ref[:, ccols] + lnb_ref[:, ccols]).astype(BF16)
        for gg in range(gpc):
            g = c * gpc + gg
            gcols = slice(gg * gc, (gg + 1) * gc)
            ws = jnp.where(causal, ws_ref[g], 0.0).astype(BF16)
            bias = bst_ref[:, g:g + 1]
            for w in range(tm // SG_BLOCK):
                rows = slice(w * SG_BLOCK, (w + 1) * SG_BLOCK)
                s = _dot(ws, vn[rows, gcols]) + bias
                y_s[rows, g * gc:(g + 1) * gc] = (u[rows, gcols] * s).astype(BF16)
    o_ref[...] = x + _dot(y_s[...], wout_ref[...])


def _sgu(x2, g, win, lng, lnb, ws, bst, wout, *, tm, cast=()):
    m, d = x2.shape
    width = wout.shape[1]
    assert d % SGU_STAGE_ROWS[0] == 0 and width % SGU_STAGE_ROWS[1] == 0
    c_in, c_out, c_shapes = _cast_through(cast, m // tm)
    return pl.pallas_call(
        functools.partial(_sgu_kernel, tm=tm, ncast=len(cast)),
        grid=(m // tm,),
        in_specs=[
            pl.BlockSpec((tm, d), lambda i: (i, 0)),
            _resident((1, d)),
            pl.BlockSpec(memory_space=pl.ANY),
            _resident((1, width)),
            _resident((1, width)),
            _resident(ws.shape),
            _resident(bst.shape),
            pl.BlockSpec(memory_space=pl.ANY),
        ] + c_in,
        out_specs=[pl.BlockSpec((tm, d), lambda i: (i, 0))] + c_out,
        out_shape=[jax.ShapeDtypeStruct((m, d), F32)] + c_shapes,
        scratch_shapes=[
            pltpu.VMEM((tm, width), F32),
            pltpu.VMEM((tm, width), BF16),
            pltpu.VMEM((d, 2 * width), BF16),
            pltpu.VMEM((width, d), BF16),
            pltpu.VMEM((SGU_STAGE_DEPTH, SGU_STAGE_ROWS[0], 2 * width), F32),
            pltpu.VMEM((SGU_STAGE_DEPTH, SGU_STAGE_ROWS[1], d), F32),
            pltpu.SemaphoreType.DMA((SGU_STAGE_DEPTH,)),
        ],
        compiler_params=pltpu.CompilerParams(
            dimension_semantics=("arbitrary",), vmem_limit_bytes=VMEM_LIMIT_BYTES),
        name="sgu",
    )(x2, g, win, lng, lnb, ws, bst, wout, *[w for w, _ in cast])


def _ref_rows(G, half):
    n, w = G.shape
    blk = 2 * half
    if blk >= 8:
        return jnp.concatenate(
            [jnp.broadcast_to(G[b * blk + half - 1:b * blk + half, :], (blk, w))
             for b in range(n // blk)], axis=0)
    assert blk == 4
    sub = lax.broadcasted_iota(jnp.int32, (8, w), 0)
    tiles = []
    for t in range(n // 8):
        lo = jnp.broadcast_to(G[8 * t + 1:8 * t + 2, :], (8, w))
        hi = jnp.broadcast_to(G[8 * t + 5:8 * t + 6, :], (8, w))
        tiles.append(jnp.where(sub < 4, lo, hi))
    return jnp.concatenate(tiles, axis=0)


def _gla_project(x_ref, row0, tm, g_ref, win_ref, wg2_ref, bg_ref, proj_ref, la_ref, flag_ref, flag_idx):
    state = {}
    ncol = proj_ref.shape[1]
    pr, pc = GLA_PROJ_PIECE
    tasks = []

    def norm(r):
        def run():
            x = x_ref[row0 + r * pr:row0 + (r + 1) * pr, :]
            state[r] = _rms(x, g_ref[...]).astype(BF16)
            if r == 0:
                flag_ref[flag_idx] = jnp.float32(0.0)
        return run

    def cols(r, p):
        def run():
            proj_ref[r * pr:(r + 1) * pr, p * pc:(p + 1) * pc] = _dot(state[r], win_ref[:, p * pc:(p + 1) * pc])
        return run

    def gate(r):
        def run():
            gz = _dot(state[r], win_ref[:, ncol:])
            z = _dot(gz.astype(BF16), wg2_ref[...].astype(BF16)) + bg_ref[...]
            la = (jnp.minimum(z, 0.0) - jnp.log1p(jnp.exp(-jnp.abs(z)))) / GLA_TAU
            la_ref[r * pr:(r + 1) * pr, :] = la
            tot = [jnp.sum(la[c * CHUNK:(c + 1) * CHUNK, :], axis=0, keepdims=True) for c in range(pr // CHUNK)]
            flag_ref[flag_idx] = jnp.minimum(flag_ref[flag_idx], jnp.min(functools.reduce(jnp.minimum, tot)))
        return run

    for r in range(tm // pr):
        tasks += [norm(r), gate(r)] + [cols(r, p) for p in range(ncol // pc)]
    return tasks


def _gla_chunks(proj_ref, la_ref, og_s, st_s, gn_ref, consts, between, *, tm, kd, vd, fast):
    dk = kd // GLA_HEADS
    dv = vd // GLA_HEADS
    C = CHUNK
    tri2, halves, lvl_mask, diag, upper, causal = consts
    scale = dk ** -0.5
    between = list(between)
    span = 2 if fast else 1
    B = span * C
    nblocks = tm // B
    nslots = nblocks * GLA_HEADS

    def chunk_decay(c):
        la_all = la_ref[c * C:(c + 1) * C, :]
        la_hi = la_all.astype(BF16)
        la_lo = (la_all - la_hi.astype(F32)).astype(BF16)
        return la_all, _dot(tri2, jnp.concatenate([la_hi, la_lo], axis=0))

    def scores(b, hd, decays):
        rows = slice(b * B, (b + 1) * B)
        ks_ = slice(hd * dk, (hd + 1) * dk)
        q = proj_ref[rows, hd * dk:(hd + 1) * dk] * scale
        k = proj_ref[rows, kd + hd * dk:kd + (hd + 1) * dk]
        if fast:
            (_, ga0), (_, ga1) = decays
            g0, g1 = ga0[:, ks_], ga1[:, ks_]
            e0, e1 = g0[C - 1:C, :], g1[C - 1:C, :]
            q0, q1, k0, k1 = q[:C], q[C:], k[:C], k[C:]
            k0r = k0 * jnp.exp(e0 - g0)
            x1 = jnp.exp(g1)
            qd = jnp.concatenate([q0 * jnp.exp(g0), q1 * (x1 * jnp.exp(e0))], axis=0)
            qs = jnp.concatenate([q0 * jnp.exp(g0 - e0), q1 * x1], axis=0)
            kr = jnp.concatenate([k0r, k1 * jnp.exp(-g1)], axis=0)
            kdec = jnp.concatenate([k0r * jnp.exp(e1), k1 * jnp.exp(e1 - g1)], axis=0)
            return ([_dot_nt(qs.astype(BF16), kr.astype(BF16))], qd.astype(BF16), kdec.astype(BF16),
                    jnp.exp(e0 + e1))
        (la_all, g_all), = decays
        la = la_all[:, ks_]
        G = g_all[:, ks_]
        g_last = G[C - 1:C, :]
        qd = (q * jnp.exp(G)).astype(BF16)
        kdec = (k * jnp.exp(g_last - G)).astype(BF16)
        dec = jnp.exp(g_last)
        ps = [jnp.sum(q * k, axis=-1, keepdims=True)]
        for lvl, hf in enumerate(halves):
            if hf == 1:
                e = jnp.exp(jnp.where(upper[lvl], la, 0.0))
            else:
                e = jnp.exp(-jnp.abs(G - _ref_rows(G, hf)))
            xl = (jnp.where(upper[lvl], q, k) * e).astype(BF16)
            ps.append(_dot_nt(xl, xl))
        return ps, qd, kdec, dec

    def outputs(b, hd, ps, qd, kdec, dec):
        rows = slice(b * B, (b + 1) * B)
        vs_ = slice(2 * kd + hd * dv, 2 * kd + (hd + 1) * dv)
        rs_ = slice(2 * kd + vd + hd * dv, 2 * kd + vd + (hd + 1) * dv)
        if fast:
            a = jnp.where(causal, ps[0], 0.0)
        else:
            a = jnp.where(diag, ps[0], 0.0)
            for lvl in range(len(halves)):
                a = jnp.where(lvl_mask[lvl], ps[lvl + 1], a)
        v = proj_ref[rows, vs_].astype(BF16)
        st = st_s[hd]
        lhs = jnp.concatenate([qd, a.astype(BF16)], axis=1)
        rhs = jnp.concatenate([st.astype(BF16), v], axis=0)
        o = _dot(lhs, rhs)
        dcol = jnp.transpose(jnp.broadcast_to(dec, (dk, dk)))
        st_s[hd] = st * jnp.concatenate([dcol] * (dv // dk), axis=1) + _dot_tn(kdec, v)
        o = o * lax.rsqrt(jnp.mean(o * o, axis=-1, keepdims=True) + EPS) * gn_ref[hd:hd + 1, :]
        r = proj_ref[rows, rs_]
        og_s[rows, hd * dv:(hd + 1) * dv] = (jax.nn.silu(r) * o).astype(BF16)

    block_decay = lambda b: [chunk_decay(b * span + j) for j in range(span)]
    decay = {0: block_decay(0)}
    pending = []
    for slot in range(nslots):
        b, hd = divmod(slot, GLA_HEADS)
        if hd == 0 and b + 1 < nblocks:
            decay[b + 1] = block_decay(b + 1)
        pending.append((b, hd) + scores(b, hd, decay[b]))
        for task in between[slot * len(between) // nslots:(slot + 1) * len(between) // nslots]:
            task()
        if len(pending) > GLA_SKEW:
            outputs(*pending.pop(0))
    for item in pending:
        outputs(*item)


def _gla_kernel(xc_ref, xn_ref, g_ref, win_ref, wg2_ref, bg_ref, gn_ref, wout_ref,
                *rest, tm, kd, vd, tiles_per_seq, ncast):
    o_ref = rest[ncast]
    pa_s, la_a, pb_s, la_b, og_s, st_s, flag_s = rest[2 * ncast + 1:]
    _run_casts(rest[:ncast], rest[ncast + 1:2 * ncast + 1])
    C = CHUNK
    step = pl.program_id(0)
    wts = (g_ref, win_ref, wg2_ref, bg_ref)

    @pl.when(step == 0)
    def _():
        for task in _gla_project(xc_ref, 0, tm, *wts, pa_s, la_a, flag_s, 0):
            task()

    @pl.when((2 * step) % tiles_per_seq == 0)
    def _():
        st_s[...] = jnp.zeros_like(st_s)

    ri = lax.broadcasted_iota(jnp.int32, (C, C), 0)
    ci = lax.broadcasted_iota(jnp.int32, (C, C), 1)
    tri = (ri >= ci).astype(BF16)
    tri2 = jnp.concatenate([tri, tri], axis=1)
    halves = (32, 16, 8, 4, 2, 1)
    lvl_mask = [((ri // (2 * hf)) == (ci // (2 * hf))) & ((ri % (2 * hf)) >= hf) & ((ci % (2 * hf)) < hf)
                for hf in halves]
    diag = ri == ci
    rowc = lax.broadcasted_iota(jnp.int32, (C, kd // GLA_HEADS), 0)
    upper = [(rowc % (2 * hf)) >= hf for hf in halves]
    bi = lax.broadcasted_iota(jnp.int32, (2 * C, 2 * C), 0)
    bj = lax.broadcasted_iota(jnp.int32, (2 * C, 2 * C), 1)
    consts = (tri2, halves, lvl_mask, diag, upper, bi >= bj)

    def tile(proj_ref, la_ref, og_ref, flag_idx, nxt_x, nxt_row0, nxt_proj, nxt_la, nxt_flag):
        def run(fast):
            def body():
                nxt = _gla_project(nxt_x, nxt_row0, tm, *wts, nxt_proj, nxt_la, flag_s, nxt_flag)
                _gla_chunks(proj_ref, la_ref, og_ref, st_s, gn_ref, consts, nxt, tm=tm, kd=kd, vd=vd, fast=fast)
            return body
        mild = flag_s[flag_idx] > -GLA_FAST_RANGE
        pl.when(mild)(run(True))
        pl.when(jnp.logical_not(mild))(run(False))

    tile(pa_s, la_a, og_s.at[pl.ds(0, tm)], 0, xc_ref, tm, pb_s, la_b, 1)
    tile(pb_s, la_b, og_s.at[pl.ds(tm, tm)], 1, xn_ref, 0, pa_s, la_a, 0)
    o_ref[...] = xc_ref[...] + _dot(og_s[...], wout_ref[...])


def _gla(x2, g, win, wg2, bg, gn, wout, *, tm, seq_len, cast=()):
    m, d = x2.shape
    kd = wg2.shape[1]
    vd = wout.shape[0]
    dk = kd // GLA_HEADS
    dv = vd // GLA_HEADS
    nproj = win.shape[1] - wg2.shape[0]
    ntiles = m // tm
    assert seq_len % (2 * tm) == 0 and nproj % GLA_PROJ_PIECE[1] == 0 and tm % GLA_PROJ_PIECE[0] == 0
    c_in, c_out, c_shapes = _cast_through(cast, ntiles // 2)
    return pl.pallas_call(
        functools.partial(_gla_kernel, tm=tm, kd=kd, vd=vd, tiles_per_seq=seq_len // tm, ncast=len(cast)),
        grid=(ntiles // 2,),
        in_specs=[
            pl.BlockSpec((2 * tm, d), lambda i: (i, 0)),
            pl.BlockSpec((tm, d), lambda i: (jnp.minimum(2 * i + 2, ntiles - 1), 0)),
            _resident((1, d)),
            _resident(win.shape),
            _resident(wg2.shape),
            _resident((1, kd)),
            _resident(gn.shape),
            _resident((vd, d)),
        ] + c_in,
        out_specs=[pl.BlockSpec((2 * tm, d), lambda i: (i, 0))] + c_out,
        out_shape=[jax.ShapeDtypeStruct((m, d), F32)] + c_shapes,
        scratch_shapes=[
            pltpu.VMEM((tm, nproj), F32),
            pltpu.VMEM((tm, kd), F32),
            pltpu.VMEM((tm, nproj), F32),
            pltpu.VMEM((tm, kd), F32),
            pltpu.VMEM((2 * tm, vd), BF16),
            pltpu.VMEM((GLA_HEADS, dk, dv), F32),
            pltpu.SMEM((2,), F32),
        ],
        compiler_params=pltpu.CompilerParams(
            dimension_semantics=("arbitrary",), vmem_limit_bytes=VMEM_LIMIT_BYTES),
        name="gla",
    )(x2, x2, g, win, wg2, bg, gn, wout, *[w for w, _ in cast])


def kernel(x, mix_norm, a_w_in, a_ln_g, a_ln_b, a_w_s, a_b_s, a_w_out, b_w_in, b_w_g2, b_b_g, b_gn_g, b_w_out, ffn_norm, ffn_w_gate, ffn_w_up, ffn_w_down, final_norm):
    b, t, d = x.shape
    depth = mix_norm.shape[0]
    assert depth == 2 and a_w_in.shape[0] == 1 and b_w_in.shape[0] == 1
    m = b * t
    bf = lambda w: w.astype(BF16)
    row = lambda v: v.reshape(1, -1)

    x2 = x.reshape(m, d)
    x2, wg0, wu0, wd0 = _sgu(
        x2, row(mix_norm[0]), a_w_in, row(a_ln_g[0]), row(a_ln_b[0]),
        a_w_s[0], a_b_s[0].T, a_w_out, tm=SGU_TM,
        cast=((ffn_w_gate, 0), (ffn_w_up, 0), (ffn_w_down, 0)))
    x2, b_wout = _ffn(
        x2, row(ffn_norm[0]), wg0, wu0, wd0, row(final_norm), final_norm=False, tm=FFN_TM,
        cast=((b_w_out, 0),))
    x2, wg1, wu1, wd1 = _gla(
        x2, row(mix_norm[1]), bf(b_w_in[0]), b_w_g2[0], row(b_b_g[0]), b_gn_g[0], b_wout, tm=GLA_TM, seq_len=t,
        cast=((ffn_w_gate, 1), (ffn_w_up, 1), (ffn_w_down, 1)))
    (x2,) = _ffn(x2, row(ffn_norm[1]), wg1, wu1, wd1, row(final_norm), final_norm=True, tm=FFN_TM)
    return x2.reshape(b, t, d)
```

```python
import functools

import jax
import jax.numpy as jnp
from jax import lax
from jax.experimental import pallas as pl
from jax.experimental.pallas import tpu as pltpu

F32 = jnp.float32
BF16 = jnp.bfloat16

EPS = 1e-6
CHUNK = 64
SG_BLOCK = 128
SG_GROUPS = 8
SG_GROUPS_PER_CHUNK = 2
GLA_HEADS = 4
GLA_TAU = 16.0
GLA_FAST_RANGE = 60.0
GLA_SKEW = 1
GLA_PROJ_PIECE = (256, 512)

BF16_SUBLANES = 16
MXU_N = 256
FFN_HIDDEN_CHUNKS = 4

SGU_TM = 512
SGU_STAGE_ROWS = (32, 192)
SGU_STAGE_DEPTH = 4
FFN_TM = 1024
GLA_TM = 256

VMEM_LIMIT_BYTES = 56 * 1024 * 1024


def _rms(x, g):
    return x * lax.rsqrt(jnp.mean(x * x, axis=-1, keepdims=True) + EPS) * g


def _gelu(x):
    return 0.5 * x * (1.0 + lax.erf(x * (2.0 ** -0.5)))


def _dot(a, b):
    return jnp.dot(a, b, preferred_element_type=F32)


def _dot_nt(a, b):
    return lax.dot_general(a, b, (((1,), (1,)), ((), ())), preferred_element_type=F32)


def _dot_tn(a, b):
    return lax.dot_general(a, b, (((0,), (0,)), ((), ())), preferred_element_type=F32)


def _resident(shape):
    return pl.BlockSpec(shape, lambda *_: (0,) * len(shape))


def _cast_through(weights, nsteps):
    in_specs, out_specs, out_shapes = [], [], []
    for w, layer in weights:
        _, r, c = w.shape
        per = 1 if r % (nsteps * BF16_SUBLANES) == 0 else 2
        rb = r * per // nsteps
        assert rb % BF16_SUBLANES == 0 and rb * nsteps == r * per, (r, c, nsteps)
        in_specs.append(pl.BlockSpec((None, rb, c), lambda i, layer=layer, per=per: (layer, i // per, 0)))
        out_specs.append(pl.BlockSpec((rb, c), lambda i, per=per: (i // per, 0)))
        out_shapes.append(jax.ShapeDtypeStruct((r, c), BF16))
    return in_specs, out_specs, out_shapes


def _run_casts(src_refs, dst_refs):
    for src, dst in zip(src_refs, dst_refs):
        dst[...] = src[...].astype(BF16)


def _ffn_kernel(x_ref, g_ref, wg_ref, wu_ref, wd_ref, fin_ref, *rest, final_norm, ncast):
    o_ref = rest[ncast]
    _run_casts(rest[:ncast], rest[ncast + 1:])
    x = x_ref[...]
    h = _rms(x, g_ref[...]).astype(BF16)
    hid = wg_ref.shape[1]
    ntiles = hid // MXU_N
    bounds = [MXU_N * (ntiles * j // FFN_HIDDEN_CHUNKS) for j in range(FFN_HIDDEN_CHUNKS + 1)]
    y = x
    for c0, c1 in zip(bounds[:-1], bounds[1:]):
        gate = _dot(h, wg_ref[:, c0:c1])
        up = _dot(h, wu_ref[:, c0:c1])
        a = (jax.nn.silu(gate) * up).astype(BF16)
        y = y + _dot(a, wd_ref[c0:c1, :])
    if final_norm:
        y = _rms(y, fin_ref[...])
    o_ref[...] = y


def _ffn(x2, g, wg, wu, wd, fin, *, final_norm, tm, cast=()):
    m, d = x2.shape
    hid = wg.shape[1]
    c_in, c_out, c_shapes = _cast_through(cast, m // tm)
    return pl.pallas_call(
        functools.partial(_ffn_kernel, final_norm=final_norm, ncast=len(cast)),
        grid=(m // tm,),
        in_specs=[
            pl.BlockSpec((tm, d), lambda i: (i, 0)),
            _resident((1, d)),
            _resident((d, hid)),
            _resident((d, hid)),
            _resident((hid, d)),
            _resident((1, d)),
        ] + c_in,
        out_specs=[pl.BlockSpec((tm, d), lambda i: (i, 0))] + c_out,
        out_shape=[jax.ShapeDtypeStruct((m, d), F32)] + c_shapes,
        compiler_params=pltpu.CompilerParams(
            dimension_semantics=("arbitrary",), vmem_limit_bytes=VMEM_LIMIT_BYTES),
        name="swiglu",
    )(x2, g, wg, wu, wd, fin, *[w for w, _ in cast])


def _load_cast(w_hbm, w_s, stage_s, sem):
    depth, rb = stage_s.shape[0], stage_s.shape[1]
    n = w_hbm.shape[0] // rb

    def copy(i, slot):
        return pltpu.make_async_copy(w_hbm.at[pl.ds(i * rb, rb)], stage_s.at[slot], sem.at[slot])

    for i in range(depth - 1):
        copy(i, i).start()

    def body(i, carry):
        slot = i % depth

        @pl.when(i + depth - 1 < n)
        def _():
            copy(i + depth - 1, (i + depth - 1) % depth).start()

        copy(i, slot).wait()
        w_s[pl.ds(pl.multiple_of(i * rb, rb), rb), :] = stage_s[slot].astype(BF16)
        return carry

    lax.fori_loop(0, n, body, 0)


def _sgu_kernel(x_ref, g_ref, win_hbm, lng_ref, lnb_ref, ws_ref, bst_ref, wout_hbm,
                *rest, tm, ncast):
    o_ref = rest[ncast]
    v_s, y_s, win_ref, wout_ref, stage_in, stage_out, sem = rest[2 * ncast + 1:]
    _run_casts(rest[:ncast], rest[ncast + 1:2 * ncast + 1])

    @pl.when(pl.program_id(0) == 0)
    def _():
        _load_cast(win_hbm.at[0], win_ref, stage_in, sem)
        _load_cast(wout_hbm.at[0], wout_ref, stage_out, sem)

    width = v_s.shape[1]
    gc = width // SG_GROUPS
    lanes = 128
    x = x_ref[...]
    h = _rms(x, g_ref[...]).astype(BF16)

    gpc = SG_GROUPS_PER_CHUNK
    cw = gpc * gc
    p1 = jnp.zeros((tm, lanes), F32)
    p2 = jnp.zeros((tm, lanes), F32)
    for c in range(SG_GROUPS // gpc):
        v = _gelu(_dot(h, win_ref[:, width + c * cw:width + (c + 1) * cw]))
        v_s[:, c * cw:(c + 1) * cw] = v
        for l in range(cw // lanes):
            vl = v[:, l * lanes:(l + 1) * lanes]
            p1 = p1 + vl
            p2 = p2 + vl * vl
    mu = jnp.sum(p1, axis=-1, keepdims=True) * (1.0 / width)
    var = jnp.sum(p2, axis=-1, keepdims=True) * (1.0 / width) - mu * mu
    rstd = lax.rsqrt(var + EPS)
    shift = -mu * rstd

    ti = lax.broadcasted_iota(jnp.int32, (SG_BLOCK, SG_BLOCK), 0) // CHUNK
    si = lax.broadcasted_iota(jnp.int32, (SG_BLOCK, SG_BLOCK), 1) // CHUNK
    causal = ti >= si
    for c in range(SG_GROUPS // gpc):
        ccols = slice(c * cw, (c + 1) * cw)
        u = _gelu(_dot(h, win_ref[:, ccols]))
        vn = ((v_s[:, ccols] * rstd + shift) * lng_ref[:, ccols] + lnb_ref[:, ccols]).astype(BF16)
        for gg in range(gpc):
            g = c * gpc + gg
            gcols = slice(gg * gc, (gg + 1) * gc)
            ws = jnp.where(causal, ws_ref[g], 0.0).astype(BF16)
            bias = bst_ref[:, g:g + 1]
            for w in range(tm // SG_BLOCK):
                rows = slice(w * SG_BLOCK, (w + 1) * SG_BLOCK)
                s = _dot(ws, vn[rows, gcols]) + bias
                y_s[rows, g * gc:(g + 1) * gc] = (u[rows, gcols] * s).astype(BF16)
    o_ref[...] = x + _dot(y_s[...], wout_ref[...])


def _sgu(x2, g, win, lng, lnb, ws, bst, wout, *, tm, cast=()):
    m, d = x2.shape
    width = wout.shape[1]
    assert d % SGU_STAGE_ROWS[0] == 0 and width % SGU_STAGE_ROWS[1] == 0
    c_in, c_out, c_shapes = _cast_through(cast, m // tm)
    return pl.pallas_call(
        functools.partial(_sgu_kernel, tm=tm, ncast=len(cast)),
        grid=(m // tm,),
        in_specs=[
            pl.BlockSpec((tm, d), lambda i: (i, 0)),
            _resident((1, d)),
            pl.BlockSpec(memory_space=pl.ANY),
            _resident((1, width)),
            _resident((1, width)),
            _resident(ws.shape),
            _resident(bst.shape),
            pl.BlockSpec(memory_space=pl.ANY),
        ] + c_in,
        out_specs=[pl.BlockSpec((tm, d), lambda i: (i, 0))] + c_out,
        out_shape=[jax.ShapeDtypeStruct((m, d), F32)] + c_shapes,
        scratch_shapes=[
            pltpu.VMEM((tm, width), F32),
            pltpu.VMEM((tm, width), BF16),
            pltpu.VMEM((d, 2 * width), BF16),
            pltpu.VMEM((width, d), BF16),
            pltpu.VMEM((SGU_STAGE_DEPTH, SGU_STAGE_ROWS[0], 2 * width), F32),
            pltpu.VMEM((SGU_STAGE_DEPTH, SGU_STAGE_ROWS[1], d), F32),
            pltpu.SemaphoreType.DMA((SGU_STAGE_DEPTH,)),
        ],
        compiler_params=pltpu.CompilerParams(
            dimension_semantics=("arbitrary",), vmem_limit_bytes=VMEM_LIMIT_BYTES),
        name="sgu",
    )(x2, g, win, lng, lnb, ws, bst, wout, *[w for w, _ in cast])


def _ref_rows(G, half):
    n, w = G.shape
    blk = 2 * half
    if blk >= 8:
        return jnp.concatenate(
            [jnp.broadcast_to(G[b * blk + half - 1:b * blk + half, :], (blk, w))
             for b in range(n // blk)], axis=0)
    assert blk == 4
    sub = lax.broadcasted_iota(jnp.int32, (8, w), 0)
    tiles = []
    for t in range(n // 8):
        lo = jnp.broadcast_to(G[8 * t + 1:8 * t + 2, :], (8, w))
        hi = jnp.broadcast_to(G[8 * t + 5:8 * t + 6, :], (8, w))
        tiles.append(jnp.where(sub < 4, lo, hi))
    return jnp.concatenate(tiles, axis=0)


def _gla_project(x_ref, row0, tm, g_ref, win_ref, wg2_ref, bg_ref, proj_ref, la_ref, flag_ref, flag_idx):
    state = {}
    ncol = proj_ref.shape[1]
    pr, pc = GLA_PROJ_PIECE
    tasks = []

    def norm(r):
        def run():
            x = x_ref[row0 + r * pr:row0 + (r + 1) * pr, :]
            state[r] = _rms(x, g_ref[...]).astype(BF16)
            if r == 0:
                flag_ref[flag_idx] = jnp.float32(0.0)
        return run

    def cols(r, p):
        def run():
            proj_ref[r * pr:(r + 1) * pr, p * pc:(p + 1) * pc] = _dot(state[r], win_ref[:, p * pc:(p + 1) * pc])
        return run

    def gate(r):
        def run():
            gz = _dot(state[r], win_ref[:, ncol:])
            z = _dot(gz.astype(BF16), wg2_ref[...].astype(BF16)) + bg_ref[...]
            la = (jnp.minimum(z, 0.0) - jnp.log1p(jnp.exp(-jnp.abs(z)))) / GLA_TAU
            la_ref[r * pr:(r + 1) * pr, :] = la
            tot = [jnp.sum(la[c * CHUNK:(c + 1) * CHUNK, :], axis=0, keepdims=True) for c in range(pr // CHUNK)]
            flag_ref[flag_idx] = jnp.minimum(flag_ref[flag_idx], jnp.min(functools.reduce(jnp.minimum, tot)))
        return run

    for r in range(tm // pr):
        pieces = [cols(r, p) for p in range(ncol // pc)]
        half = len(pieces) // 2
        tasks += [norm(r)] + pieces[:half] + [gate(r)] + pieces[half:]
    return tasks


def _gla_chunks(proj_ref, la_ref, og_s, st_s, gn_ref, consts, between, *, tm, kd, vd, fast):
    dk = kd // GLA_HEADS
    dv = vd // GLA_HEADS
    C = CHUNK
    tri2, halves, lvl_mask, diag, upper, causal = consts
    scale = dk ** -0.5
    between = list(between)
    span = 2 if fast else 1
    B = span * C
    nblocks = tm // B
    nslots = nblocks * GLA_HEADS

    def chunk_decay(c):
        la_all = la_ref[c * C:(c + 1) * C, :]
        la_hi = la_all.astype(BF16)
        la_lo = (la_all - la_hi.astype(F32)).astype(BF16)
        return la_all, _dot(tri2, jnp.concatenate([la_hi, la_lo], axis=0))

    def scores(b, hd, decays):
        rows = slice(b * B, (b + 1) * B)
        ks_ = slice(hd * dk, (hd + 1) * dk)
        q = proj_ref[rows, hd * dk:(hd + 1) * dk] * scale
        k = proj_ref[rows, kd + hd * dk:kd + (hd + 1) * dk]
        if fast:
            (_, ga0), (_, ga1) = decays
            g0, g1 = ga0[:, ks_], ga1[:, ks_]
            e0, e1 = g0[C - 1:C, :], g1[C - 1:C, :]
            q0, q1, k0, k1 = q[:C], q[C:], k[:C], k[C:]
            k0r = k0 * jnp.exp(e0 - g0)
            x1 = jnp.exp(g1)
            qd = jnp.concatenate([q0 * jnp.exp(g0), q1 * (x1 * jnp.exp(e0))], axis=0)
            qs = jnp.concatenate([q0 * jnp.exp(g0 - e0), q1 * x1], axis=0)
            kr = jnp.concatenate([k0r, k1 * jnp.exp(-g1)], axis=0)
            kdec = jnp.concatenate([k0r * jnp.exp(e1), k1 * jnp.exp(e1 - g1)], axis=0)
            return ([_dot_nt(qs.astype(BF16), kr.astype(BF16))], qd.astype(BF16), kdec.astype(BF16),
                    jnp.exp(e0 + e1))
        (la_all, g_all), = decays
        la = la_all[:, ks_]
        G = g_all[:, ks_]
        g_last = G[C - 1:C, :]
        qd = (q * jnp.exp(G)).astype(BF16)
        kdec = (k * jnp.exp(g_last - G)).astype(BF16)
        dec = jnp.exp(g_last)
        ps = [jnp.sum(q * k, axis=-1, keepdims=True)]
        for lvl, hf in enumerate(halves):
            if hf == 1:
                e = jnp.exp(jnp.where(upper[lvl], la, 0.0))
            else:
                e = jnp.exp(-jnp.abs(G - _ref_rows(G, hf)))
            xl = (jnp.where(upper[lvl], q, k) * e).astype(BF16)
            ps.append(_dot_nt(xl, xl))
        return ps, qd, kdec, dec

    def outputs(b, hd, ps, qd, kdec, dec):
        rows = slice(b * B, (b + 1) * B)
        vs_ = slice(2 * kd + hd * dv, 2 * kd + (hd + 1) * dv)
        rs_ = slice(2 * kd + vd + hd * dv, 2 * kd + vd + (hd + 1) * dv)
        if fast:
            a = jnp.where(causal, ps[0], 0.0)
        else:
            a = jnp.where(diag, ps[0], 0.0)
            for lvl in range(len(halves)):
                a = jnp.where(lvl_mask[lvl], ps[lvl + 1], a)
        v = proj_ref[rows, vs_].astype(BF16)
        st = st_s[hd]
        lhs = jnp.concatenate([qd, a.astype(BF16)], axis=1)
        rhs = jnp.concatenate([st.astype(BF16), v], axis=0)
        o = _dot(lhs, rhs)
        dcol = jnp.transpose(jnp.broadcast_to(dec, (dk, dk)))
        st_s[hd] = st * jnp.concatenate([dcol] * (dv // dk), axis=1) + _dot_tn(kdec, v)
        o = o * lax.rsqrt(jnp.mean(o * o, axis=-1, keepdims=True) + EPS) * gn_ref[hd:hd + 1, :]
        r = proj_ref[rows, rs_]
        og_s[rows, hd * dv:(hd + 1) * dv] = (jax.nn.silu(r) * o).astype(BF16)

    block_decay = lambda b: [chunk_decay(b * span + j) for j in range(span)]
    decay = {0: block_decay(0)}
    pending = []
    for slot in range(nslots):
        b, hd = divmod(slot, GLA_HEADS)
        if hd == 0 and b + 1 < nblocks:
            decay[b + 1] = block_decay(b + 1)
        pending.append((b, hd) + scores(b, hd, decay[b]))
        for task in between[slot * len(between) // nslots:(slot + 1) * len(between) // nslots]:
            task()
        if len(pending) > GLA_SKEW:
            outputs(*pending.pop(0))
    for item in pending:
        outputs(*item)


def _gla_kernel(xc_ref, xn_ref, g_ref, win_ref, wg2_ref, bg_ref, gn_ref, wout_ref,
                *rest, tm, kd, vd, tiles_per_seq, ncast):
    o_ref = rest[ncast]
    pa_s, la_a, pb_s, la_b, og_s, st_s, flag_s = rest[2 * ncast + 1:]
    _run_casts(rest[:ncast], rest[ncast + 1:2 * ncast + 1])
    C = CHUNK
    step = pl.program_id(0)
    wts = (g_ref, win_ref, wg2_ref, bg_ref)

    @pl.when(step == 0)
    def _():
        for task in _gla_project(xc_ref, 0, tm, *wts, pa_s, la_a, flag_s, 0):
            task()

    @pl.when((2 * step) % tiles_per_seq == 0)
    def _():
        st_s[...] = jnp.zeros_like(st_s)

    ri = lax.broadcasted_iota(jnp.int32, (C, C), 0)
    ci = lax.broadcasted_iota(jnp.int32, (C, C), 1)
    tri = (ri >= ci).astype(BF16)
    tri2 = jnp.concatenate([tri, tri], axis=1)
    halves = (32, 16, 8, 4, 2, 1)
    lvl_mask = [((ri // (2 * hf)) == (ci // (2 * hf))) & ((ri % (2 * hf)) >= hf) & ((ci % (2 * hf)) < hf)
                for hf in halves]
    diag = ri == ci
    rowc = lax.broadcasted_iota(jnp.int32, (C, kd // GLA_HEADS), 0)
    upper = [(rowc % (2 * hf)) >= hf for hf in halves]
    bi = lax.broadcasted_iota(jnp.int32, (2 * C, 2 * C), 0)
    bj = lax.broadcasted_iota(jnp.int32, (2 * C, 2 * C), 1)
    consts = (tri2, halves, lvl_mask, diag, upper, bi >= bj)

    def tile(proj_ref, la_ref, og_ref, flag_idx, nxt_x, nxt_row0, nxt_proj, nxt_la, nxt_flag):
        def run(fast):
            def body():
                nxt = _gla_project(nxt_x, nxt_row0, tm, *wts, nxt_proj, nxt_la, flag_s, nxt_flag)
                _gla_chunks(proj_ref, la_ref, og_ref, st_s, gn_ref, consts, nxt, tm=tm, kd=kd, vd=vd, fast=fast)
            return body
        mild = flag_s[flag_idx] > -GLA_FAST_RANGE
        pl.when(mild)(run(True))
        pl.when(jnp.logical_not(mild))(run(False))

    tile(pa_s, la_a, og_s.at[pl.ds(0, tm)], 0, xc_ref, tm, pb_s, la_b, 1)
    tile(pb_s, la_b, og_s.at[pl.ds(tm, tm)], 1, xn_ref, 0, pa_s, la_a, 0)
    o_ref[...] = xc_ref[...] + _dot(og_s[...], wout_ref[...])


def _gla(x2, g, win, wg2, bg, gn, wout, *, tm, seq_len, cast=()):
    m, d = x2.shape
    kd = wg2.shape[1]
    vd = wout.shape[0]
    dk = kd // GLA_HEADS
    dv = vd // GLA_HEADS
    nproj = win.shape[1] - wg2.shape[0]
    ntiles = m // tm
    assert seq_len % (2 * tm) == 0 and nproj % GLA_PROJ_PIECE[1] == 0 and tm % GLA_PROJ_PIECE[0] == 0
    c_in, c_out, c_shapes = _cast_through(cast, ntiles // 2)
    return pl.pallas_call(
        functools.partial(_gla_kernel, tm=tm, kd=kd, vd=vd, tiles_per_seq=seq_len // tm, ncast=len(cast)),
        grid=(ntiles // 2,),
        in_specs=[
            pl.BlockSpec((2 * tm, d), lambda i: (i, 0)),
            pl.BlockSpec((tm, d), lambda i: (jnp.minimum(2 * i + 2, ntiles - 1), 0)),
            _resident((1, d)),
            _resident(win.shape),
            _resident(wg2.shape),
            _resident((1, kd)),
            _resident(gn.shape),
            _resident((vd, d)),
        ] + c_in,
        out_specs=[pl.BlockSpec((2 * tm, d), lambda i: (i, 0))] + c_out,
        out_shape=[jax.ShapeDtypeStruct((m, d), F32)] + c_shapes,
        scratch_shapes=[
            pltpu.VMEM((tm, nproj), F32),
            pltpu.VMEM((tm, kd), F32),
            pltpu.VMEM((tm, nproj), F32),
            pltpu.VMEM((tm, kd), F32),
            pltpu.VMEM((2 * tm, vd), BF16),
            pltpu.VMEM((GLA_HEADS, dk, dv), F32),
            pltpu.SMEM((2,), F32),
        ],
        compiler_params=pltpu.CompilerParams(
            dimension_semantics=("arbitrary",), vmem_limit_bytes=VMEM_LIMIT_BYTES),
        name="gla",
    )(x2, x2, g, win, wg2, bg, gn, wout, *[w for w, _ in cast])


def kernel(x, mix_norm, a_w_in, a_ln_g, a_ln_b, a_w_s, a_b_s, a_w_out, b_w_in, b_w_g2, b_b_g, b_gn_g, b_w_out, ffn_norm, ffn_w_gate, ffn_w_up, ffn_w_down, final_norm):
    b, t, d = x.shape
    depth = mix_norm.shape[0]
    assert depth == 2 and a_w_in.shape[0] == 1 and b_w_in.shape[0] == 1
    m = b * t
    bf = lambda w: w.astype(BF16)
    row = lambda v: v.reshape(1, -1)

    x2 = x.reshape(m, d)
    x2, wg0, wu0, wd0 = _sgu(
        x2, row(mix_norm[0]), a_w_in, row(a_ln_g[0]), row(a_ln_b[0]),
        a_w_s[0], a_b_s[0].T, a_w_out, tm=SGU_TM,
        cast=((ffn_w_gate, 0), (ffn_w_up, 0), (ffn_w_down, 0)))
    x2, b_wout = _ffn(
        x2, row(ffn_norm[0]), wg0, wu0, wd0, row(final_norm), final_norm=False, tm=FFN_TM,
        cast=((b_w_out, 0),))
    x2, wg1, wu1, wd1 = _gla(
        x2, row(mix_norm[1]), bf(b_w_in[0]), b_w_g2[0], row(b_b_g[0]), b_gn_g[0], b_wout, tm=GLA_TM, seq_len=t,
        cast=((ffn_w_gate, 1), (ffn_w_up, 1), (ffn_w_down, 1)))
    (x2,) = _ffn(x2, row(ffn_norm[1]), wg1, wu1, wd1, row(final_norm), final_norm=True, tm=FFN_TM)
    return x2.reshape(b, t, d)
```

```python
import functools

import jax
import jax.numpy as jnp
from jax import lax
from jax.experimental import pallas as pl
from jax.experimental.pallas import tpu as pltpu

F32 = jnp.float32
BF16 = jnp.bfloat16

EPS = 1e-6
CHUNK = 64
SG_BLOCK = 128
SG_GROUPS = 8
SG_GROUPS_PER_CHUNK = 2
GLA_HEADS = 4
GLA_TAU = 16.0
GLA_FAST_RANGE = 60.0
GLA_SKEW = 1
GLA_PROJ_PIECE = (256, 512)

BF16_SUBLANES = 16
MXU_N = 256
FFN_HIDDEN_CHUNKS = 4

SGU_TM = 512
SGU_STAGE_ROWS = (32, 192)
SGU_STAGE_DEPTH = 4
FFN_TM = 1024
GLA_TM = 256

VMEM_LIMIT_BYTES = 56 * 1024 * 1024


def _rms(x, g):
    return x * lax.rsqrt(jnp.mean(x * x, axis=-1, keepdims=True) + EPS) * g


def _gelu(x):
    return 0.5 * x * (1.0 + lax.erf(x * (2.0 ** -0.5)))


def _dot(a, b):
    return jnp.dot(a, b, preferred_element_type=F32)


def _dot_nt(a, b):
    return lax.dot_general(a, b, (((1,), (1,)), ((), ())), preferred_element_type=F32)


def _dot_tn(a, b):
    return lax.dot_general(a, b, (((0,), (0,)), ((), ())), preferred_element_type=F32)


def _resident(shape):
    return pl.BlockSpec(shape, lambda *_: (0,) * len(shape))


def _cast_through(weights, nsteps):
    in_specs, out_specs, out_shapes = [], [], []
    for w, layer in weights:
        _, r, c = w.shape
        per = 1 if r % (nsteps * BF16_SUBLANES) == 0 else 2
        rb = r * per // nsteps
        assert rb % BF16_SUBLANES == 0 and rb * nsteps == r * per, (r, c, nsteps)
        in_specs.append(pl.BlockSpec((None, rb, c), lambda i, layer=layer, per=per: (layer, i // per, 0)))
        out_specs.append(pl.BlockSpec((rb, c), lambda i, per=per: (i // per, 0)))
        out_shapes.append(jax.ShapeDtypeStruct((r, c), BF16))
    return in_specs, out_specs, out_shapes


def _run_casts(src_refs, dst_refs):
    for src, dst in zip(src_refs, dst_refs):
        dst[...] = src[...].astype(BF16)


def _ffn_kernel(x_ref, g_ref, wg_ref, wu_ref, wd_ref, fin_ref, *rest, final_norm, ncast):
    o_ref = rest[ncast]
    _run_casts(rest[:ncast], rest[ncast + 1:])
    x = x_ref[...]
    h = _rms(x, g_ref[...]).astype(BF16)
    hid = wg_ref.shape[1]
    ntiles = hid // MXU_N
    bounds = [MXU_N * (ntiles * j // FFN_HIDDEN_CHUNKS) for j in range(FFN_HIDDEN_CHUNKS + 1)]
    y = x
    for c0, c1 in zip(bounds[:-1], bounds[1:]):
        gate = _dot(h, wg_ref[:, c0:c1])
        up = _dot(h, wu_ref[:, c0:c1])
        a = (jax.nn.silu(gate) * up).astype(BF16)
        y = y + _dot(a, wd_ref[c0:c1, :])
    if final_norm:
        y = _rms(y, fin_ref[...])
    o_ref[...] = y


def _ffn(x2, g, wg, wu, wd, fin, *, final_norm, tm, cast=()):
    m, d = x2.shape
    hid = wg.shape[1]
    c_in, c_out, c_shapes = _cast_through(cast, m // tm)
    return pl.pallas_call(
        functools.partial(_ffn_kernel, final_norm=final_norm, ncast=len(cast)),
        grid=(m // tm,),
        in_specs=[
            pl.BlockSpec((tm, d), lambda i: (i, 0)),
            _resident((1, d)),
            _resident((d, hid)),
            _resident((d, hid)),
            _resident((hid, d)),
            _resident((1, d)),
        ] + c_in,
        out_specs=[pl.BlockSpec((tm, d), lambda i: (i, 0))] + c_out,
        out_shape=[jax.ShapeDtypeStruct((m, d), F32)] + c_shapes,
        compiler_params=pltpu.CompilerParams(
            dimension_semantics=("arbitrary",), vmem_limit_bytes=VMEM_LIMIT_BYTES),
        name="swiglu",
    )(x2, g, wg, wu, wd, fin, *[w for w, _ in cast])


def _load_cast(w_hbm, w_s, stage_s, sem):
    depth, rb = stage_s.shape[0], stage_s.shape[1]
    n = w_hbm.shape[0] // rb

    def copy(i, slot):
        return pltpu.make_async_copy(w_hbm.at[pl.ds(i * rb, rb)], stage_s.at[slot], sem.at[slot])

    for i in range(depth - 1):
        copy(i, i).start()

    def body(i, carry):
        slot = i % depth

        @pl.when(i + depth - 1 < n)
        def _():
            copy(i + depth - 1, (i + depth - 1) % depth).start()

        copy(i, slot).wait()
        w_s[pl.ds(pl.multiple_of(i * rb, rb), rb), :] = stage_s[slot].astype(BF16)
        return carry

    lax.fori_loop(0, n, body, 0)


def _sgu_kernel(x_ref, g_ref, win_hbm, lng_ref, lnb_ref, ws_ref, bst_ref, wout_hbm,
                *rest, tm, ncast):
    o_ref = rest[ncast]
    v_s, y_s, win_ref, wout_ref, stage_in, stage_out, sem = rest[2 * ncast + 1:]
    _run_casts(rest[:ncast], rest[ncast + 1:2 * ncast + 1])

    @pl.when(pl.program_id(0) == 0)
    def _():
        _load_cast(win_hbm.at[0], win_ref, stage_in, sem)
        _load_cast(wout_hbm.at[0], wout_ref, stage_out, sem)

    width = v_s.shape[1]
    gc = width // SG_GROUPS
    lanes = 128
    x = x_ref[...]
    h = _rms(x, g_ref[...]).astype(BF16)

    gpc = SG_GROUPS_PER_CHUNK
    cw = gpc * gc
    p1 = jnp.zeros((tm, lanes), F32)
    p2 = jnp.zeros((tm, lanes), F32)
    for c in range(SG_GROUPS // gpc):
        v = _gelu(_dot(h, win_ref[:, width + c * cw:width + (c + 1) * cw]))
        v_s[:, c * cw:(c + 1) * cw] = v
        for l in range(cw // lanes):
            vl = v[:, l * lanes:(l + 1) * lanes]
            p1 = p1 + vl
            p2 = p2 + vl * vl
    mu = jnp.sum(p1, axis=-1, keepdims=True) * (1.0 / width)
    var = jnp.sum(p2, axis=-1, keepdims=True) * (1.0 / width) - mu * mu
    rstd = lax.rsqrt(var + EPS)
    shift = -mu * rstd

    ti = lax.broadcasted_iota(jnp.int32, (SG_BLOCK, SG_BLOCK), 0) // CHUNK
    si = lax.broadcasted_iota(jnp.int32, (SG_BLOCK, SG_BLOCK), 1) // CHUNK
    causal = ti >= si
    for c in range(SG_GROUPS // gpc):
        ccols = slice(c * cw, (c + 1) * cw)
        u = _gelu(_dot(h, win_ref[:, ccols]))
        vn = ((v_s[:, ccols] * rstd + shift) * lng_ref[:, ccols] + lnb_ref[:, ccols]).astype(BF16)
        for gg in range(gpc):
            g = c * gpc + gg
            gcols = slice(gg * gc, (gg + 1) * gc)
            ws = jnp.where(causal, ws_ref[g], 0.0).astype(BF16)
            bias = bst_ref[:, g:g + 1]
            for w in range(tm // SG_BLOCK):
                rows = slice(w * SG_BLOCK, (w + 1) * SG_BLOCK)
                s = _dot(ws, vn[rows, gcols]) + bias
                y_s[rows, g * gc:(g + 1) * gc] = (u[rows, gcols] * s).astype(BF16)
    o_ref[...] = x + _dot(y_s[...], wout_ref[...])


def _sgu(x2, g, win, lng, lnb, ws, bst, wout, *, tm, cast=()):
    m, d = x2.shape
    width = wout.shape[1]
    assert d % SGU_STAGE_ROWS[0] == 0 and width % SGU_STAGE_ROWS[1] == 0
    c_in, c_out, c_shapes = _cast_through(cast, m // tm)
    return pl.pallas_call(
        functools.partial(_sgu_kernel, tm=tm, ncast=len(cast)),
        grid=(m // tm,),
        in_specs=[
            pl.BlockSpec((tm, d), lambda i: (i, 0)),
            _resident((1, d)),
            pl.BlockSpec(memory_space=pl.ANY),
            _resident((1, width)),
            _resident((1, width)),
            _resident(ws.shape),
            _resident(bst.shape),
            pl.BlockSpec(memory_space=pl.ANY),
        ] + c_in,
        out_specs=[pl.BlockSpec((tm, d), lambda i: (i, 0))] + c_out,
        out_shape=[jax.ShapeDtypeStruct((m, d), F32)] + c_shapes,
        scratch_shapes=[
            pltpu.VMEM((tm, width), F32),
            pltpu.VMEM((tm, width), BF16),
            pltpu.VMEM((d, 2 * width), BF16),
            pltpu.VMEM((width, d), BF16),
            pltpu.VMEM((SGU_STAGE_DEPTH, SGU_STAGE_ROWS[0], 2 * width), F32),
            pltpu.VMEM((SGU_STAGE_DEPTH, SGU_STAGE_ROWS[1], d), F32),
            pltpu.SemaphoreType.DMA((SGU_STAGE_DEPTH,)),
        ],
        compiler_params=pltpu.CompilerParams(
            dimension_semantics=("arbitrary",), vmem_limit_bytes=VMEM_LIMIT_BYTES),
        name="sgu",
    )(x2, g, win, lng, lnb, ws, bst, wout, *[w for w, _ in cast])


def _ref_rows(G, half):
    n, w = G.shape
    blk = 2 * half
    if blk >= 8:
        return jnp.concatenate(
            [jnp.broadcast_to(G[b * blk + half - 1:b * blk + half, :], (blk, w))
             for b in range(n // blk)], axis=0)
    assert blk == 4
    sub = lax.broadcasted_iota(jnp.int32, (8, w), 0)
    tiles = []
    for t in range(n // 8):
        lo = jnp.broadcast_to(G[8 * t + 1:8 * t + 2, :], (8, w))
        hi = jnp.broadcast_to(G[8 * t + 5:8 * t + 6, :], (8, w))
        tiles.append(jnp.where(sub < 4, lo, hi))
    return jnp.concatenate(tiles, axis=0)


def _gla_project(x_ref, row0, tm, g_ref, win_ref, wg2_ref, bg_ref, proj_ref, la_ref, flag_ref, flag_idx):
    state = {}
    ncol = proj_ref.shape[1]
    pr, pc = GLA_PROJ_PIECE
    tasks = []

    def norm(r):
        def run():
            x = x_ref[row0 + r * pr:row0 + (r + 1) * pr, :]
            state[r] = _rms(x, g_ref[...]).astype(BF16)
            if r == 0:
                flag_ref[flag_idx] = jnp.float32(0.0)
        return run

    def cols(r, p):
        def run():
            proj_ref[r * pr:(r + 1) * pr, p * pc:(p + 1) * pc] = _dot(state[r], win_ref[:, p * pc:(p + 1) * pc])
        return run

    def gate(r):
        def run():
            gz = _dot(state[r], win_ref[:, ncol:])
            z = _dot(gz.astype(BF16), wg2_ref[...].astype(BF16)) + bg_ref[...]
            la = (jnp.minimum(z, 0.0) - jnp.log1p(jnp.exp(-jnp.abs(z)))) / GLA_TAU
            la_ref[r * pr:(r + 1) * pr, :] = la
            tot = [jnp.sum(la[c * CHUNK:(c + 1) * CHUNK, :], axis=0, keepdims=True) for c in range(pr // CHUNK)]
            flag_ref[flag_idx] = jnp.minimum(flag_ref[flag_idx], jnp.min(functools.reduce(jnp.minimum, tot)))
        return run

    for r in range(tm // pr):
        pieces = [cols(r, p) for p in range(ncol // pc)]
        half = len(pieces) // 2
        tasks += [norm(r)] + pieces[:half] + [gate(r)] + pieces[half:]
    return tasks


def _gla_chunks(proj_ref, la_ref, og_s, st_s, gn_ref, consts, between, *, tm, kd, vd, fast):
    dk = kd // GLA_HEADS
    dv = vd // GLA_HEADS
    C = CHUNK
    row_all, halves, lvl_mask, diag, upper, causal = consts
    scale = dk ** -0.5
    between = list(between)
    span = 2 if fast else 1
    B = span * C
    nblocks = tm // B
    nslots = nblocks * GLA_HEADS

    def chunk_decay(c):
        la_all = la_ref[c * C:(c + 1) * C, :]
        g = la_all
        shift = 1
        while shift < C:
            g = g + jnp.where(row_all >= shift, pltpu.roll(g, shift, axis=0), 0.0)
            shift *= 2
        return la_all, g

    def scores(b, hd, decays):
        rows = slice(b * B, (b + 1) * B)
        ks_ = slice(hd * dk, (hd + 1) * dk)
        q = proj_ref[rows, hd * dk:(hd + 1) * dk] * scale
        k = proj_ref[rows, kd + hd * dk:kd + (hd + 1) * dk]
        if fast:
            (_, ga0), (_, ga1) = decays
            g0, g1 = ga0[:, ks_], ga1[:, ks_]
            e0, e1 = g0[C - 1:C, :], g1[C - 1:C, :]
            q0, q1, k0, k1 = q[:C], q[C:], k[:C], k[C:]
            k0r = k0 * jnp.exp(e0 - g0)
            x1 = jnp.exp(g1)
            qd = jnp.concatenate([q0 * jnp.exp(g0), q1 * (x1 * jnp.exp(e0))], axis=0)
            qs = jnp.concatenate([q0 * jnp.exp(g0 - e0), q1 * x1], axis=0)
            kr = jnp.concatenate([k0r, k1 * jnp.exp(-g1)], axis=0)
            kdec = jnp.concatenate([k0r * jnp.exp(e1), k1 * jnp.exp(e1 - g1)], axis=0)
            return ([_dot_nt(qs.astype(BF16), kr.astype(BF16))], qd.astype(BF16), kdec.astype(BF16),
                    jnp.exp(e0 + e1))
        (la_all, g_all), = decays
        la = la_all[:, ks_]
        G = g_all[:, ks_]
        g_last = G[C - 1:C, :]
        qd = (q * jnp.exp(G)).astype(BF16)
        kdec = (k * jnp.exp(g_last - G)).astype(BF16)
        dec = jnp.exp(g_last)
        ps = [jnp.sum(q * k, axis=-1, keepdims=True)]
        for lvl, hf in enumerate(halves):
            if hf == 1:
                e = jnp.exp(jnp.where(upper[lvl], la, 0.0))
            else:
                e = jnp.exp(-jnp.abs(G - _ref_rows(G, hf)))
            xl = (jnp.where(upper[lvl], q, k) * e).astype(BF16)
            ps.append(_dot_nt(xl, xl))
        return ps, qd, kdec, dec

    def outputs(b, hd, ps, qd, kdec, dec):
        rows = slice(b * B, (b + 1) * B)
        vs_ = slice(2 * kd + hd * dv, 2 * kd + (hd + 1) * dv)
        rs_ = slice(2 * kd + vd + hd * dv, 2 * kd + vd + (hd + 1) * dv)
        if fast:
            a = jnp.where(causal, ps[0], 0.0)
        else:
            a = jnp.where(diag, ps[0], 0.0)
            for lvl in range(len(halves)):
                a = jnp.where(lvl_mask[lvl], ps[lvl + 1], a)
        v = proj_ref[rows, vs_].astype(BF16)
        st = st_s[hd]
        lhs = jnp.concatenate([qd, a.astype(BF16)], axis=1)
        rhs = jnp.concatenate([st.astype(BF16), v], axis=0)
        o = _dot(lhs, rhs)
        dcol = jnp.transpose(jnp.broadcast_to(dec, (dk, dk)))
        st_s[hd] = st * jnp.concatenate([dcol] * (dv // dk), axis=1) + _dot_tn(kdec, v)
        o = o * lax.rsqrt(jnp.mean(o * o, axis=-1, keepdims=True) + EPS) * gn_ref[hd:hd + 1, :]
        r = proj_ref[rows, rs_]
        og_s[rows, hd * dv:(hd + 1) * dv] = (jax.nn.silu(r) * o).astype(BF16)

    block_decay = lambda b: [chunk_decay(b * span + j) for j in range(span)]
    decay = {0: block_decay(0)}
    pending = []
    for slot in range(nslots):
        b, hd = divmod(slot, GLA_HEADS)
        if hd == 0 and b + 1 < nblocks:
            decay[b + 1] = block_decay(b + 1)
        pending.append((b, hd) + scores(b, hd, decay[b]))
        for task in between[slot * len(between) // nslots:(slot + 1) * len(between) // nslots]:
            task()
        if len(pending) > GLA_SKEW:
            outputs(*pending.pop(0))
    for item in pending:
        outputs(*item)


def _gla_kernel(xc_ref, xn_ref, g_ref, win_ref, wg2_ref, bg_ref, gn_ref, wout_ref,
                *rest, tm, kd, vd, tiles_per_seq, ncast):
    o_ref = rest[ncast]
    pa_s, la_a, pb_s, la_b, og_s, st_s, flag_s = rest[2 * ncast + 1:]
    _run_casts(rest[:ncast], rest[ncast + 1:2 * ncast + 1])
    C = CHUNK
    step = pl.program_id(0)
    wts = (g_ref, win_ref, wg2_ref, bg_ref)

    @pl.when(step == 0)
    def _():
        for task in _gla_project(xc_ref, 0, tm, *wts, pa_s, la_a, flag_s, 0):
            task()

    @pl.when((2 * step) % tiles_per_seq == 0)
    def _():
        st_s[...] = jnp.zeros_like(st_s)

    ri = lax.broadcasted_iota(jnp.int32, (C, C), 0)
    ci = lax.broadcasted_iota(jnp.int32, (C, C), 1)
    row_all = lax.broadcasted_iota(jnp.int32, (C, kd), 0)
    halves = (32, 16, 8, 4, 2, 1)
    lvl_mask = [((ri // (2 * hf)) == (ci // (2 * hf))) & ((ri % (2 * hf)) >= hf) & ((ci % (2 * hf)) < hf)
                for hf in halves]
    diag = ri == ci
    rowc = lax.broadcasted_iota(jnp.int32, (C, kd // GLA_HEADS), 0)
    upper = [(rowc % (2 * hf)) >= hf for hf in halves]
    bi = lax.broadcasted_iota(jnp.int32, (2 * C, 2 * C), 0)
    bj = lax.broadcasted_iota(jnp.int32, (2 * C, 2 * C), 1)
    consts = (row_all, halves, lvl_mask, diag, upper, bi >= bj)

    def tile(proj_ref, la_ref, og_ref, flag_idx, nxt_x, nxt_row0, nxt_proj, nxt_la, nxt_flag):
        def run(fast):
            def body():
                nxt = _gla_project(nxt_x, nxt_row0, tm, *wts, nxt_proj, nxt_la, flag_s, nxt_flag)
                _gla_chunks(proj_ref, la_ref, og_ref, st_s, gn_ref, consts, nxt, tm=tm, kd=kd, vd=vd, fast=fast)
            return body
        mild = flag_s[flag_idx] > -GLA_FAST_RANGE
        pl.when(mild)(run(True))
        pl.when(jnp.logical_not(mild))(run(False))

    tile(pa_s, la_a, og_s.at[pl.ds(0, tm)], 0, xc_ref, tm, pb_s, la_b, 1)
    tile(pb_s, la_b, og_s.at[pl.ds(tm, tm)], 1, xn_ref, 0, pa_s, la_a, 0)
    o_ref[...] = xc_ref[...] + _dot(og_s[...], wout_ref[...])


def _gla(x2, g, win, wg2, bg, gn, wout, *, tm, seq_len, cast=()):
    m, d = x2.shape
    kd = wg2.shape[1]
    vd = wout.shape[0]
    dk = kd // GLA_HEADS
    dv = vd // GLA_HEADS
    nproj = win.shape[1] - wg2.shape[0]
    ntiles = m // tm
    assert seq_len % (2 * tm) == 0 and nproj % GLA_PROJ_PIECE[1] == 0 and tm % GLA_PROJ_PIECE[0] == 0
    c_in, c_out, c_shapes = _cast_through(cast, ntiles // 2)
    return pl.pallas_call(
        functools.partial(_gla_kernel, tm=tm, kd=kd, vd=vd, tiles_per_seq=seq_len // tm, ncast=len(cast)),
        grid=(ntiles // 2,),
        in_specs=[
            pl.BlockSpec((2 * tm, d), lambda i: (i, 0)),
            pl.BlockSpec((tm, d), lambda i: (jnp.minimum(2 * i + 2, ntiles - 1), 0)),
            _resident((1, d)),
            _resident(win.shape),
            _resident(wg2.shape),
            _resident((1, kd)),
            _resident(gn.shape),
            _resident((vd, d)),
        ] + c_in,
        out_specs=[pl.BlockSpec((2 * tm, d), lambda i: (i, 0))] + c_out,
        out_shape=[jax.ShapeDtypeStruct((m, d), F32)] + c_shapes,
        scratch_shapes=[
            pltpu.VMEM((tm, nproj), F32),
            pltpu.VMEM((tm, kd), F32),
            pltpu.VMEM((tm, nproj), F32),
            pltpu.VMEM((tm, kd), F32),
            pltpu.VMEM((2 * tm, vd), BF16),
            pltpu.VMEM((GLA_HEADS, dk, dv), F32),
            pltpu.SMEM((2,), F32),
        ],
        compiler_params=pltpu.CompilerParams(
            dimension_semantics=("arbitrary",), vmem_limit_bytes=VMEM_LIMIT_BYTES),
        name="gla",
    )(x2, x2, g, win, wg2, bg, gn, wout, *[w for w, _ in cast])


def kernel(x, mix_norm, a_w_in, a_ln_g, a_ln_b, a_w_s, a_b_s, a_w_out, b_w_in, b_w_g2, b_b_g, b_gn_g, b_w_out, ffn_norm, ffn_w_gate, ffn_w_up, ffn_w_down, final_norm):
    b, t, d = x.shape
    depth = mix_norm.shape[0]
    assert depth == 2 and a_w_in.shape[0] == 1 and b_w_in.shape[0] == 1
    m = b * t
    bf = lambda w: w.astype(BF16)
    row = lambda v: v.reshape(1, -1)

    x2 = x.reshape(m, d)
    x2, wg0, wu0, wd0 = _sgu(
        x2, row(mix_norm[0]), a_w_in, row(a_ln_g[0]), row(a_ln_b[0]),
        a_w_s[0], a_b_s[0].T, a_w_out, tm=SGU_TM,
        cast=((ffn_w_gate, 0), (ffn_w_up, 0), (ffn_w_down, 0)))
    x2, b_wout = _ffn(
        x2, row(ffn_norm[0]), wg0, wu0, wd0, row(final_norm), final_norm=False, tm=FFN_TM,
        cast=((b_w_out, 0),))
    x2, wg1, wu1, wd1 = _gla(
        x2, row(mix_norm[1]), bf(b_w_in[0]), b_w_g2[0], row(b_b_g[0]), b_gn_g[0], b_wout, tm=GLA_TM, seq_len=t,
        cast=((ffn_w_gate, 1), (ffn_w_up, 1), (ffn_w_down, 1)))
    (x2,) = _ffn(x2, row(ffn_norm[1]), wg1, wu1, wd1, row(final_norm), final_norm=True, tm=FFN_TM)
    return x2.reshape(b, t, d)
```

```python
import functools

import jax
import jax.numpy as jnp
from jax import lax
from jax.experimental import pallas as pl
from jax.experimental.pallas import tpu as pltpu

F32 = jnp.float32
BF16 = jnp.bfloat16

EPS = 1e-6
CHUNK = 64
SG_BLOCK = 128
SG_GROUPS = 8
SG_GROUPS_PER_CHUNK = 2
GLA_HEADS = 4
GLA_TAU = 16.0
GLA_FAST_RANGE = 60.0
GLA_SKEW = 1
GLA_PROJ_PIECE = (256, 512)

BF16_SUBLANES = 16
MXU_N = 256
FFN_HIDDEN_CHUNKS = 4

SGU_TM = 512
SGU_STAGE_ROWS = (16, 96)
SGU_STAGE_DEPTH = 8
FFN_TM = 1024
GLA_TM = 256

VMEM_LIMIT_BYTES = 56 * 1024 * 1024


def _rms(x, g):
    return x * lax.rsqrt(jnp.mean(x * x, axis=-1, keepdims=True) + EPS) * g


def _gelu(x):
    return 0.5 * x * (1.0 + lax.erf(x * (2.0 ** -0.5)))


def _dot(a, b):
    return jnp.dot(a, b, preferred_element_type=F32)


def _dot_nt(a, b):
    return lax.dot_general(a, b, (((1,), (1,)), ((), ())), preferred_element_type=F32)


def _dot_tn(a, b):
    return lax.dot_general(a, b, (((0,), (0,)), ((), ())), preferred_element_type=F32)


def _resident(shape):
    return pl.BlockSpec(shape, lambda *_: (0,) * len(shape))


def _cast_through(weights, nsteps):
    in_specs, out_specs, out_shapes = [], [], []
    for w, layer in weights:
        _, r, c = w.shape
        per = 1 if r % (nsteps * BF16_SUBLANES) == 0 else 2
        rb = r * per // nsteps
        assert rb % BF16_SUBLANES == 0 and rb * nsteps == r * per, (r, c, nsteps)
        in_specs.append(pl.BlockSpec((None, rb, c), lambda i, layer=layer, per=per: (layer, i // per, 0)))
        out_specs.append(pl.BlockSpec((rb, c), lambda i, per=per: (i // per, 0)))
        out_shapes.append(jax.ShapeDtypeStruct((r, c), BF16))
    return in_specs, out_specs, out_shapes


def _run_casts(src_refs, dst_refs):
    for src, dst in zip(src_refs, dst_refs):
        dst[...] = src[...].astype(BF16)


def _ffn_kernel(x_ref, g_ref, wg_ref, wu_ref, wd_ref, fin_ref, *rest, final_norm, ncast):
    o_ref = rest[ncast]
    _run_casts(rest[:ncast], rest[ncast + 1:])
    x = x_ref[...]
    h = _rms(x, g_ref[...]).astype(BF16)
    hid = wg_ref.shape[1]
    ntiles = hid // MXU_N
    bounds = [MXU_N * (ntiles * j // FFN_HIDDEN_CHUNKS) for j in range(FFN_HIDDEN_CHUNKS + 1)]
    y = x
    for c0, c1 in zip(bounds[:-1], bounds[1:]):
        gate = _dot(h, wg_ref[:, c0:c1])
        up = _dot(h, wu_ref[:, c0:c1])
        a = (jax.nn.silu(gate) * up).astype(BF16)
        y = y + _dot(a, wd_ref[c0:c1, :])
    if final_norm:
        y = _rms(y, fin_ref[...])
    o_ref[...] = y


def _ffn(x2, g, wg, wu, wd, fin, *, final_norm, tm, cast=()):
    m, d = x2.shape
    hid = wg.shape[1]
    c_in, c_out, c_shapes = _cast_through(cast, m // tm)
    return pl.pallas_call(
        functools.partial(_ffn_kernel, final_norm=final_norm, ncast=len(cast)),
        grid=(m // tm,),
        in_specs=[
            pl.BlockSpec((tm, d), lambda i: (i, 0)),
            _resident((1, d)),
            _resident((d, hid)),
            _resident((d, hid)),
            _resident((hid, d)),
            _resident((1, d)),
        ] + c_in,
        out_specs=[pl.BlockSpec((tm, d), lambda i: (i, 0))] + c_out,
        out_shape=[jax.ShapeDtypeStruct((m, d), F32)] + c_shapes,
        compiler_params=pltpu.CompilerParams(
            dimension_semantics=("arbitrary",), vmem_limit_bytes=VMEM_LIMIT_BYTES),
        name="swiglu",
    )(x2, g, wg, wu, wd, fin, *[w for w, _ in cast])


def _load_cast(w_hbm, w_s, stage_s, sem):
    depth, rb = stage_s.shape[0], stage_s.shape[1]
    n = w_hbm.shape[0] // rb

    def copy(i, slot):
        return pltpu.make_async_copy(w_hbm.at[pl.ds(i * rb, rb)], stage_s.at[slot], sem.at[slot])

    for i in range(depth - 1):
        copy(i, i).start()

    def body(i, carry):
        slot = i % depth

        @pl.when(i + depth - 1 < n)
        def _():
            copy(i + depth - 1, (i + depth - 1) % depth).start()

        copy(i, slot).wait()
        w_s[pl.ds(pl.multiple_of(i * rb, rb), rb), :] = stage_s[slot].astype(BF16)
        return carry

    lax.fori_loop(0, n, body, 0)


def _sgu_kernel(x_ref, g_ref, win_hbm, lng_ref, lnb_ref, ws_ref, bst_ref, wout_hbm,
                *rest, tm, ncast):
    o_ref = rest[ncast]
    v_s, y_s, win_ref, wout_ref, stage_in, stage_out, sem = rest[2 * ncast + 1:]
    _run_casts(rest[:ncast], rest[ncast + 1:2 * ncast + 1])

    @pl.when(pl.program_id(0) == 0)
    def _():
        _load_cast(win_hbm.at[0], win_ref, stage_in, sem)
        _load_cast(wout_hbm.at[0], wout_ref, stage_out, sem)

    width = v_s.shape[1]
    gc = width // SG_GROUPS
    lanes = 128
    x = x_ref[...]
    h = _rms(x, g_ref[...]).astype(BF16)

    gpc = SG_GROUPS_PER_CHUNK
    cw = gpc * gc
    p1 = jnp.zeros((tm, lanes), F32)
    p2 = jnp.zeros((tm, lanes), F32)
    for c in range(SG_GROUPS // gpc):
        v = _gelu(_dot(h, win_ref[:, width + c * cw:width + (c + 1) * cw]))
        v_s[:, c * cw:(c + 1) * cw] = v
        for l in range(cw // lanes):
            vl = v[:, l * lanes:(l + 1) * lanes]
            p1 = p1 + vl
            p2 = p2 + vl * vl
    mu = jnp.sum(p1, axis=-1, keepdims=True) * (1.0 / width)
    var = jnp.sum(p2, axis=-1, keepdims=True) * (1.0 / width) - mu * mu
    rstd = lax.rsqrt(var + EPS)
    shift = -mu * rstd

    ti = lax.broadcasted_iota(jnp.int32, (SG_BLOCK, SG_BLOCK), 0) // CHUNK
    si = lax.broadcasted_iota(jnp.int32, (SG_BLOCK, SG_BLOCK), 1) // CHUNK
    causal = ti >= si
    for c in range(SG_GROUPS // gpc):
        ccols = slice(c * cw, (c + 1) * cw)
        u = _gelu(_dot(h, win_ref[:, ccols]))
        vn = ((v_s[:, ccols] * rstd + shift) * lng_ref[:, ccols] + lnb_ref[:, ccols]).astype(BF16)
        for gg in range(gpc):
            g = c * gpc + gg
            gcols = slice(gg * gc, (gg + 1) * gc)
            ws = jnp.where(causal, ws_ref[g], 0.0).astype(BF16)
            bias = bst_ref[:, g:g + 1]
            for w in range(tm // SG_BLOCK):
                rows = slice(w * SG_BLOCK, (w + 1) * SG_BLOCK)
                s = _dot(ws, vn[rows, gcols]) + bias
                y_s[rows, g * gc:(g + 1) * gc] = (u[rows, gcols] * s).astype(BF16)
    o_ref[...] = x + _dot(y_s[...], wout_ref[...])


def _sgu(x2, g, win, lng, lnb, ws, bst, wout, *, tm, cast=()):
    m, d = x2.shape
    width = wout.shape[1]
    assert d % SGU_STAGE_ROWS[0] == 0 and width % SGU_STAGE_ROWS[1] == 0
    c_in, c_out, c_shapes = _cast_through(cast, m // tm)
    return pl.pallas_call(
        functools.partial(_sgu_kernel, tm=tm, ncast=len(cast)),
        grid=(m // tm,),
        in_specs=[
            pl.BlockSpec((tm, d), lambda i: (i, 0)),
            _resident((1, d)),
            pl.BlockSpec(memory_space=pl.ANY),
            _resident((1, width)),
            _resident((1, width)),
            _resident(ws.shape),
            _resident(bst.shape),
            pl.BlockSpec(memory_space=pl.ANY),
        ] + c_in,
        out_specs=[pl.BlockSpec((tm, d), lambda i: (i, 0))] + c_out,
        out_shape=[jax.ShapeDtypeStruct((m, d), F32)] + c_shapes,
        scratch_shapes=[
            pltpu.VMEM((tm, width), F32),
            pltpu.VMEM((tm, width), BF16),
            pltpu.VMEM((d, 2 * width), BF16),
            pltpu.VMEM((width, d), BF16),
            pltpu.VMEM((SGU_STAGE_DEPTH, SGU_STAGE_ROWS[0], 2 * width), F32),
            pltpu.VMEM((SGU_STAGE_DEPTH, SGU_STAGE_ROWS[1], d), F32),
            pltpu.SemaphoreType.DMA((SGU_STAGE_DEPTH,)),
        ],
        compiler_params=pltpu.CompilerParams(
            dimension_semantics=("arbitrary",), vmem_limit_bytes=VMEM_LIMIT_BYTES),
        name="sgu",
    )(x2, g, win, lng, lnb, ws, bst, wout, *[w for w, _ in cast])


def _ref_rows(G, half):
    n, w = G.shape
    blk = 2 * half
    if blk >= 8:
        return jnp.concatenate(
            [jnp.broadcast_to(G[b * blk + half - 1:b * blk + half, :], (blk, w))
             for b in range(n // blk)], axis=0)
    assert blk == 4
    sub = lax.broadcasted_iota(jnp.int32, (8, w), 0)
    tiles = []
    for t in range(n // 8):
        lo = jnp.broadcast_to(G[8 * t + 1:8 * t + 2, :], (8, w))
        hi = jnp.broadcast_to(G[8 * t + 5:8 * t + 6, :], (8, w))
        tiles.append(jnp.where(sub < 4, lo, hi))
    return jnp.concatenate(tiles, axis=0)


def _gla_project(x_ref, row0, tm, g_ref, win_ref, wg2_ref, bg_ref, proj_ref, la_ref, flag_ref, flag_idx):
    state = {}
    ncol = proj_ref.shape[1]
    pr, pc = GLA_PROJ_PIECE
    tasks = []

    def norm(r):
        def run():
            x = x_ref[row0 + r * pr:row0 + (r + 1) * pr, :]
            state[r] = _rms(x, g_ref[...]).astype(BF16)
            if r == 0:
                flag_ref[flag_idx] = jnp.float32(0.0)
        return run

    def cols(r, p):
        def run():
            proj_ref[r * pr:(r + 1) * pr, p * pc:(p + 1) * pc] = _dot(state[r], win_ref[:, p * pc:(p + 1) * pc])
        return run

    def gate(r):
        def run():
            gz = _dot(state[r], win_ref[:, ncol:])
            z = _dot(gz.astype(BF16), wg2_ref[...].astype(BF16)) + bg_ref[...]
            la = (jnp.minimum(z, 0.0) - jnp.log1p(jnp.exp(-jnp.abs(z)))) / GLA_TAU
            la_ref[r * pr:(r + 1) * pr, :] = la
            tot = [jnp.sum(la[c * CHUNK:(c + 1) * CHUNK, :], axis=0, keepdims=True) for c in range(pr // CHUNK)]
            flag_ref[flag_idx] = jnp.minimum(flag_ref[flag_idx], jnp.min(functools.reduce(jnp.minimum, tot)))
        return run

    for r in range(tm // pr):
        pieces = [cols(r, p) for p in range(ncol // pc)]
        half = len(pieces) // 2
        tasks += [norm(r)] + pieces[:half] + [gate(r)] + pieces[half:]
    return tasks


def _gla_chunks(proj_ref, la_ref, og_s, st_s, gn_ref, consts, between, *, tm, kd, vd, fast):
    dk = kd // GLA_HEADS
    dv = vd // GLA_HEADS
    C = CHUNK
    row_all, halves, lvl_mask, diag, upper, causal = consts
    scale = dk ** -0.5
    between = list(between)
    span = 2 if fast else 1
    B = span * C
    nblocks = tm // B
    nslots = nblocks * GLA_HEADS

    def chunk_decay(c):
        la_all = la_ref[c * C:(c + 1) * C, :]
        g = la_all
        shift = 1
        while shift < C:
            g = g + jnp.where(row_all >= shift, pltpu.roll(g, shift, axis=0), 0.0)
            shift *= 2
        return la_all, g

    def scores(b, hd, decays):
        rows = slice(b * B, (b + 1) * B)
        ks_ = slice(hd * dk, (hd + 1) * dk)
        q = proj_ref[rows, hd * dk:(hd + 1) * dk] * scale
        k = proj_ref[rows, kd + hd * dk:kd + (hd + 1) * dk]
        if fast:
            (_, ga0), (_, ga1) = decays
            g0, g1 = ga0[:, ks_], ga1[:, ks_]
            e0, e1 = g0[C - 1:C, :], g1[C - 1:C, :]
            q0, q1, k0, k1 = q[:C], q[C:], k[:C], k[C:]
            k0r = k0 * jnp.exp(e0 - g0)
            x1 = jnp.exp(g1)
            qd = jnp.concatenate([q0 * jnp.exp(g0), q1 * (x1 * jnp.exp(e0))], axis=0)
            qs = jnp.concatenate([q0 * jnp.exp(g0 - e0), q1 * x1], axis=0)
            kr = jnp.concatenate([k0r, k1 * jnp.exp(-g1)], axis=0)
            kdec = jnp.concatenate([k0r * jnp.exp(e1), k1 * jnp.exp(e1 - g1)], axis=0)
            return ([_dot_nt(qs.astype(BF16), kr.astype(BF16))], qd.astype(BF16), kdec.astype(BF16),
                    jnp.exp(e0 + e1))
        (la_all, g_all), = decays
        la = la_all[:, ks_]
        G = g_all[:, ks_]
        g_last = G[C - 1:C, :]
        qd = (q * jnp.exp(G)).astype(BF16)
        kdec = (k * jnp.exp(g_last - G)).astype(BF16)
        dec = jnp.exp(g_last)
        ps = [jnp.sum(q * k, axis=-1, keepdims=True)]
        for lvl, hf in enumerate(halves):
            if hf == 1:
                e = jnp.exp(jnp.where(upper[lvl], la, 0.0))
            else:
                e = jnp.exp(-jnp.abs(G - _ref_rows(G, hf)))
            xl = (jnp.where(upper[lvl], q, k) * e).astype(BF16)
            ps.append(_dot_nt(xl, xl))
        return ps, qd, kdec, dec

    def outputs(b, hd, ps, qd, kdec, dec):
        rows = slice(b * B, (b + 1) * B)
        vs_ = slice(2 * kd + hd * dv, 2 * kd + (hd + 1) * dv)
        rs_ = slice(2 * kd + vd + hd * dv, 2 * kd + vd + (hd + 1) * dv)
        if fast:
            a = jnp.where(causal, ps[0], 0.0)
        else:
            a = jnp.where(diag, ps[0], 0.0)
            for lvl in range(len(halves)):
                a = jnp.where(lvl_mask[lvl], ps[lvl + 1], a)
        v = proj_ref[rows, vs_].astype(BF16)
        st = st_s[hd]
        lhs = jnp.concatenate([qd, a.astype(BF16)], axis=1)
        rhs = jnp.concatenate([st.astype(BF16), v], axis=0)
        o = _dot(lhs, rhs)
        dcol = jnp.transpose(jnp.broadcast_to(dec, (dk, dk)))
        st_s[hd] = st * jnp.concatenate([dcol] * (dv // dk), axis=1) + _dot_tn(kdec, v)
        o = o * lax.rsqrt(jnp.mean(o * o, axis=-1, keepdims=True) + EPS) * gn_ref[hd:hd + 1, :]
        r = proj_ref[rows, rs_]
        og_s[rows, hd * dv:(hd + 1) * dv] = (jax.nn.silu(r) * o).astype(BF16)

    block_decay = lambda b: [chunk_decay(b * span + j) for j in range(span)]
    decay = {0: block_decay(0)}
    pending = []
    for slot in range(nslots):
        b, hd = divmod(slot, GLA_HEADS)
        if hd == 0 and b + 1 < nblocks:
            decay[b + 1] = block_decay(b + 1)
        pending.append((b, hd) + scores(b, hd, decay[b]))
        for task in between[slot * len(between) // nslots:(slot + 1) * len(between) // nslots]:
            task()
        if len(pending) > GLA_SKEW:
            outputs(*pending.pop(0))
    for item in pending:
        outputs(*item)


def _gla_kernel(xc_ref, xn_ref, g_ref, win_ref, wg2_ref, bg_ref, gn_ref, wout_ref,
                *rest, tm, kd, vd, tiles_per_seq, ncast):
    o_ref = rest[ncast]
    pa_s, la_a, pb_s, la_b, og_s, st_s, flag_s = rest[2 * ncast + 1:]
    _run_casts(rest[:ncast], rest[ncast + 1:2 * ncast + 1])
    C = CHUNK
    step = pl.program_id(0)
    wts = (g_ref, win_ref, wg2_ref, bg_ref)

    @pl.when(step == 0)
    def _():
        for task in _gla_project(xc_ref, 0, tm, *wts, pa_s, la_a, flag_s, 0):
            task()

    @pl.when((2 * step) % tiles_per_seq == 0)
    def _():
        st_s[...] = jnp.zeros_like(st_s)

    ri = lax.broadcasted_iota(jnp.int32, (C, C), 0)
    ci = lax.broadcasted_iota(jnp.int32, (C, C), 1)
    row_all = lax.broadcasted_iota(jnp.int32, (C, kd), 0)
    halves = (32, 16, 8, 4, 2, 1)
    lvl_mask = [((ri // (2 * hf)) == (ci // (2 * hf))) & ((ri % (2 * hf)) >= hf) & ((ci % (2 * hf)) < hf)
                for hf in halves]
    diag = ri == ci
    rowc = lax.broadcasted_iota(jnp.int32, (C, kd // GLA_HEADS), 0)
    upper = [(rowc % (2 * hf)) >= hf for hf in halves]
    bi = lax.broadcasted_iota(jnp.int32, (2 * C, 2 * C), 0)
    bj = lax.broadcasted_iota(jnp.int32, (2 * C, 2 * C), 1)
    consts = (row_all, halves, lvl_mask, diag, upper, bi >= bj)

    def tile(proj_ref, la_ref, og_ref, flag_idx, nxt_x, nxt_row0, nxt_proj, nxt_la, nxt_flag):
        def run(fast):
            def body():
                nxt = _gla_project(nxt_x, nxt_row0, tm, *wts, nxt_proj, nxt_la, flag_s, nxt_flag)
                _gla_chunks(proj_ref, la_ref, og_ref, st_s, gn_ref, consts, nxt, tm=tm, kd=kd, vd=vd, fast=fast)
            return body
        mild = flag_s[flag_idx] > -GLA_FAST_RANGE
        pl.when(mild)(run(True))
        pl.when(jnp.logical_not(mild))(run(False))

    tile(pa_s, la_a, og_s.at[pl.ds(0, tm)], 0, xc_ref, tm, pb_s, la_b, 1)
    tile(pb_s, la_b, og_s.at[pl.ds(tm, tm)], 1, xn_ref, 0, pa_s, la_a, 0)
    o_ref[...] = xc_ref[...] + _dot(og_s[...], wout_ref[...])


def _gla(x2, g, win, wg2, bg, gn, wout, *, tm, seq_len, cast=()):
    m, d = x2.shape
    kd = wg2.shape[1]
    vd = wout.shape[0]
    dk = kd // GLA_HEADS
    dv = vd // GLA_HEADS
    nproj = win.shape[1] - wg2.shape[0]
    ntiles = m // tm
    assert seq_len % (2 * tm) == 0 and nproj % GLA_PROJ_PIECE[1] == 0 and tm % GLA_PROJ_PIECE[0] == 0
    c_in, c_out, c_shapes = _cast_through(cast, ntiles // 2)
    return pl.pallas_call(
        functools.partial(_gla_kernel, tm=tm, kd=kd, vd=vd, tiles_per_seq=seq_len // tm, ncast=len(cast)),
        grid=(ntiles // 2,),
        in_specs=[
            pl.BlockSpec((2 * tm, d), lambda i: (i, 0)),
            pl.BlockSpec((tm, d), lambda i: (jnp.minimum(2 * i + 2, ntiles - 1), 0)),
            _resident((1, d)),
            _resident(win.shape),
            _resident(wg2.shape),
            _resident((1, kd)),
            _resident(gn.shape),
            _resident((vd, d)),
        ] + c_in,
        out_specs=[pl.BlockSpec((2 * tm, d), lambda i: (i, 0))] + c_out,
        out_shape=[jax.ShapeDtypeStruct((m, d), F32)] + c_shapes,
        scratch_shapes=[
            pltpu.VMEM((tm, nproj), F32),
            pltpu.VMEM((tm, kd), F32),
            pltpu.VMEM((tm, nproj), F32),
            pltpu.VMEM((tm, kd), F32),
            pltpu.VMEM((2 * tm, vd), BF16),
            pltpu.VMEM((GLA_HEADS, dk, dv), F32),
            pltpu.SMEM((2,), F32),
        ],
        compiler_params=pltpu.CompilerParams(
            dimension_semantics=("arbitrary",), vmem_limit_bytes=VMEM_LIMIT_BYTES),
        name="gla",
    )(x2, x2, g, win, wg2, bg, gn, wout, *[w for w, _ in cast])


def kernel(x, mix_norm, a_w_in, a_ln_g, a_ln_b, a_w_s, a_b_s, a_w_out, b_w_in, b_w_g2, b_b_g, b_gn_g, b_w_out, ffn_norm, ffn_w_gate, ffn_w_up, ffn_w_down, final_norm):
    b, t, d = x.shape
    depth = mix_norm.shape[0]
    assert depth == 2 and a_w_in.shape[0] == 1 and b_w_in.shape[0] == 1
    m = b * t
    bf = lambda w: w.astype(BF16)
    row = lambda v: v.reshape(1, -1)

    x2 = x.reshape(m, d)
    x2, wg0, wu0, wd0 = _sgu(
        x2, row(mix_norm[0]), a_w_in, row(a_ln_g[0]), row(a_ln_b[0]),
        a_w_s[0], a_b_s[0].T, a_w_out, tm=SGU_TM,
        cast=((ffn_w_gate, 0), (ffn_w_up, 0), (ffn_w_down, 0)))
    x2, b_wout = _ffn(
        x2, row(ffn_norm[0]), wg0, wu0, wd0, row(final_norm), final_norm=False, tm=FFN_TM,
        cast=((b_w_out, 0),))
    x2, wg1, wu1, wd1 = _gla(
        x2, row(mix_norm[1]), bf(b_w_in[0]), b_w_g2[0], row(b_b_g[0]), b_gn_g[0], b_wout, tm=GLA_TM, seq_len=t,
        cast=((ffn_w_gate, 1), (ffn_w_up, 1), (ffn_w_down, 1)))
    (x2,) = _ffn(x2, row(ffn_norm[1]), wg1, wu1, wd1, row(final_norm), final_norm=True, tm=FFN_TM)
    return x2.reshape(b, t, d)
```

```python
import functools

import jax
import jax.numpy as jnp
from jax import lax
from jax.experimental import pallas as pl
from jax.experimental.pallas import tpu as pltpu

F32 = jnp.float32
BF16 = jnp.bfloat16

EPS = 1e-6
CHUNK = 64
SG_BLOCK = 128
SG_GROUPS = 8
SG_GROUPS_PER_CHUNK = 2
GLA_HEADS = 4
GLA_TAU = 16.0
GLA_FAST_RANGE = 60.0
GLA_SKEW = 1
GLA_PROJ_PIECE = (256, 512)

BF16_SUBLANES = 16
MXU_N = 256
FFN_HIDDEN_CHUNKS = 4

SGU_TM = 512
SGU_STAGE_ROWS = (16, 96)
SGU_STAGE_DEPTH = 8
FFN_TM = 1024
GLA_TM = 256

VMEM_LIMIT_BYTES = 56 * 1024 * 1024


def _rms(x, g):
    return x * lax.rsqrt(jnp.mean(x * x, axis=-1, keepdims=True) + EPS) * g


def _gelu(x):
    return 0.5 * x * (1.0 + lax.erf(x * (2.0 ** -0.5)))


def _dot(a, b):
    return jnp.dot(a, b, preferred_element_type=F32)


def _dot_nt(a, b):
    return lax.dot_general(a, b, (((1,), (1,)), ((), ())), preferred_element_type=F32)


def _dot_tn(a, b):
    return lax.dot_general(a, b, (((0,), (0,)), ((), ())), preferred_element_type=F32)


def _resident(shape):
    return pl.BlockSpec(shape, lambda *_: (0,) * len(shape))


def _cast_through(weights, nsteps):
    in_specs, out_specs, out_shapes = [], [], []
    for w, layer, *flags in weights:
        if flags:
            _, c, r = w.shape
            last = -(-c // MXU_N) - 1
            assert last < nsteps, (c, nsteps)
            in_specs.append(pl.BlockSpec((None, MXU_N, r), lambda i, layer=layer, last=last: (layer, jnp.minimum(i, last), 0)))
            out_specs.append(pl.BlockSpec((r, MXU_N), lambda i, last=last: (0, jnp.minimum(i, last))))
            out_shapes.append(jax.ShapeDtypeStruct((r, c), BF16))
            continue
        _, r, c = w.shape
        per = 1 if r % (nsteps * BF16_SUBLANES) == 0 else 2
        rb = r * per // nsteps
        assert rb % BF16_SUBLANES == 0 and rb * nsteps == r * per, (r, c, nsteps)
        in_specs.append(pl.BlockSpec((None, rb, c), lambda i, layer=layer, per=per: (layer, i // per, 0)))
        out_specs.append(pl.BlockSpec((rb, c), lambda i, per=per: (i // per, 0)))
        out_shapes.append(jax.ShapeDtypeStruct((r, c), BF16))
    return in_specs, out_specs, out_shapes


def _run_casts(src_refs, dst_refs):
    for src, dst in zip(src_refs, dst_refs):
        w = src[...]
        dst[...] = (w if w.shape == dst.shape else w.T).astype(BF16)


def _ffn_kernel(x_ref, g_ref, wg_ref, wu_ref, wd_ref, fin_ref, *rest, final_norm, ncast):
    o_ref = rest[ncast]
    _run_casts(rest[:ncast], rest[ncast + 1:])
    x = x_ref[...]
    h = _rms(x, g_ref[...]).astype(BF16)
    hid = wg_ref.shape[1]
    ntiles = hid // MXU_N
    bounds = [MXU_N * (ntiles * j // FFN_HIDDEN_CHUNKS) for j in range(FFN_HIDDEN_CHUNKS + 1)]
    y = x
    for c0, c1 in zip(bounds[:-1], bounds[1:]):
        gate = _dot(h, wg_ref[:, c0:c1])
        up = _dot(h, wu_ref[:, c0:c1])
        a = (jax.nn.silu(gate) * up).astype(BF16)
        y = y + _dot(a, wd_ref[c0:c1, :])
    if final_norm:
        y = _rms(y, fin_ref[...])
    o_ref[...] = y


def _ffn(x2, g, wg, wu, wd, fin, *, final_norm, tm, cast=()):
    m, d = x2.shape
    hid = wg.shape[1]
    c_in, c_out, c_shapes = _cast_through(cast, m // tm)
    return pl.pallas_call(
        functools.partial(_ffn_kernel, final_norm=final_norm, ncast=len(cast)),
        grid=(m // tm,),
        in_specs=[
            pl.BlockSpec((tm, d), lambda i: (i, 0)),
            _resident((1, d)),
            _resident((d, hid)),
            _resident((d, hid)),
            _resident((hid, d)),
            _resident((1, d)),
        ] + c_in,
        out_specs=[pl.BlockSpec((tm, d), lambda i: (i, 0))] + c_out,
        out_shape=[jax.ShapeDtypeStruct((m, d), F32)] + c_shapes,
        compiler_params=pltpu.CompilerParams(
            dimension_semantics=("arbitrary",), vmem_limit_bytes=VMEM_LIMIT_BYTES),
        name="swiglu",
    )(x2, g, wg, wu, wd, fin, *[c[0] for c in cast])


def _load_cast(w_hbm, w_s, stage_s, sem):
    depth, rb = stage_s.shape[0], stage_s.shape[1]
    n = w_hbm.shape[0] // rb

    def copy(i, slot):
        return pltpu.make_async_copy(w_hbm.at[pl.ds(i * rb, rb)], stage_s.at[slot], sem.at[slot])

    for i in range(depth - 1):
        copy(i, i).start()

    def body(i, carry):
        slot = i % depth

        @pl.when(i + depth - 1 < n)
        def _():
            copy(i + depth - 1, (i + depth - 1) % depth).start()

        copy(i, slot).wait()
        w_s[pl.ds(pl.multiple_of(i * rb, rb), rb), :] = stage_s[slot].astype(BF16)
        return carry

    lax.fori_loop(0, n, body, 0)


def _sgu_kernel(x_ref, g_ref, win_hbm, lng_ref, lnb_ref, ws_ref, bst_ref, wout_hbm,
                *rest, tm, ncast):
    o_ref = rest[ncast]
    v_s, y_s, win_ref, wout_ref, stage_in, stage_out, sem = rest[2 * ncast + 1:]
    _run_casts(rest[:ncast], rest[ncast + 1:2 * ncast + 1])

    @pl.when(pl.program_id(0) == 0)
    def _():
        _load_cast(win_hbm.at[0], win_ref, stage_in, sem)
        _load_cast(wout_hbm.at[0], wout_ref, stage_out, sem)

    width = v_s.shape[1]
    gc = width // SG_GROUPS
    lanes = 128
    x = x_ref[...]
    h = _rms(x, g_ref[...]).astype(BF16)

    gpc = SG_GROUPS_PER_CHUNK
    cw = gpc * gc
    p1 = jnp.zeros((tm, lanes), F32)
    p2 = jnp.zeros((tm, lanes), F32)
    for c in range(SG_GROUPS // gpc):
        v = _gelu(_dot(h, win_ref[:, width + c * cw:width + (c + 1) * cw]))
        v_s[:, c * cw:(c + 1) * cw] = v
        for l in range(cw // lanes):
            vl = v[:, l * lanes:(l + 1) * lanes]
            p1 = p1 + vl
            p2 = p2 + vl * vl
    mu = jnp.sum(p1, axis=-1, keepdims=True) * (1.0 / width)
    var = jnp.sum(p2, axis=-1, keepdims=True) * (1.0 / width) - mu * mu
    rstd = lax.rsqrt(var + EPS)
    shift = -mu * rstd

    ti = lax.broadcasted_iota(jnp.int32, (SG_BLOCK, SG_BLOCK), 0) // CHUNK
    si = lax.broadcasted_iota(jnp.int32, (SG_BLOCK, SG_BLOCK), 1) // CHUNK
    causal = ti >= si
    for c in range(SG_GROUPS // gpc):
        ccols = slice(c * cw, (c + 1) * cw)
        u = _gelu(_dot(h, win_ref[:, ccols]))
        vn = ((v_s[:, ccols] * rstd + shift) * lng_ref[:, ccols] + lnb_ref[:, ccols]).astype(BF16)
        for gg in range(gpc):
            g = c * gpc + gg
            gcols = slice(gg * gc, (gg + 1) * gc)
            ws = jnp.where(causal, ws_ref[g], 0.0).astype(BF16)
            bias = bst_ref[:, g:g + 1]
            for w in range(tm // SG_BLOCK):
                rows = slice(w * SG_BLOCK, (w + 1) * SG_BLOCK)
                s = _dot(ws, vn[rows, gcols]) + bias
                y_s[rows, g * gc:(g + 1) * gc] = (u[rows, gcols] * s).astype(BF16)
    o_ref[...] = x + _dot(y_s[...], wout_ref[...])


def _sgu(x2, g, win, lng, lnb, ws, bst, wout, *, tm, cast=()):
    m, d = x2.shape
    width = wout.shape[1]
    assert d % SGU_STAGE_ROWS[0] == 0 and width % SGU_STAGE_ROWS[1] == 0
    c_in, c_out, c_shapes = _cast_through(cast, m // tm)
    return pl.pallas_call(
        functools.partial(_sgu_kernel, tm=tm, ncast=len(cast)),
        grid=(m // tm,),
        in_specs=[
            pl.BlockSpec((tm, d), lambda i: (i, 0)),
            _resident((1, d)),
            pl.BlockSpec(memory_space=pl.ANY),
            _resident((1, width)),
            _resident((1, width)),
            _resident(ws.shape),
            _resident(bst.shape),
            pl.BlockSpec(memory_space=pl.ANY),
        ] + c_in,
        out_specs=[pl.BlockSpec((tm, d), lambda i: (i, 0))] + c_out,
        out_shape=[jax.ShapeDtypeStruct((m, d), F32)] + c_shapes,
        scratch_shapes=[
            pltpu.VMEM((tm, width), F32),
            pltpu.VMEM((tm, width), BF16),
            pltpu.VMEM((d, 2 * width), BF16),
            pltpu.VMEM((width, d), BF16),
            pltpu.VMEM((SGU_STAGE_DEPTH, SGU_STAGE_ROWS[0], 2 * width), F32),
            pltpu.VMEM((SGU_STAGE_DEPTH, SGU_STAGE_ROWS[1], d), F32),
            pltpu.SemaphoreType.DMA((SGU_STAGE_DEPTH,)),
        ],
        compiler_params=pltpu.CompilerParams(
            dimension_semantics=("arbitrary",), vmem_limit_bytes=VMEM_LIMIT_BYTES),
        name="sgu",
    )(x2, g, win, lng, lnb, ws, bst, wout, *[c[0] for c in cast])


def _ref_rows(G, half):
    n, w = G.shape
    blk = 2 * half
    if blk >= 8:
        return jnp.concatenate(
            [jnp.broadcast_to(G[b * blk + half - 1:b * blk + half, :], (blk, w))
             for b in range(n // blk)], axis=0)
    assert blk == 4
    sub = lax.broadcasted_iota(jnp.int32, (8, w), 0)
    tiles = []
    for t in range(n // 8):
        lo = jnp.broadcast_to(G[8 * t + 1:8 * t + 2, :], (8, w))
        hi = jnp.broadcast_to(G[8 * t + 5:8 * t + 6, :], (8, w))
        tiles.append(jnp.where(sub < 4, lo, hi))
    return jnp.concatenate(tiles, axis=0)


def _gla_project(x_ref, row0, tm, g_ref, win_ref, wg2_ref, bg_ref, proj_ref, la_ref, flag_ref, flag_idx):
    state = {}
    ncol = proj_ref.shape[1]
    pr, pc = GLA_PROJ_PIECE
    tasks = []

    def norm(r):
        def run():
            x = x_ref[row0 + r * pr:row0 + (r + 1) * pr, :]
            state[r] = _rms(x, g_ref[...]).astype(BF16)
            if r == 0:
                flag_ref[flag_idx] = jnp.float32(0.0)
        return run

    def cols(r, p):
        def run():
            proj_ref[r * pr:(r + 1) * pr, p * pc:(p + 1) * pc] = _dot(state[r], win_ref[:, p * pc:(p + 1) * pc])
        return run

    def gate(r):
        def run():
            gz = _dot(state[r], win_ref[:, ncol:])
            z = _dot(gz.astype(BF16), wg2_ref[...].astype(BF16)) + bg_ref[...]
            la = (jnp.minimum(z, 0.0) - jnp.log1p(jnp.exp(-jnp.abs(z)))) / GLA_TAU
            la_ref[r * pr:(r + 1) * pr, :] = la
            tot = [jnp.sum(la[c * CHUNK:(c + 1) * CHUNK, :], axis=0, keepdims=True) for c in range(pr // CHUNK)]
            flag_ref[flag_idx] = jnp.minimum(flag_ref[flag_idx], jnp.min(functools.reduce(jnp.minimum, tot)))
        return run

    for r in range(tm // pr):
        pieces = [cols(r, p) for p in range(ncol // pc)]
        half = len(pieces) // 2
        tasks += [norm(r)] + pieces[:half] + [gate(r)] + pieces[half:]
    return tasks


def _gla_chunks(proj_ref, la_ref, og_s, st_s, gn_ref, consts, between, *, tm, kd, vd, fast):
    dk = kd // GLA_HEADS
    dv = vd // GLA_HEADS
    C = CHUNK
    row_all, halves, lvl_mask, diag, upper, causal = consts
    scale = dk ** -0.5
    between = list(between)
    span = 2 if fast else 1
    B = span * C
    nblocks = tm // B
    nslots = nblocks * GLA_HEADS

    def chunk_decay(c):
        la_all = la_ref[c * C:(c + 1) * C, :]
        g = la_all
        shift = 1
        while shift < C:
            g = g + jnp.where(row_all >= shift, pltpu.roll(g, shift, axis=0), 0.0)
            shift *= 2
        return la_all, g

    def scores(b, hd, decays):
        rows = slice(b * B, (b + 1) * B)
        ks_ = slice(hd * dk, (hd + 1) * dk)
        q = proj_ref[rows, hd * dk:(hd + 1) * dk] * scale
        k = proj_ref[rows, kd + hd * dk:kd + (hd + 1) * dk]
        if fast:
            (_, ga0), (_, ga1) = decays
            g0, g1 = ga0[:, ks_], ga1[:, ks_]
            e0, e1 = g0[C - 1:C, :], g1[C - 1:C, :]
            q0, q1, k0, k1 = q[:C], q[C:], k[:C], k[C:]
            k0r = k0 * jnp.exp(e0 - g0)
            x1 = jnp.exp(g1)
            qd = jnp.concatenate([q0 * jnp.exp(g0), q1 * (x1 * jnp.exp(e0))], axis=0)
            qs = jnp.concatenate([q0 * jnp.exp(g0 - e0), q1 * x1], axis=0)
            kr = jnp.concatenate([k0r, k1 * jnp.exp(-g1)], axis=0)
            kdec = jnp.concatenate([k0r * jnp.exp(e1), k1 * jnp.exp(e1 - g1)], axis=0)
            return ([_dot_nt(qs.astype(BF16), kr.astype(BF16))], qd.astype(BF16), kdec.astype(BF16),
                    jnp.exp(e0 + e1))
        (la_all, g_all), = decays
        la = la_all[:, ks_]
        G = g_all[:, ks_]
        g_last = G[C - 1:C, :]
        qd = (q * jnp.exp(G)).astype(BF16)
        kdec = (k * jnp.exp(g_last - G)).astype(BF16)
        dec = jnp.exp(g_last)
        ps = [jnp.sum(q * k, axis=-1, keepdims=True)]
        for lvl, hf in enumerate(halves):
            if hf == 1:
                e = jnp.exp(jnp.where(upper[lvl], la, 0.0))
            else:
                e = jnp.exp(-jnp.abs(G - _ref_rows(G, hf)))
            xl = (jnp.where(upper[lvl], q, k) * e).astype(BF16)
            ps.append(_dot_nt(xl, xl))
        return ps, qd, kdec, dec

    def outputs(b, hd, ps, qd, kdec, dec):
        rows = slice(b * B, (b + 1) * B)
        vs_ = slice(2 * kd + hd * dv, 2 * kd + (hd + 1) * dv)
        rs_ = slice(2 * kd + vd + hd * dv, 2 * kd + vd + (hd + 1) * dv)
        if fast:
            a = jnp.where(causal, ps[0], 0.0)
        else:
            a = jnp.where(diag, ps[0], 0.0)
            for lvl in range(len(halves)):
                a = jnp.where(lvl_mask[lvl], ps[lvl + 1], a)
        v = proj_ref[rows, vs_].astype(BF16)
        st = st_s[hd]
        lhs = jnp.concatenate([qd, a.astype(BF16)], axis=1)
        rhs = jnp.concatenate([st.astype(BF16), v], axis=0)
        o = _dot(lhs, rhs)
        dcol = jnp.transpose(jnp.broadcast_to(dec, (dk, dk)))
        st_s[hd] = st * jnp.concatenate([dcol] * (dv // dk), axis=1) + _dot_tn(kdec, v)
        o = o * lax.rsqrt(jnp.mean(o * o, axis=-1, keepdims=True) + EPS) * gn_ref[hd:hd + 1, :]
        r = proj_ref[rows, rs_]
        og_s[rows, hd * dv:(hd + 1) * dv] = (jax.nn.silu(r) * o).astype(BF16)

    block_decay = lambda b: [chunk_decay(b * span + j) for j in range(span)]
    decay = {0: block_decay(0)}
    pending = []
    for slot in range(nslots):
        b, hd = divmod(slot, GLA_HEADS)
        if hd == 0 and b + 1 < nblocks:
            decay[b + 1] = block_decay(b + 1)
        pending.append((b, hd) + scores(b, hd, decay[b]))
        for task in between[slot * len(between) // nslots:(slot + 1) * len(between) // nslots]:
            task()
        if len(pending) > GLA_SKEW:
            outputs(*pending.pop(0))
    for item in pending:
        outputs(*item)


def _gla_kernel(xc_ref, xn_ref, g_ref, win_ref, wg2_ref, bg_ref, gn_ref, wout_ref,
                *rest, tm, kd, vd, tiles_per_seq, ncast):
    o_ref = rest[ncast]
    pa_s, la_a, pb_s, la_b, og_s, st_s, flag_s = rest[2 * ncast + 1:]
    _run_casts(rest[:ncast], rest[ncast + 1:2 * ncast + 1])
    C = CHUNK
    step = pl.program_id(0)
    wts = (g_ref, win_ref, wg2_ref, bg_ref)

    @pl.when(step == 0)
    def _():
        for task in _gla_project(xc_ref, 0, tm, *wts, pa_s, la_a, flag_s, 0):
            task()

    @pl.when((2 * step) % tiles_per_seq == 0)
    def _():
        st_s[...] = jnp.zeros_like(st_s)

    ri = lax.broadcasted_iota(jnp.int32, (C, C), 0)
    ci = lax.broadcasted_iota(jnp.int32, (C, C), 1)
    row_all = lax.broadcasted_iota(jnp.int32, (C, kd), 0)
    halves = (32, 16, 8, 4, 2, 1)
    lvl_mask = [((ri // (2 * hf)) == (ci // (2 * hf))) & ((ri % (2 * hf)) >= hf) & ((ci % (2 * hf)) < hf)
                for hf in halves]
    diag = ri == ci
    rowc = lax.broadcasted_iota(jnp.int32, (C, kd // GLA_HEADS), 0)
    upper = [(rowc % (2 * hf)) >= hf for hf in halves]
    bi = lax.broadcasted_iota(jnp.int32, (2 * C, 2 * C), 0)
    bj = lax.broadcasted_iota(jnp.int32, (2 * C, 2 * C), 1)
    consts = (row_all, halves, lvl_mask, diag, upper, bi >= bj)

    def tile(proj_ref, la_ref, og_ref, flag_idx, nxt_x, nxt_row0, nxt_proj, nxt_la, nxt_flag):
        def run(fast):
            def body():
                nxt = _gla_project(nxt_x, nxt_row0, tm, *wts, nxt_proj, nxt_la, flag_s, nxt_flag)
                _gla_chunks(proj_ref, la_ref, og_ref, st_s, gn_ref, consts, nxt, tm=tm, kd=kd, vd=vd, fast=fast)
            return body
        mild = flag_s[flag_idx] > -GLA_FAST_RANGE
        pl.when(mild)(run(True))
        pl.when(jnp.logical_not(mild))(run(False))

    tile(pa_s, la_a, og_s.at[pl.ds(0, tm)], 0, xc_ref, tm, pb_s, la_b, 1)
    tile(pb_s, la_b, og_s.at[pl.ds(tm, tm)], 1, xn_ref, 0, pa_s, la_a, 0)
    o_ref[...] = xc_ref[...] + _dot(og_s[...], wout_ref[...])


def _gla(x2, g, win, wg2, bg, gn, wout, *, tm, seq_len, cast=()):
    m, d = x2.shape
    kd = wg2.shape[1]
    vd = wout.shape[0]
    dk = kd // GLA_HEADS
    dv = vd // GLA_HEADS
    nproj = win.shape[1] - wg2.shape[0]
    ntiles = m // tm
    assert seq_len % (2 * tm) == 0 and nproj % GLA_PROJ_PIECE[1] == 0 and tm % GLA_PROJ_PIECE[0] == 0
    c_in, c_out, c_shapes = _cast_through(cast, ntiles // 2)
    return pl.pallas_call(
        functools.partial(_gla_kernel, tm=tm, kd=kd, vd=vd, tiles_per_seq=seq_len // tm, ncast=len(cast)),
        grid=(ntiles // 2,),
        in_specs=[
            pl.BlockSpec((2 * tm, d), lambda i: (i, 0)),
            pl.BlockSpec((tm, d), lambda i: (jnp.minimum(2 * i + 2, ntiles - 1), 0)),
            _resident((1, d)),
            _resident(win.shape),
            _resident(wg2.shape),
            _resident((1, kd)),
            _resident(gn.shape),
            _resident((vd, d)),
        ] + c_in,
        out_specs=[pl.BlockSpec((2 * tm, d), lambda i: (i, 0))] + c_out,
        out_shape=[jax.ShapeDtypeStruct((m, d), F32)] + c_shapes,
        scratch_shapes=[
            pltpu.VMEM((tm, nproj), F32),
            pltpu.VMEM((tm, kd), F32),
            pltpu.VMEM((tm, nproj), F32),
            pltpu.VMEM((tm, kd), F32),
            pltpu.VMEM((2 * tm, vd), BF16),
            pltpu.VMEM((GLA_HEADS, dk, dv), F32),
            pltpu.SMEM((2,), F32),
        ],
        compiler_params=pltpu.CompilerParams(
            dimension_semantics=("arbitrary",), vmem_limit_bytes=VMEM_LIMIT_BYTES),
        name="gla",
    )(x2, x2, g, win, wg2, bg, gn, wout, *[c[0] for c in cast])


def kernel(x, mix_norm, a_w_in, a_ln_g, a_ln_b, a_w_s, a_b_s, a_w_out, b_w_in, b_w_g2, b_b_g, b_gn_g, b_w_out, ffn_norm, ffn_w_gate, ffn_w_up, ffn_w_down, final_norm):
    b, t, d = x.shape
    depth = mix_norm.shape[0]
    assert depth == 2 and a_w_in.shape[0] == 1 and b_w_in.shape[0] == 1
    m = b * t
    bf = lambda w: w.astype(BF16)
    row = lambda v: v.reshape(1, -1)

    x2 = x.reshape(m, d)
    x2, wg0, wu0, wd0 = _sgu(
        x2, row(mix_norm[0]), a_w_in, row(a_ln_g[0]), row(a_ln_b[0]),
        a_w_s[0], a_b_s[0].T, a_w_out, tm=SGU_TM,
        cast=((ffn_w_gate, 0), (ffn_w_up, 0), (ffn_w_down, 0)))
    x2, b_win, b_wout = _ffn(
        x2, row(ffn_norm[0]), wg0, wu0, wd0, row(final_norm), final_norm=False, tm=FFN_TM,
        cast=((jnp.swapaxes(b_w_in, 1, 2), 0, "T"), (b_w_out, 0)))
    x2, wg1, wu1, wd1 = _gla(
        x2, row(mix_norm[1]), b_win, b_w_g2[0], row(b_b_g[0]), b_gn_g[0], b_wout, tm=GLA_TM, seq_len=t,
        cast=((ffn_w_gate, 1), (ffn_w_up, 1), (ffn_w_down, 1)))
    (x2,) = _ffn(x2, row(ffn_norm[1]), wg1, wu1, wd1, row(final_norm), final_norm=True, tm=FFN_TM)
    return x2.reshape(b, t, d)
```
